```python
import math
import jax, jax.numpy as jnp
from jax import lax
import numpy as np

D_MODEL = 1024
BATCH = 8
SEQ = 4096
DEPTH = 1

HEAD_DIM = 64
N_SLOTS = 8
GROUP_PATTERNS = ((128, 1), (512, 4), (2048, 16))
N_ATTN_GROUPS = len(GROUP_PATTERNS)
ATTN_WIDTH = N_SLOTS * HEAD_DIM
QKV_WIDTH = N_ATTN_GROUPS * ATTN_WIDTH
BLOCK = 128
SSM_WIDTH = 512
SSM_GROUP = 16
SSM_GROUPS = SSM_WIDTH // SSM_GROUP
SSM_STATE = 64
DT_MIN = 1e-3
DT_MAX = 1e-1
N_BRANCHES = 2
D_IN = 3 * QKV_WIDTH + ATTN_WIDTH + 2 * SSM_WIDTH + N_BRANCHES * D_MODEL
DEEPNORM_ALPHA = (2.0 * DEPTH) ** 0.25
DEEPNORM_BETA = (8.0 * DEPTH) ** -0.25
LN_EPS = 1e-5

kernel_name = "hybrid_dilated_attn_s5_gated_deepnorm"


def _alibi_slopes(n):
    return 2.0 ** (-8.0 * jnp.arange(1, n + 1, dtype=jnp.float32) / n)


def _layer_norm(h, g, b):
    h = h.astype(jnp.float32)
    mu = h.mean(-1, keepdims=True)
    var = jnp.mean(jnp.square(h - mu), -1, keepdims=True)
    return (h - mu) * lax.rsqrt(var + LN_EPS) * g.astype(jnp.float32) + b.astype(jnp.float32)


def _dilated_group(q, k, v, window, dilation, slopes):
    b, L, h, dh = q.shape
    r = dilation
    span = r * BLOCK
    Lp = -(-L // span) * span
    n_sub = Lp // r
    nb = n_sub // BLOCK
    pad = ((0, 0), (0, Lp - L), (0, 0), (0, 0))

    def to_sub(t):
        t = jnp.pad(t, pad).reshape(b, n_sub, r, h, dh).transpose(0, 2, 3, 1, 4)
        return t.reshape(b, r, h, nb, BLOCK, dh)

    def band(t):
        prev = jnp.pad(t, ((0, 0), (0, 0), (0, 0), (1, 0), (0, 0), (0, 0)))[:, :, :, :-1]
        return jnp.concatenate([prev, t], axis=4)

    qs = to_sub(q)
    kb = band(to_sub(k))
    vb = band(to_sub(v))
    s = jnp.einsum('brhnqd,brhnkd->brhnqk', qs, kb,
                   preferred_element_type=jnp.float32) / math.sqrt(dh)
    qi = jnp.arange(BLOCK)[:, None]
    kj = jnp.arange(2 * BLOCK)[None, :]
    dist = BLOCK + qi - kj
    in_range = (jnp.arange(nb)[:, None, None] > 0) | (kj[None] >= BLOCK)
    valid = (dist >= 0) & (dist <= window // r) & in_range
    alibi = -slopes[:, None, None, None] * (dist * r).astype(jnp.float32)
    s = jnp.where(valid, s + alibi[None, None], -jnp.inf)
    m = s.max(-1)
    p = jnp.exp(s - m[..., None])
    l = p.sum(-1)
    o = jnp.einsum('brhnqk,brhnkd->brhnqd', p, vb.astype(jnp.float32)) / l[..., None]
    o = o.reshape(b, r, h, n_sub, dh).transpose(0, 3, 1, 2, 4).reshape(b, Lp, h, dh)[:, :L]
    m = m.reshape(b, r, h, n_sub).transpose(0, 3, 1, 2).reshape(b, Lp, h)[:, :L]
    l = l.reshape(b, r, h, n_sub).transpose(0, 3, 1, 2).reshape(b, Lp, h)[:, :L]
    return o, m, l


def _dilated_mixture(q, k, v):
    slopes = _alibi_slopes(N_SLOTS)
    outs, maxes, sums = [], [], []
    for g, (win, dil) in enumerate(GROUP_PATTERNS):
        o, m, l = _dilated_group(q[:, :, g], k[:, :, g], v[:, :, g], win, dil, slopes)
        outs.append(o)
        maxes.append(m)
        sums.append(l)
    m_all = jnp.stack(maxes)
    w = jnp.stack(sums) * jnp.exp(m_all - m_all.max(0, keepdims=True))
    return jnp.sum(w[..., None] * jnp.stack(outs), 0) / jnp.sum(w, 0)[..., None]


def _s5(u, lam_re, lam_im, log_dt, b_re, b_im, c_re, c_im, d_skip):
    f32 = jnp.float32
    bsz, L, _ = u.shape
    uf = u.astype(f32)
    ug = uf.reshape(bsz, L, SSM_GROUPS, SSM_GROUP)
    lr, li = lam_re.astype(f32), lam_im.astype(f32)
    dt = jnp.exp(log_dt.astype(f32))[:, None]
    mag = jnp.exp(lr * dt)
    ab_re, ab_im = mag * jnp.cos(li * dt), mag * jnp.sin(li * dt)
    nr, ni = ab_re - 1.0, ab_im
    den = lr * lr + li * li
    coef_re = (nr * lr + ni * li) / den
    coef_im = (ni * lr - nr * li) / den
    br, bi = b_re.astype(f32), b_im.astype(f32)
    bb_re = coef_re[..., None] * br - coef_im[..., None] * bi
    bb_im = coef_re[..., None] * bi + coef_im[..., None] * br
    bu_re = jnp.einsum('gpc,blgc->blgp', bb_re, ug)
    bu_im = jnp.einsum('gpc,blgc->blgp', bb_im, ug)
    a_re = jnp.broadcast_to(ab_re, bu_re.shape)
    a_im = jnp.broadcast_to(ab_im, bu_im.shape)

    def combine(e1, e2):
        a1r, a1i, b1r, b1i = e1
        a2r, a2i, b2r, b2i = e2
        return (a2r * a1r - a2i * a1i,
                a2r * a1i + a2i * a1r,
                a2r * b1r - a2i * b1i + b2r,
                a2r * b1i + a2i * b1r + b2i)

    _, _, xr, xi = lax.associative_scan(combine, (a_re, a_im, bu_re, bu_im), axis=1)
    y = (jnp.einsum('gcp,blgp->blgc', c_re.astype(f32), xr)
         - jnp.einsum('gcp,blgp->blgc', c_im.astype(f32), xi))
    return y.reshape(bsz, L, SSM_WIDTH) + d_skip.astype(f32) * uf


def setup_inputs(seed: int = 0) -> dict:
    key = jax.random.key(seed)
    ks = jax.random.split(key, 20)
    f32 = jnp.float32

    def nrm(k, shape, scale):
        return jax.random.normal(k, shape, f32) * scale

    x = jax.random.normal(ks[0], (BATCH, SEQ, D_MODEL), f32)
    w_in = nrm(ks[1], (DEPTH, D_MODEL, D_IN), D_MODEL ** -0.5)
    w_in = w_in.at[:, :, 2 * QKV_WIDTH:3 * QKV_WIDTH].multiply(DEEPNORM_BETA)
    lam_re = -0.5 + nrm(ks[2], (DEPTH, SSM_GROUPS, SSM_STATE), 0.01)
    lam_im = (math.pi * jnp.arange(SSM_STATE, dtype=f32))[None, None, :] + nrm(ks[3], (DEPTH, SSM_GROUPS, SSM_STATE), 0.01)
    log_dt = jax.random.uniform(ks[4], (DEPTH, SSM_GROUPS), f32, math.log(DT_MIN), math.log(DT_MAX))
    b_re = nrm(ks[5], (DEPTH, SSM_GROUPS, SSM_STATE, SSM_GROUP), (2.0 * SSM_GROUP) ** -0.5)
    b_im = nrm(ks[6], (DEPTH, SSM_GROUPS, SSM_STATE, SSM_GROUP), (2.0 * SSM_GROUP) ** -0.5)
    c_re = nrm(ks[7], (DEPTH, SSM_GROUPS, SSM_GROUP, SSM_STATE), (2.0 * SSM_STATE) ** -0.5)
    c_im = nrm(ks[8], (DEPTH, SSM_GROUPS, SSM_GROUP, SSM_STATE), (2.0 * SSM_STATE) ** -0.5)
    d_skip = nrm(ks[9], (DEPTH, SSM_WIDTH), 1.0)
    w_glu = nrm(ks[10], (DEPTH, SSM_WIDTH, SSM_WIDTH), SSM_WIDTH ** -0.5)
    b_glu = nrm(ks[11], (DEPTH, SSM_WIDTH), 0.01)
    w_attn_up = nrm(ks[12], (DEPTH, ATTN_WIDTH, D_MODEL), ATTN_WIDTH ** -0.5)
    w_ssm_up = nrm(ks[13], (DEPTH, SSM_WIDTH, D_MODEL), SSM_WIDTH ** -0.5)
    w_o = nrm(ks[14], (DEPTH, D_MODEL, D_MODEL), DEEPNORM_BETA * D_MODEL ** -0.5)
    ln_g = 1.0 + nrm(ks[15], (DEPTH, D_MODEL), 0.01)
    ln_b = nrm(ks[16], (DEPTH, D_MODEL), 0.01)
    return {"x": x, "w_in": w_in, "lam_re": lam_re, "lam_im": lam_im, "log_dt": log_dt,
            "b_re": b_re, "b_im": b_im, "c_re": c_re, "c_im": c_im, "d_skip": d_skip,
            "w_glu": w_glu, "b_glu": b_glu, "w_attn_up": w_attn_up, "w_ssm_up": w_ssm_up,
            "w_o": w_o, "ln_g": ln_g, "ln_b": ln_b}


def reference(x, w_in, lam_re, lam_im, log_dt, b_re, b_im, c_re, c_im, d_skip,
              w_glu, b_glu, w_attn_up, w_ssm_up, w_o, ln_g, ln_b):
    bsz, L, _ = x.shape
    h = x
    for layer in range(DEPTH):
        proj = jnp.einsum('bld,de->ble', h, w_in[layer])
        o0 = 0
        q = proj[..., o0:o0 + QKV_WIDTH]; o0 += QKV_WIDTH
        k = proj[..., o0:o0 + QKV_WIDTH]; o0 += QKV_WIDTH
        v = proj[..., o0:o0 + QKV_WIDTH]; o0 += QKV_WIDTH
        z_a = proj[..., o0:o0 + ATTN_WIDTH]; o0 += ATTN_WIDTH
        u_s = proj[..., o0:o0 + SSM_WIDTH]; o0 += SSM_WIDTH
        z_s = proj[..., o0:o0 + SSM_WIDTH]; o0 += SSM_WIDTH
        gate_logits = proj[..., o0:o0 + N_BRANCHES * D_MODEL].astype(jnp.float32)
        hs = (bsz, L, N_ATTN_GROUPS, N_SLOTS, HEAD_DIM)
        attn = _dilated_mixture(q.reshape(hs), k.reshape(hs), v.reshape(hs)).reshape(bsz, L, ATTN_WIDTH)
        attn = attn * jax.nn.silu(z_a.astype(jnp.float32))
        y_a = jnp.einsum('blc,cd->bld', attn, w_attn_up[layer])
        y = jax.nn.gelu(_s5(u_s, lam_re[layer], lam_im[layer], log_dt[layer], b_re[layer], b_im[layer],
                            c_re[layer], c_im[layer], d_skip[layer]))
        y = y * jax.nn.sigmoid(jnp.einsum('blc,ce->ble', y, w_glu[layer]) + b_glu[layer])
        y = y * jax.nn.silu(z_s.astype(jnp.float32))
        y_s = jnp.einsum('blc,cd->bld', y, w_ssm_up[layer])
        g_a = jax.nn.sigmoid(gate_logits[..., :D_MODEL])
        g_s = jax.nn.sigmoid(gate_logits[..., D_MODEL:])
        merged = g_a * y_a + g_s * y_s
        out = jnp.einsum('bld,de->ble', merged, w_o[layer])
        h = _layer_norm(DEEPNORM_ALPHA * h.astype(jnp.float32) + out, ln_g[layer], ln_b[layer]).astype(x.dtype)
    return h
```

```python
import functools
import math

import jax
import jax.numpy as jnp
from jax import lax
from jax.experimental import pallas as pl
from jax.experimental.pallas import tpu as pltpu

F32 = jnp.float32
BF16 = jnp.bfloat16

HEAD_DIM = 64
N_SLOTS = 8
GROUP_PATTERNS = ((128, 1), (512, 4), (2048, 16))
N_GROUPS = len(GROUP_PATTERNS)
ATTN_WIDTH = N_SLOTS * HEAD_DIM
QKV_WIDTH = N_GROUPS * ATTN_WIDTH
BLOCK = 128
SSM_WIDTH = 512
SSM_GROUP = 16
SSM_GROUPS = SSM_WIDTH // SSM_GROUP
SSM_STATE = 64
LN_EPS = 1e-5
MASK_VALUE = -1e30

LANES = 128
SSM_CHUNK = 8
SSM_SLABS = SSM_WIDTH // LANES
GROUPS_PER_SLAB = LANES // SSM_GROUP
SLAB_STATE = GROUPS_PER_SLAB * SSM_STATE
VMEM_LIMIT = 56 * 1024 * 1024


def _const_spec(shape):
    nd = len(shape)
    return pl.BlockSpec(shape, lambda *_: (0,) * nd, pipeline_mode=pl.Buffered(1))


def _qkv_kernel(x_ref, w_ref, q_ref, k_ref, v_ref):
    xb = x_ref[...].astype(BF16)
    for j, o_ref in enumerate((q_ref, k_ref, v_ref)):
        w = w_ref[:, j * QKV_WIDTH:(j + 1) * QKV_WIDTH]
        o_ref[...] = jnp.dot(xb, w, preferred_element_type=F32).astype(BF16)


def _qkv_proj(x2, w_qkv, tm=512):
    n, d = x2.shape
    out = jax.ShapeDtypeStruct((n, QKV_WIDTH), BF16)
    return pl.pallas_call(
        _qkv_kernel,
        grid=(n // tm,),
        in_specs=[pl.BlockSpec((tm, d), lambda i: (i, 0)), _const_spec(w_qkv.shape)],
        out_specs=[pl.BlockSpec((tm, QKV_WIDTH), lambda i: (i, 0))] * 3,
        out_shape=[out] * 3,
        compiler_params=pltpu.CompilerParams(dimension_semantics=("arbitrary",),
                                             vmem_limit_bytes=VMEM_LIMIT),
        name="qkv_proj",
    )(x2, w_qkv)


Q_BLOCKS = 2


def _attn_kernel(q_ref, kc_ref, kp_ref, vc_ref, vp_ref, bias_ref, o_ref, lse_ref):
    n = pl.program_id(2)
    lane = lax.broadcasted_iota(jnp.int32, (BLOCK, LANES), 1)
    low = lane < HEAD_DIM
    col = lax.broadcasted_iota(jnp.int32, (BLOCK, 2 * BLOCK), 1)
    no_prev = jnp.logical_and(n == 0, col < BLOCK)
    for i in range(Q_BLOCKS):
        rows = slice(i * BLOCK, (i + 1) * BLOCK)
        for hp in range(ATTN_WIDTH // LANES):
            cols = slice(hp * LANES, (hp + 1) * LANES)
            qp = q_ref[rows, cols]
            if i == 0:
                k_prev, v_prev = kp_ref[:, cols], vp_ref[:, cols]
            else:
                prev_rows = slice((i - 1) * BLOCK, i * BLOCK)
                k_prev, v_prev = kc_ref[prev_rows, cols], vc_ref[prev_rows, cols]
            kcat = jnp.concatenate([k_prev, kc_ref[rows, cols]], axis=0)
            vcat = jnp.concatenate([v_prev, vc_ref[rows, cols]], axis=0)
            outs, lses = [], []
            for hh in range(2):
                keep = low if hh == 0 else jnp.logical_not(low)
                qm = jnp.where(keep, qp, jnp.zeros_like(qp))
                s = lax.dot_general(qm, kcat, (((1,), (1,)), ((), ())), preferred_element_type=F32)
                s = s + bias_ref[2 * hp + hh]
                if i == 0:
                    s = jnp.where(no_prev, MASK_VALUE, s)
                m = jnp.max(s, axis=1, keepdims=True)
                p = jnp.exp(s - m)
                l = jnp.sum(p, axis=1, keepdims=True)
                o = jnp.dot(p.astype(BF16), vcat, preferred_element_type=F32)
                outs.append(o * (1.0 / l))
                lses.append(jnp.broadcast_to(m + jnp.log(l), (BLOCK, LANES)))
            o_ref[rows, cols] = jnp.where(low, outs[0], outs[1]).astype(BF16)
            lse_ref[rows, cols] = jnp.where(low, lses[0], lses[1])


def _attn_bias(dilation):
    slopes = 2.0 ** (-8.0 * jnp.arange(1, N_SLOTS + 1, dtype=F32) / N_SLOTS)
    qi = jnp.arange(BLOCK)[:, None]
    kj = jnp.arange(2 * BLOCK)[None, :]
    dist = BLOCK + qi - kj
    valid = (dist >= 0) & (dist <= BLOCK)
    alibi = -slopes[:, None, None] * (dist * dilation).astype(F32)[None]
    return jnp.where(valid[None], alibi, MASK_VALUE).astype(F32)


def _attention_group(q, k, v, g, dilation, bsz, seq):
    r = dilation
    n_sub = seq // r
    tq = Q_BLOCKS * BLOCK
    view = (bsz, n_sub, r * QKV_WIDTH)
    qv, kv, vv = q.reshape(view), k.reshape(view), v.reshape(view)
    cur = pl.BlockSpec((None, tq, ATTN_WIDTH), lambda b, p, n: (b, n, N_GROUPS * p + g))
    prev = pl.BlockSpec((None, BLOCK, ATTN_WIDTH),
                        lambda b, p, n: (b, jnp.maximum(Q_BLOCKS * n - 1, 0), N_GROUPS * p + g))
    out_spec = pl.BlockSpec((None, tq, ATTN_WIDTH), lambda b, p, n: (b, n, p))
    o, lse = pl.pallas_call(
        _attn_kernel,
        grid=(bsz, r, n_sub // tq),
        in_specs=[cur, cur, prev, cur, prev, _const_spec((N_SLOTS, BLOCK, 2 * BLOCK))],
        out_specs=[out_spec, out_spec],
        out_shape=[jax.ShapeDtypeStruct((bsz, n_sub, r * ATTN_WIDTH), BF16),
                   jax.ShapeDtypeStruct((bsz, n_sub, r * ATTN_WIDTH), F32)],
        compiler_params=pltpu.CompilerParams(dimension_semantics=("arbitrary",) * 3,
                                             vmem_limit_bytes=VMEM_LIMIT),
        name=f"band_attn_g{g}",
    )(qv, kv, kv, vv, vv, _attn_bias(r))
    return o.reshape(bsz * seq, ATTN_WIDTH), lse.reshape(bsz * seq, ATTN_WIDTH)


def _ssm_weights(lam_re, lam_im, log_dt, b_re, b_im, c_re, c_im):
    t = SSM_CHUNK
    lr, li = lam_re.astype(F32), lam_im.astype(F32)
    dt = jnp.exp(log_dt.astype(F32))[:, None]
    mag = jnp.exp(lr * dt)
    ab_re, ab_im = mag * jnp.cos(li * dt), mag * jnp.sin(li * dt)
    nr, ni = ab_re - 1.0, ab_im
    den = lr * lr + li * li
    coef_re = (nr * lr + ni * li) / den
    coef_im = (ni * lr - nr * li) / den
    br, bi = b_re.astype(F32), b_im.astype(F32)
    bb_re = coef_re[..., None] * br - coef_im[..., None] * bi
    bb_im = coef_re[..., None] * bi + coef_im[..., None] * br
    pw_re, pw_im = [jnp.ones_like(ab_re)], [jnp.zeros_like(ab_im)]
    for _ in range(t):
        pr, pi = pw_re[-1], pw_im[-1]
        pw_re.append(pr * ab_re - pi * ab_im)
        pw_im.append(pr * ab_im + pi * ab_re)
    pw_re, pw_im = jnp.stack(pw_re), jnp.stack(pw_im)
    cr, ci = c_re.astype(F32), c_im.astype(F32)
    hi = lax.Precision.HIGHEST

    ab_b_re = pw_re[..., None] * bb_re[None] - pw_im[..., None] * bb_im[None]
    ab_b_im = pw_re[..., None] * bb_im[None] + pw_im[..., None] * bb_re[None]
    taps = (jnp.einsum('gop,dgpi->dgio', cr, ab_b_re[:t], precision=hi)
            - jnp.einsum('gop,dgpi->dgio', ci, ab_b_im[:t], precision=hi))
    eye = jnp.eye(GROUPS_PER_SLAB, dtype=F32)
    s = SSM_SLABS
    taps = taps.reshape(t, s, GROUPS_PER_SLAB, SSM_GROUP, SSM_GROUP)
    taps_bd = jnp.einsum('dsgio,gh->dsgiho', taps, eye).reshape(t, s, LANES, LANES)
    lag = jnp.arange(t)[None, :] - jnp.arange(t)[:, None]
    toep = jnp.where((lag >= 0)[None, :, None, :, None],
                     taps_bd[jnp.clip(lag, 0, t - 1)].transpose(2, 0, 3, 1, 4), 0.0)
    toep = toep.reshape(s, t * LANES, t * LANES)

    def to_state(w):
        w = w.reshape(t, s, GROUPS_PER_SLAB, SSM_STATE, SSM_GROUP)
        w = jnp.einsum('dsgpc,gh->sdgchp', w, eye)
        return w.reshape(s, t * LANES, SLAB_STATE)
    w_st = jnp.concatenate([to_state(ab_b_re[:t][::-1]), to_state(ab_b_im[:t][::-1])], axis=2)

    ca_re = cr[None] * pw_re[1:, :, None, :] - ci[None] * pw_im[1:, :, None, :]
    ca_im = cr[None] * pw_im[1:, :, None, :] + ci[None] * pw_re[1:, :, None, :]

    def from_state(w):
        w = w.reshape(t, s, GROUPS_PER_SLAB, SSM_GROUP, SSM_STATE)
        w = jnp.einsum('dsgcp,gh->shpdgc', w, eye)
        return w.reshape(s, SLAB_STATE, t * LANES)
    w_out = jnp.concatenate([from_state(ca_re), from_state(-ca_im)], axis=1)

    a_t = jnp.stack([pw_re[t].reshape(-1), pw_im[t].reshape(-1)])
    return toep.astype(BF16), w_st.astype(BF16), w_out.astype(BF16), a_t


def _ssm_kernel(x_ref, wuz_ref, toep_ref, wst_ref, wout_ref, at_ref, dskip_ref, wglu_ref, bglu_ref,
                out_ref, u_scr, z_scr, s_scr, xp_scr, y_scr, carry_scr):
    bsz, ch, _ = x_ref.shape
    rows = bsz * ch
    t = SSM_CHUNK
    d_model = x_ref.shape[2] // t
    tiles_per_slab = SLAB_STATE // LANES
    n_tiles = SSM_SLABS * tiles_per_slab

    @pl.when(pl.program_id(0) == 0)
    def _():
        carry_scr[...] = jnp.zeros_like(carry_scr)

    for sg in range(t):
        xs = x_ref[:, :, sg * d_model:(sg + 1) * d_model].reshape(rows, d_model).astype(BF16)
        uz = jnp.dot(xs, wuz_ref[...], preferred_element_type=F32)
        u_scr[:, sg * SSM_WIDTH:(sg + 1) * SSM_WIDTH] = uz[:, :SSM_WIDTH]
        z_scr[:, sg * SSM_WIDTH:(sg + 1) * SSM_WIDTH] = uz[:, SSM_WIDTH:]

    for j in range(SSM_SLABS):
        lhs = jnp.concatenate(
            [u_scr[:, sg * SSM_WIDTH + j * LANES: sg * SSM_WIDTH + (j + 1) * LANES] for sg in range(t)],
            axis=1).astype(BF16)
        st = jnp.dot(lhs, wst_ref[j], preferred_element_type=F32)
        for c in range(tiles_per_slab):
            s_scr[j * tiles_per_slab + c] = st[:, c * LANES:(c + 1) * LANES]
            s_scr[n_tiles + j * tiles_per_slab + c] = st[:, SLAB_STATE + c * LANES: SLAB_STATE + (c + 1) * LANES]
        y_scr[:, j * t * LANES:(j + 1) * t * LANES] = jnp.dot(lhs, toep_ref[j], preferred_element_type=F32)

    def chunk_step(k, carry):
        sel = pl.ds(k, bsz, stride=ch)
        new = []
        for c in range(n_tiles):
            c_re, c_im = carry[c], carry[n_tiles + c]
            xp_scr[c, sel, :] = c_re
            xp_scr[n_tiles + c, sel, :] = c_im
            a_re = at_ref[c:c + 1, :]
            a_im = at_ref[n_tiles + c:n_tiles + c + 1, :]
            new.append((a_re * c_re - a_im * c_im + s_scr[c, sel, :],
                        a_re * c_im + a_im * c_re + s_scr[n_tiles + c, sel, :]))
        return tuple(v[0] for v in new) + tuple(v[1] for v in new)

    carry = lax.fori_loop(0, ch, chunk_step, tuple(carry_scr[c] for c in range(2 * n_tiles)))
    for c in range(2 * n_tiles):
        carry_scr[c] = carry[c]

    for j in range(SSM_SLABS):
        xin = jnp.concatenate(
            [xp_scr[half + j * tiles_per_slab + c] for half in (0, n_tiles) for c in range(tiles_per_slab)],
            axis=1).astype(BF16)
        yj = y_scr[:, j * t * LANES:(j + 1) * t * LANES] + jnp.dot(xin, wout_ref[j], preferred_element_type=F32)
        y_scr[:, j * t * LANES:(j + 1) * t * LANES] = yj

    for tau in range(t):
        y = jnp.concatenate(
            [y_scr[:, j * t * LANES + tau * LANES: j * t * LANES + (tau + 1) * LANES] for j in range(SSM_SLABS)],
            axis=1)
        cols = slice(tau * SSM_WIDTH, (tau + 1) * SSM_WIDTH)
        y = y + dskip_ref[...] * u_scr[:, cols]
        y = jax.nn.gelu(y)
        gate = jax.nn.sigmoid(jnp.dot(y.astype(BF16), wglu_ref[...], preferred_element_type=F32) + bglu_ref[...])
        y = y * gate * jax.nn.silu(z_scr[:, cols])
        out_ref[:, :, cols] = y.reshape(bsz, ch, SSM_WIDTH).astype(BF16)


def _ssm_branch(x, w_uz, ssm_w, d_skip, w_glu, b_glu, tl=128):
    bsz, seq, d = x.shape
    t = SSM_CHUNK
    ch = tl // t
    rows = bsz * ch
    toep, w_st, w_out, a_t = ssm_w
    xv = x.reshape(bsz, seq // t, t * d)
    n_tiles2 = 2 * SSM_SLABS * SLAB_STATE // LANES
    a_t = a_t.reshape(n_tiles2, LANES)
    out = pl.pallas_call(
        _ssm_kernel,
        grid=(seq // tl,),
        in_specs=[pl.BlockSpec((bsz, ch, t * d), lambda i: (0, i, 0)),
                  _const_spec(w_uz.shape), _const_spec(toep.shape), _const_spec(w_st.shape),
                  _const_spec(w_out.shape), _const_spec(a_t.shape), _const_spec((1, SSM_WIDTH)),
                  _const_spec(w_glu.shape), _const_spec((1, SSM_WIDTH))],
        out_specs=pl.BlockSpec((bsz, ch, t * SSM_WIDTH), lambda i: (0, i, 0)),
        out_shape=jax.ShapeDtypeStruct((bsz, seq // t, t * SSM_WIDTH), BF16),
        scratch_shapes=[pltpu.VMEM((rows, t * SSM_WIDTH), F32),
                        pltpu.VMEM((rows, t * SSM_WIDTH), F32),
                        pltpu.VMEM((n_tiles2, rows, LANES), F32),
                        pltpu.VMEM((n_tiles2, rows, LANES), F32),
                        pltpu.VMEM((rows, t * SSM_WIDTH), F32),
                        pltpu.VMEM((n_tiles2, bsz, LANES), F32)],
        compiler_params=pltpu.CompilerParams(dimension_semantics=("arbitrary",),
                                             vmem_limit_bytes=VMEM_LIMIT),
        name="s5_branch",
    )(xv, w_uz, toep, w_st, w_out, a_t, d_skip.reshape(1, SSM_WIDTH).astype(F32),
      w_glu, b_glu.reshape(1, SSM_WIDTH).astype(F32))
    return out.reshape(bsz * seq, SSM_WIDTH)


def _merge_kernel(alpha, x_ref, o0_ref, o1_ref, o2_ref, l0_ref, l1_ref, l2_ref, ys_ref,
                  wzg_ref, wa_ref, ws_ref, wo_ref, g_ref, b_ref, out_ref):
    x = x_ref[...]
    d_model = x.shape[1]
    zg = jnp.dot(x.astype(BF16), wzg_ref[...], preferred_element_type=F32)
    z_a = zg[:, :ATTN_WIDTH]
    gate_a = zg[:, ATTN_WIDTH:ATTN_WIDTH + d_model]
    gate_s = zg[:, ATTN_WIDTH + d_model:]

    lses = (l0_ref[...], l1_ref[...], l2_ref[...])
    top = jnp.maximum(jnp.maximum(lses[0], lses[1]), lses[2])
    num = jnp.zeros_like(top)
    den = jnp.zeros_like(top)
    for lse, o_ref in zip(lses, (o0_ref, o1_ref, o2_ref)):
        w = jnp.exp(lse - top)
        num = num + w * o_ref[...].astype(F32)
        den = den + w
    attn = num / den * jax.nn.silu(z_a)
    y_a = jnp.dot(attn.astype(BF16), wa_ref[...], preferred_element_type=F32)
    y_s = jnp.dot(ys_ref[...], ws_ref[...], preferred_element_type=F32)
    merged = jax.nn.sigmoid(gate_a) * y_a + jax.nn.sigmoid(gate_s) * y_s
    out = jnp.dot(merged.astype(BF16), wo_ref[...], preferred_element_type=F32)
    h = alpha * x + out
    mu = jnp.mean(h, axis=-1, keepdims=True)
    hc = h - mu
    var = jnp.mean(hc * hc, axis=-1, keepdims=True)
    out_ref[...] = hc * lax.rsqrt(var + LN_EPS) * g_ref[...] + b_ref[...]


def _merge(x2, attn_o, attn_lse, ys, w_zg, w_attn_up, w_ssm_up, w_o, ln_g, ln_b, alpha, tm=512):
    n, d = x2.shape
    row = lambda width: pl.BlockSpec((tm, width), lambda i: (i, 0))
    return pl.pallas_call(
        functools.partial(_merge_kernel, alpha),
        grid=(n // tm,),
        in_specs=[row(d)] + [row(ATTN_WIDTH)] * 6 + [row(SSM_WIDTH),
                  _const_spec(w_zg.shape), _const_spec(w_attn_up.shape), _const_spec(w_ssm_up.shape),
                  _const_spec(w_o.shape), _const_spec((1, d)), _const_spec((1, d))],
        out_specs=row(d),
        out_shape=jax.ShapeDtypeStruct((n, d), F32),
        compiler_params=pltpu.CompilerParams(dimension_semantics=("arbitrary",),
                                             vmem_limit_bytes=VMEM_LIMIT),
        name="merge_out_ln",
    )(x2, *attn_o, *attn_lse, ys, w_zg, w_attn_up, w_ssm_up, w_o,
      ln_g.reshape(1, d).astype(F32), ln_b.reshape(1, d).astype(F32))


def _layer(h, w_in, lam_re, lam_im, log_dt, b_re, b_im, c_re, c_im, d_skip,
           w_glu, b_glu, w_attn_up, w_ssm_up, w_o, ln_g, ln_b, alpha):
    bsz, seq, d = h.shape
    assert seq % (GROUP_PATTERNS[-1][1] * Q_BLOCKS * BLOCK) == 0 or seq % (GROUP_PATTERNS[-1][1] * BLOCK) == 0
    x2 = h.reshape(bsz * seq, d)
    o0 = 3 * QKV_WIDTH
    w_qkv = jnp.concatenate([w_in[:, :QKV_WIDTH] * (1.0 / math.sqrt(HEAD_DIM)), w_in[:, QKV_WIDTH:o0]],
                            axis=1).astype(BF16)
    w_uz = w_in[:, o0 + ATTN_WIDTH: o0 + ATTN_WIDTH + 2 * SSM_WIDTH].astype(BF16)
    w_zg = jnp.concatenate([w_in[:, o0:o0 + ATTN_WIDTH], w_in[:, o0 + ATTN_WIDTH + 2 * SSM_WIDTH:]],
                           axis=1).astype(BF16)

    q, k, v = _qkv_proj(x2, w_qkv)
    attn_o, attn_lse = [], []
    for g, (window, dilation) in enumerate(GROUP_PATTERNS):
        assert window // dilation == BLOCK
        o, lse = _attention_group(q, k, v, g, dilation, bsz, seq)
        attn_o.append(o)
        attn_lse.append(lse)

    ssm_w = _ssm_weights(lam_re, lam_im, log_dt, b_re, b_im, c_re, c_im)
    ys = _ssm_branch(h, w_uz, ssm_w, d_skip, w_glu.astype(BF16), b_glu)

    out = _merge(x2, attn_o, attn_lse, ys, w_zg, w_attn_up.astype(BF16), w_ssm_up.astype(BF16),
                 w_o.astype(BF16), ln_g, ln_b, alpha)
    return out.reshape(bsz, seq, d)


def kernel(x, w_in, lam_re, lam_im, log_dt, b_re, b_im, c_re, c_im, d_skip,
           w_glu, b_glu, w_attn_up, w_ssm_up, w_o, ln_g, ln_b):
    depth = w_in.shape[0]
    alpha = (2.0 * depth) ** 0.25
    h = x
    for layer in range(depth):
        h = _layer(h, w_in[layer], lam_re[layer], lam_im[layer], log_dt[layer], b_re[layer], b_im[layer],
                   c_re[layer], c_im[layer], d_skip[layer], w_glu[layer], b_glu[layer], w_attn_up[layer],
                   w_ssm_up[layer], w_o[layer], ln_g[layer], ln_b[layer], alpha)
    return h
```

```python
import functools
import math

import jax
import jax.numpy as jnp
from jax import lax
from jax.experimental import pallas as pl
from jax.experimental.pallas import tpu as pltpu

F32 = jnp.float32
BF16 = jnp.bfloat16

HEAD_DIM = 64
N_SLOTS = 8
GROUP_PATTERNS = ((128, 1), (512, 4), (2048, 16))
N_GROUPS = len(GROUP_PATTERNS)
ATTN_WIDTH = N_SLOTS * HEAD_DIM
QKV_WIDTH = N_GROUPS * ATTN_WIDTH
BLOCK = 128
SPAN = GROUP_PATTERNS[-1][1] * BLOCK
UNITS = SPAN // BLOCK
SSM_WIDTH = 512
SSM_GROUP = 16
SSM_GROUPS = SSM_WIDTH // SSM_GROUP
SSM_STATE = 64
LN_EPS = 1e-5
MASK_VALUE = -1e30

LANES = 128
HEAD_PAIRS = ATTN_WIDTH // LANES
SSM_CHUNK = 8
SSM_SLABS = SSM_WIDTH // LANES
GROUPS_PER_SLAB = LANES // SSM_GROUP
SLAB_STATE = GROUPS_PER_SLAB * SSM_STATE
VMEM_LIMIT = 56 * 1024 * 1024


def _const_spec(shape):
    nd = len(shape)
    return pl.BlockSpec(shape, lambda *_: (0,) * nd, pipeline_mode=pl.Buffered(1))


def _lane_tiles(x):
    return [x[:, c * LANES:(c + 1) * LANES] for c in range(x.shape[1] // LANES)]


def _qkv_kernel(x_ref, w_ref, *refs):
    out_refs, stage = refs[:-1], refs[-1]
    tm = x_ref.shape[0]
    xb = x_ref[...].astype(BF16)
    for kind in range(3):
        for g, (_, r) in enumerate(GROUP_PATTERNS):
            col = kind * QKV_WIDTH + g * ATTN_WIDTH
            res = jnp.dot(xb, w_ref[:, col:col + ATTN_WIDTH], preferred_element_type=F32)
            o_ref = out_refs[kind * N_GROUPS + g]
            if r == 1:
                o_ref[0] = res.astype(BF16)
                continue
            for c, tile in enumerate(_lane_tiles(res)):
                stage[c] = tile
            for p in range(r):
                o_ref[p] = jnp.concatenate(
                    [stage[c, pl.ds(p, tm // r, stride=r), :] for c in range(HEAD_PAIRS)], axis=1).astype(BF16)


def _qkv_proj(x2, w_qkv, bsz, seq, tm=512):
    n, d = x2.shape
    tiles = seq // tm
    out_specs, out_shape = [], []
    for _ in range(3):
        for _, r in GROUP_PATTERNS:
            out_specs.append(pl.BlockSpec((None, r, tm // r, ATTN_WIDTH),
                                          lambda i: (i // tiles, 0, i % tiles, 0)))
            out_shape.append(jax.ShapeDtypeStruct((bsz, r, seq // r, ATTN_WIDTH), BF16))
    return pl.pallas_call(
        _qkv_kernel,
        grid=(n // tm,),
        in_specs=[pl.BlockSpec((tm, d), lambda i: (i, 0)), _const_spec(w_qkv.shape)],
        out_specs=out_specs,
        out_shape=out_shape,
        scratch_shapes=[pltpu.VMEM((HEAD_PAIRS, tm, LANES), F32)],
        compiler_params=pltpu.CompilerParams(dimension_semantics=("arbitrary",),
                                             vmem_limit_bytes=VMEM_LIMIT),
        name="qkv_proj",
    )(x2, w_qkv)


def _attn_bias():
    slopes = 2.0 ** (-8.0 * jnp.arange(1, N_SLOTS + 1, dtype=F32) / N_SLOTS)
    qi = jnp.arange(BLOCK)[:, None]
    kj = jnp.arange(2 * BLOCK)[None, :]
    dist = BLOCK + qi - kj
    valid = (dist >= 0) & (dist <= BLOCK)
    out = []
    for _, r in GROUP_PATTERNS:
        alibi = -slopes[:, None, None] * (dist * r).astype(F32)[None]
        out.append(jnp.where(valid[None], alibi, MASK_VALUE).astype(F32))
    return jnp.stack(out)


def _attn_kernel(*refs):
    ins, bias_ref, out_ref = refs[:5 * N_GROUPS], refs[5 * N_GROUPS], refs[5 * N_GROUPS + 1]
    scr = refs[5 * N_GROUPS + 2:]
    kv_scr, o_stage, lse_stage = scr[:2 * N_GROUPS], scr[2 * N_GROUPS], scr[2 * N_GROUPS + 1]
    first_span = pl.program_id(1) == 0
    lane = lax.broadcasted_iota(jnp.int32, (BLOCK, LANES), 1)
    low = lane < HEAD_DIM
    col = lax.broadcasted_iota(jnp.int32, (BLOCK, 2 * BLOCK), 1)
    prev_cols = col < BLOCK

    for g, (_, r) in enumerate(GROUP_PATTERNS):
        q_ref, kc_ref, kp_ref, vc_ref, vp_ref = ins[5 * g:5 * g + 5]
        k_scr, v_scr = kv_scr[2 * g:2 * g + 2]
        nb = UNITS // r
        k_scr[:, :BLOCK, :] = kp_ref[...]
        k_scr[:, BLOCK:, :] = kc_ref[...]
        v_scr[:, :BLOCK, :] = vp_ref[...]
        v_scr[:, BLOCK:, :] = vc_ref[...]

        def unit(u, carry, g=g, r=r, nb=nb, q_ref=q_ref, k_scr=k_scr, v_scr=v_scr):
            p, n = u // nb, u % nb
            row0 = pl.multiple_of(n * BLOCK, BLOCK)
            q = q_ref[p, pl.ds(row0, BLOCK), :]
            kcat = k_scr[p, pl.ds(row0, 2 * BLOCK), :]
            vcat = v_scr[p, pl.ds(row0, 2 * BLOCK), :]
            no_prev = jnp.logical_and(jnp.logical_and(first_span, n == 0), prev_cols)
            outs, lses = [], []
            for hh in range(2):
                keep = low if hh == 0 else jnp.logical_not(low)
                qm = jnp.where(keep, q, jnp.zeros_like(q))
                s = lax.dot_general(qm, kcat, (((1,), (1,)), ((), ())), preferred_element_type=F32)
                s = jnp.where(no_prev, MASK_VALUE, s + bias_ref[g, hh])
                m = jnp.max(s, axis=1, keepdims=True)
                e = jnp.exp(s - m)
                l = jnp.sum(e, axis=1, keepdims=True)
                o = jnp.dot(e.astype(BF16), vcat, preferred_element_type=F32)
                outs.append(o * (1.0 / l))
                lses.append(jnp.broadcast_to(m + jnp.log(l), (BLOCK, LANES)))
            start = n * (BLOCK * r) + p
            rows = pl.ds(start, BLOCK) if r == 1 else pl.ds(start, BLOCK, stride=r)
            o_stage[g, rows, :] = jnp.where(low, outs[0], outs[1])
            lse_stage[g, rows, :] = jnp.where(low, lses[0], lses[1])
            return carry

        lax.fori_loop(0, UNITS, unit, 0)

    def mix(i, carry):
        rows = pl.ds(pl.multiple_of(i * BLOCK, BLOCK), BLOCK)
        lses = [lse_stage[g, rows, :] for g in range(N_GROUPS)]
        top = functools.reduce(jnp.maximum, lses)
        num = jnp.zeros((BLOCK, LANES), F32)
        den = jnp.zeros((BLOCK, LANES), F32)
        for g in range(N_GROUPS):
            w = jnp.exp(lses[g] - top)
            num = num + w * o_stage[g, rows, :]
            den = den + w
        out_ref[rows, :] = (num / den).astype(BF16)
        return carry

    lax.fori_loop(0, UNITS, mix, 0)


def _attention(qkv, bsz, seq):
    spans = seq // SPAN
    in_specs, args, scratch = [], [], []
    for g, (_, r) in enumerate(GROUP_PATTERNS):
        nb = UNITS // r
        cur = pl.BlockSpec((None, r, SPAN // r, LANES), lambda b, s, hp: (b, 0, s, hp))
        prev = pl.BlockSpec((None, r, BLOCK, LANES),
                            lambda b, s, hp, nb=nb: (b, 0, jnp.maximum(nb * s - 1, 0), hp))
        q, k, v = qkv[g], qkv[N_GROUPS + g], qkv[2 * N_GROUPS + g]
        in_specs += [cur, cur, prev, cur, prev]
        args += [q, k, k, v, v]
        scratch += [pltpu.VMEM((r, SPAN // r + BLOCK, LANES), BF16)] * 2
    in_specs.append(pl.BlockSpec((N_GROUPS, 2, BLOCK, 2 * BLOCK), lambda b, s, hp: (0, hp, 0, 0)))
    scratch += [pltpu.VMEM((N_GROUPS, SPAN, LANES), F32)] * 2
    return pl.pallas_call(
        _attn_kernel,
        grid=(bsz, spans, HEAD_PAIRS),
        in_specs=in_specs,
        out_specs=pl.BlockSpec((SPAN, LANES), lambda b, s, hp: (b * spans + s, hp)),
        out_shape=jax.ShapeDtypeStruct((bsz * seq, ATTN_WIDTH), BF16),
        scratch_shapes=scratch,
        compiler_params=pltpu.CompilerParams(dimension_semantics=("arbitrary",) * 3,
                                             vmem_limit_bytes=VMEM_LIMIT),
        name="band_attn",
    )(*args, _attn_bias())


def _ssm_weights(lam_re, lam_im, log_dt, b_re, b_im, c_re, c_im):
    t = SSM_CHUNK
    lr, li = lam_re.astype(F32), lam_im.astype(F32)
    dt = jnp.exp(log_dt.astype(F32))[:, None]
    mag = jnp.exp(lr * dt)
    ab_re, ab_im = mag * jnp.cos(li * dt), mag * jnp.sin(li * dt)
    nr, ni = ab_re - 1.0, ab_im
    den = lr * lr + li * li
    coef_re = (nr * lr + ni * li) / den
    coef_im = (ni * lr - nr * li) / den
    br, bi = b_re.astype(F32), b_im.astype(F32)
    bb_re = coef_re[..., None] * br - coef_im[..., None] * bi
    bb_im = coef_re[..., None] * bi + coef_im[..., None] * br
    pw_re, pw_im = [jnp.ones_like(ab_re)], [jnp.zeros_like(ab_im)]
    for _ in range(t):
        pr, pi = pw_re[-1], pw_im[-1]
        pw_re.append(pr * ab_re - pi * ab_im)
        pw_im.append(pr * ab_im + pi * ab_re)
    pw_re, pw_im = jnp.stack(pw_re), jnp.stack(pw_im)
    cr, ci = c_re.astype(F32), c_im.astype(F32)
    hi = lax.Precision.HIGHEST

    ab_b_re = pw_re[..., None] * bb_re[None] - pw_im[..., None] * bb_im[None]
    ab_b_im = pw_re[..., None] * bb_im[None] + pw_im[..., None] * bb_re[None]
    taps = (jnp.einsum('gop,dgpi->dgio', cr, ab_b_re[:t], precision=hi)
            - jnp.einsum('gop,dgpi->dgio', ci, ab_b_im[:t], precision=hi))
    eye = jnp.eye(GROUPS_PER_SLAB, dtype=F32)
    s = SSM_SLABS
    taps = taps.reshape(t, s, GROUPS_PER_SLAB, SSM_GROUP, SSM_GROUP)
    taps_bd = jnp.einsum('dsgio,gh->dsgiho', taps, eye).reshape(t, s, LANES, LANES)
    lag = jnp.arange(t)[None, :] - jnp.arange(t)[:, None]
    toep = jnp.where((lag >= 0)[None, :, None, :, None],
                     taps_bd[jnp.clip(lag, 0, t - 1)].transpose(2, 0, 3, 1, 4), 0.0)
    toep = toep.reshape(s, t * LANES, t * LANES)

    def to_state(w):
        w = w.reshape(t, s, GROUPS_PER_SLAB, SSM_STATE, SSM_GROUP)
        w = jnp.einsum('dsgpc,gh->sdgchp', w, eye)
        return w.reshape(s, t * LANES, SLAB_STATE)
    w_st = jnp.concatenate([to_state(ab_b_re[:t][::-1]), to_state(ab_b_im[:t][::-1])], axis=2)

    ca_re = cr[None] * pw_re[1:, :, None, :] - ci[None] * pw_im[1:, :, None, :]
    ca_im = cr[None] * pw_im[1:, :, None, :] + ci[None] * pw_re[1:, :, None, :]

    def from_state(w):
        w = w.reshape(t, s, GROUPS_PER_SLAB, SSM_GROUP, SSM_STATE)
        w = jnp.einsum('dsgcp,gh->shpdgc', w, eye)
        return w.reshape(s, SLAB_STATE, t * LANES)
    w_y = jnp.concatenate([toep, from_state(ca_re), from_state(-ca_im)], axis=1)

    a_t = jnp.stack([pw_re[t].reshape(-1), pw_im[t].reshape(-1)])
    return w_y.astype(BF16), w_st.astype(BF16), a_t


def _ssm_kernel(x_ref, wu_ref, wz_ref, wy_ref, wst_ref, at_ref, dskip_ref, wglu_ref, bglu_ref,
                out_ref, u_st, z_scr, s_scr, xp_scr, y_st, carry_scr):
    bsz, tl, d_model = x_ref.shape
    t = SSM_CHUNK
    ch = tl // t
    rows = bsz * tl
    crows = bsz * ch
    tiles_per_slab = SLAB_STATE // LANES
    n_tiles = SSM_SLABS * tiles_per_slab

    @pl.when(pl.program_id(0) == 0)
    def _():
        carry_scr[...] = jnp.zeros_like(carry_scr)

    xb = x_ref[...].reshape(rows, d_model).astype(BF16)
    for c, tile in enumerate(_lane_tiles(jnp.dot(xb, wu_ref[...], preferred_element_type=F32))):
        u_st[c] = tile
    z_scr[...] = jnp.dot(xb, wz_ref[...], preferred_element_type=F32)

    def chunk_lhs(j):
        return jnp.concatenate([u_st[j, pl.ds(sg, crows, stride=t), :] for sg in range(t)], axis=1).astype(BF16)

    for j in range(SSM_SLABS):
        st = jnp.dot(chunk_lhs(j), wst_ref[j], preferred_element_type=F32)
        for c, tile in enumerate(_lane_tiles(st)):
            half, cc = divmod(c, tiles_per_slab)
            s_scr[half * n_tiles + j * tiles_per_slab + cc] = tile

    def chunk_step(k, carry):
        sel = pl.ds(k, bsz, stride=ch)
        new = []
        for c in range(n_tiles):
            c_re, c_im = carry[c], carry[n_tiles + c]
            xp_scr[c, sel, :] = c_re
            xp_scr[n_tiles + c, sel, :] = c_im
            a_re = at_ref[c:c + 1, :]
            a_im = at_ref[n_tiles + c:n_tiles + c + 1, :]
            new.append((a_re * c_re - a_im * c_im + s_scr[c, sel, :],
                        a_re * c_im + a_im * c_re + s_scr[n_tiles + c, sel, :]))
        return tuple(v[0] for v in new) + tuple(v[1] for v in new)

    carry = lax.fori_loop(0, ch, chunk_step, tuple(carry_scr[c] for c in range(2 * n_tiles)))
    for c in range(2 * n_tiles):
        carry_scr[c] = carry[c]

    for j in range(SSM_SLABS):
        xin = [xp_scr[half + j * tiles_per_slab + c] for half in (0, n_tiles) for c in range(tiles_per_slab)]
        lhs = jnp.concatenate([chunk_lhs(j)] + [v.astype(BF16) for v in xin], axis=1)
        yj = jnp.dot(lhs, wy_ref[j], preferred_element_type=F32)
        for tau, tile in enumerate(_lane_tiles(yj)):
            y_st[j, pl.ds(tau, crows, stride=t), :] = tile

    for b in range(bsz):
        rs = slice(b * tl, (b + 1) * tl)
        y = jnp.concatenate([y_st[c, rs, :] for c in range(SSM_SLABS)], axis=1)
        u = jnp.concatenate([u_st[c, rs, :] for c in range(SSM_SLABS)], axis=1)
        y = jax.nn.gelu(y + dskip_ref[...] * u)
        gate = jax.nn.sigmoid(jnp.dot(y.astype(BF16), wglu_ref[...], preferred_element_type=F32) + bglu_ref[...])
        out_ref[b] = (y * gate * jax.nn.silu(z_scr[rs, :])).astype(BF16)


def _ssm_branch(x, w_u, w_z, ssm_w, d_skip, w_glu, b_glu, tl=128):
    bsz, seq, d = x.shape
    t = SSM_CHUNK
    rows, crows = bsz * tl, bsz * tl // t
    w_y, w_st, a_t = ssm_w
    n_tiles2 = 2 * SSM_SLABS * SLAB_STATE // LANES
    a_t = a_t.reshape(n_tiles2, LANES)
    out = pl.pallas_call(
        _ssm_kernel,
        grid=(seq // tl,),
        in_specs=[pl.BlockSpec((bsz, tl, d), lambda i: (0, i, 0)),
                  _const_spec(w_u.shape), _const_spec(w_z.shape), _const_spec(w_y.shape),
                  _const_spec(w_st.shape), _const_spec(a_t.shape), _const_spec((1, SSM_WIDTH)),
                  _const_spec(w_glu.shape), _const_spec((1, SSM_WIDTH))],
        out_specs=pl.BlockSpec((bsz, tl, SSM_WIDTH), lambda i: (0, i, 0)),
        out_shape=jax.ShapeDtypeStruct((bsz, seq, SSM_WIDTH), BF16),
        scratch_shapes=[pltpu.VMEM((SSM_SLABS, rows, LANES), F32),
                        pltpu.VMEM((rows, SSM_WIDTH), F32),
                        pltpu.VMEM((n_tiles2, crows, LANES), F32),
                        pltpu.VMEM((n_tiles2, crows, LANES), F32),
                        pltpu.VMEM((SSM_SLABS, rows, LANES), F32),
                        pltpu.VMEM((n_tiles2, bsz, LANES), F32)],
        compiler_params=pltpu.CompilerParams(dimension_semantics=("arbitrary",),
                                             vmem_limit_bytes=VMEM_LIMIT),
        name="s5_branch",
    )(x, w_u, w_z, w_y, w_st, a_t, d_skip.reshape(1, SSM_WIDTH).astype(F32),
      w_glu, b_glu.reshape(1, SSM_WIDTH).astype(F32))
    return out.reshape(bsz * seq, SSM_WIDTH)


def _merge_kernel(alpha, x_ref, attn_ref, ys_ref, wzg_ref, wa_ref, ws_ref, wo_ref, g_ref, b_ref, out_ref):
    x = x_ref[...]
    d_model = x.shape[1]
    zg = jnp.dot(x.astype(BF16), wzg_ref[...], preferred_element_type=F32)
    z_a = zg[:, :ATTN_WIDTH]
    gate_a = zg[:, ATTN_WIDTH:ATTN_WIDTH + d_model]
    gate_s = zg[:, ATTN_WIDTH + d_model:]
    attn = attn_ref[...].astype(F32) * jax.nn.silu(z_a)
    y_a = jnp.dot(attn.astype(BF16), wa_ref[...], preferred_element_type=F32)
    y_s = jnp.dot(ys_ref[...], ws_ref[...], preferred_element_type=F32)
    merged = jax.nn.sigmoid(gate_a) * y_a + jax.nn.sigmoid(gate_s) * y_s
    out = jnp.dot(merged.astype(BF16), wo_ref[...], preferred_element_type=F32)
    h = alpha * x + out
    mu = jnp.mean(h, axis=-1, keepdims=True)
    hc = h - mu
    var = jnp.mean(hc * hc, axis=-1, keepdims=True)
    out_ref[...] = hc * lax.rsqrt(var + LN_EPS) * g_ref[...] + b_ref[...]


def _merge(x2, attn, ys, w_zg, w_attn_up, w_ssm_up, w_o, ln_g, ln_b, alpha, tm=512):
    n, d = x2.shape
    row = lambda width: pl.BlockSpec((tm, width), lambda i: (i, 0))
    return pl.pallas_call(
        functools.partial(_merge_kernel, alpha),
        grid=(n // tm,),
        in_specs=[row(d), row(ATTN_WIDTH), row(SSM_WIDTH),
                  _const_spec(w_zg.shape), _const_spec(w_attn_up.shape), _const_spec(w_ssm_up.shape),
                  _const_spec(w_o.shape), _const_spec((1, d)), _const_spec((1, d))],
        out_specs=row(d),
        out_shape=jax.ShapeDtypeStruct((n, d), F32),
        compiler_params=pltpu.CompilerParams(dimension_semantics=("arbitrary",),
                                             vmem_limit_bytes=VMEM_LIMIT),
        name="merge_out_ln",
    )(x2, attn, ys, w_zg, w_attn_up, w_ssm_up, w_o,
      ln_g.reshape(1, d).astype(F32), ln_b.reshape(1, d).astype(F32))


def _layer(h, w_in, lam_re, lam_im, log_dt, b_re, b_im, c_re, c_im, d_skip,
           w_glu, b_glu, w_attn_up, w_ssm_up, w_o, ln_g, ln_b, alpha):
    bsz, seq, d = h.shape
    assert seq % SPAN == 0 and all(w // r == BLOCK for w, r in GROUP_PATTERNS)
    x2 = h.reshape(bsz * seq, d)
    o0 = 3 * QKV_WIDTH
    o1 = o0 + ATTN_WIDTH
    w_qkv = jnp.concatenate([w_in[:, :QKV_WIDTH] * (1.0 / math.sqrt(HEAD_DIM)), w_in[:, QKV_WIDTH:o0]],
                            axis=1).astype(BF16)
    w_u = w_in[:, o1:o1 + SSM_WIDTH].astype(BF16)
    w_z = w_in[:, o1 + SSM_WIDTH:o1 + 2 * SSM_WIDTH].astype(BF16)
    w_zg = jnp.concatenate([w_in[:, o0:o1], w_in[:, o1 + 2 * SSM_WIDTH:]], axis=1).astype(BF16)

    qkv = _qkv_proj(x2, w_qkv, bsz, seq)
    attn = _attention(qkv, bsz, seq)
    ssm_w = _ssm_weights(lam_re, lam_im, log_dt, b_re, b_im, c_re, c_im)
    ys = _ssm_branch(h, w_u, w_z, ssm_w, d_skip, w_glu.astype(BF16), b_glu)
    out = _merge(x2, attn, ys, w_zg, w_attn_up.astype(BF16), w_ssm_up.astype(BF16),
                 w_o.astype(BF16), ln_g, ln_b, alpha)
    return out.reshape(bsz, seq, d)


def kernel(x, w_in, lam_re, lam_im, log_dt, b_re, b_im, c_re, c_im, d_skip,
           w_glu, b_glu, w_attn_up, w_ssm_up, w_o, ln_g, ln_b):
    depth = w_in.shape[0]
    alpha = (2.0 * depth) ** 0.25
    h = x
    for layer in range(depth):
        h = _layer(h, w_in[layer], lam_re[layer], lam_im[layer], log_dt[layer], b_re[layer], b_im[layer],
                   c_re[layer], c_im[layer], d_skip[layer], w_glu[layer], b_glu[layer], w_attn_up[layer],
                   w_ssm_up[layer], w_o[layer], ln_g[layer], ln_b[layer], alpha)
    return h
```

```python
import functools
import math

import jax
import jax.numpy as jnp
from jax import lax
from jax.experimental import pallas as pl
from jax.experimental.pallas import tpu as pltpu

F32 = jnp.float32
BF16 = jnp.bfloat16

HEAD_DIM = 64
N_SLOTS = 8
GROUP_PATTERNS = ((128, 1), (512, 4), (2048, 16))
N_GROUPS = len(GROUP_PATTERNS)
ATTN_WIDTH = N_SLOTS * HEAD_DIM
QKV_WIDTH = N_GROUPS * ATTN_WIDTH
BLOCK = 128
SPAN = GROUP_PATTERNS[-1][1] * BLOCK
UNITS = SPAN // BLOCK
SSM_WIDTH = 512
SSM_GROUP = 16
SSM_GROUPS = SSM_WIDTH // SSM_GROUP
SSM_STATE = 64
LN_EPS = 1e-5
MASK_VALUE = -1e30

LANES = 128
HEAD_PAIRS = ATTN_WIDTH // LANES
UNIT_UNROLL = 16
SSM_CHUNK = 8
SSM_SLABS = SSM_WIDTH // LANES
GROUPS_PER_SLAB = LANES // SSM_GROUP
SLAB_STATE = GROUPS_PER_SLAB * SSM_STATE
VMEM_LIMIT = 56 * 1024 * 1024


def _const_spec(shape):
    nd = len(shape)
    return pl.BlockSpec(shape, lambda *_: (0,) * nd, pipeline_mode=pl.Buffered(1))


def _lane_tiles(x):
    return [x[:, c * LANES:(c + 1) * LANES] for c in range(x.shape[1] // LANES)]


def _qkv_kernel(x_ref, w_ref, *refs):
    out_refs, stage = refs[:-1], refs[-1]
    tm = x_ref.shape[0]
    xb = x_ref[...].astype(BF16)
    for kind in range(3):
        for g, (_, r) in enumerate(GROUP_PATTERNS):
            col = kind * QKV_WIDTH + g * ATTN_WIDTH
            res = jnp.dot(xb, w_ref[:, col:col + ATTN_WIDTH], preferred_element_type=F32)
            o_ref = out_refs[kind * N_GROUPS + g]
            if r == 1:
                o_ref[0] = res.astype(BF16)
                continue
            for c, tile in enumerate(_lane_tiles(res)):
                stage[c] = tile
            for p in range(r):
                o_ref[p] = jnp.concatenate(
                    [stage[c, pl.ds(p, tm // r, stride=r), :] for c in range(HEAD_PAIRS)], axis=1).astype(BF16)


def _qkv_proj(x2, w_qkv, bsz, seq, tm=512):
    n, d = x2.shape
    tiles = seq // tm
    out_specs, out_shape = [], []
    for _ in range(3):
        for _, r in GROUP_PATTERNS:
            out_specs.append(pl.BlockSpec((None, r, tm // r, ATTN_WIDTH),
                                          lambda i: (i // tiles, 0, i % tiles, 0)))
            out_shape.append(jax.ShapeDtypeStruct((bsz, r, seq // r, ATTN_WIDTH), BF16))
    return pl.pallas_call(
        _qkv_kernel,
        grid=(n // tm,),
        in_specs=[pl.BlockSpec((tm, d), lambda i: (i, 0)), _const_spec(w_qkv.shape)],
        out_specs=out_specs,
        out_shape=out_shape,
        scratch_shapes=[pltpu.VMEM((HEAD_PAIRS, tm, LANES), F32)],
        compiler_params=pltpu.CompilerParams(dimension_semantics=("arbitrary",),
                                             vmem_limit_bytes=VMEM_LIMIT),
        name="qkv_proj",
    )(x2, w_qkv)


LOG2E = math.log2(math.e)


def _attn_bias():
    slopes = 2.0 ** (-8.0 * jnp.arange(1, N_SLOTS + 1, dtype=F32) / N_SLOTS)
    qi = jnp.arange(BLOCK)[:, None]
    kj = jnp.arange(2 * BLOCK)[None, :]
    dist = BLOCK + qi - kj
    valid = (dist >= 0) & (dist <= BLOCK)
    valid = jnp.stack([valid, valid & (kj >= BLOCK)])
    out = []
    for _, r in GROUP_PATTERNS:
        alibi = (-LOG2E * slopes)[:, None, None, None] * (dist * r).astype(F32)[None, None]
        out.append(jnp.where(valid[None], alibi, MASK_VALUE).astype(F32))
    return jnp.stack(out)


def _attn_kernel(*refs):
    ins, bias_ref, out_ref = refs[:5 * N_GROUPS], refs[5 * N_GROUPS], refs[5 * N_GROUPS + 1]
    scr = refs[5 * N_GROUPS + 2:]
    kv_scr, (o_stage, l_stage, m_stage, ms_stage) = scr[:3 * N_GROUPS], scr[3 * N_GROUPS:]
    first_span = pl.program_id(1) == 0
    lane = lax.broadcasted_iota(jnp.int32, (BLOCK, LANES), 1)
    low = lane < HEAD_DIM

    for g, (_, r) in enumerate(GROUP_PATTERNS):
        q_ref, kc_ref, kp_ref, vc_ref, vp_ref = ins[5 * g:5 * g + 5]
        k_scr, v0_scr, v1_scr = kv_scr[3 * g:3 * g + 3]
        nb = UNITS // r
        k_scr[:, :BLOCK, :] = kp_ref[...]
        k_scr[:, BLOCK:, :] = kc_ref[...]
        for dst, src in ((slice(0, BLOCK), vp_ref), (slice(BLOCK, None), vc_ref)):
            v = src[...]
            low_v = lax.broadcasted_iota(jnp.int32, v.shape, 2) < HEAD_DIM
            ones = jnp.ones_like(v)
            v0_scr[:, dst, :] = jnp.where(low_v, v, ones)
            v1_scr[:, dst, :] = jnp.where(low_v, ones, v)

        def unit(u, carry, g=g, r=r, nb=nb, q_ref=q_ref, k_scr=k_scr, v_scrs=(v0_scr, v1_scr)):
            p, n = u // nb, u % nb
            row0 = pl.multiple_of(n * BLOCK, BLOCK)
            q = q_ref[p, pl.ds(row0, BLOCK), :]
            kcat = k_scr[p, pl.ds(row0, 2 * BLOCK), :]
            variant = jnp.where(jnp.logical_and(first_span, n == 0), 1, 0)
            acc, tops = [], []
            for hh in range(2):
                keep = low if hh == 0 else jnp.logical_not(low)
                qm = jnp.where(keep, q, jnp.zeros_like(q))
                s = lax.dot_general(qm, kcat, (((1,), (1,)), ((), ())), preferred_element_type=F32)
                s = s + bias_ref[g, hh, variant]
                m = jnp.max(s, axis=1, keepdims=True)
                e = jnp.exp2(s - m).astype(BF16)
                acc.append(jnp.dot(e, v_scrs[hh][p, pl.ds(row0, 2 * BLOCK), :], preferred_element_type=F32))
                tops.append(m)
            start = n * (BLOCK * r) + p
            rows = pl.ds(start, BLOCK) if r == 1 else pl.ds(start, BLOCK, stride=r)
            o_stage[g, rows, :] = jnp.where(low, acc[0], acc[1])
            l_stage[g, rows, :] = jnp.where(low, acc[1], acc[0])
            m_stage[g, rows, :] = jnp.where(low, tops[0], tops[1])
            ms_stage[g, rows, :] = jnp.where(low, tops[1], tops[0])
            return carry

        lax.fori_loop(0, UNITS, unit, 0, unroll=UNIT_UNROLL)

    def mix(i, carry):
        rows = pl.ds(pl.multiple_of(i * BLOCK, BLOCK), BLOCK)

        def weighted_sum(top_stage, val_stage):
            tops = [top_stage[g, rows, :] for g in range(N_GROUPS)]
            top = functools.reduce(jnp.maximum, tops)
            return sum(jnp.exp2(tops[g] - top) * val_stage[g, rows, :] for g in range(N_GROUPS))

        num = weighted_sum(m_stage, o_stage)
        den = pltpu.roll(weighted_sum(ms_stage, l_stage), HEAD_DIM, 1)
        out_ref[rows, :] = (num / den).astype(BF16)
        return carry

    lax.fori_loop(0, UNITS, mix, 0, unroll=4)


def _attention(qkv, bsz, seq):
    spans = seq // SPAN
    in_specs, args, scratch = [], [], []
    for g, (_, r) in enumerate(GROUP_PATTERNS):
        nb = UNITS // r
        cur = pl.BlockSpec((None, r, SPAN // r, LANES), lambda b, s, hp: (b, 0, s, hp))
        prev = pl.BlockSpec((None, r, BLOCK, LANES),
                            lambda b, s, hp, nb=nb: (b, 0, jnp.maximum(nb * s - 1, 0), hp))
        q, k, v = qkv[g], qkv[N_GROUPS + g], qkv[2 * N_GROUPS + g]
        in_specs += [cur, cur, prev, cur, prev]
        args += [q, k, k, v, v]
        scratch += [pltpu.VMEM((r, SPAN // r + BLOCK, LANES), BF16)] * 3
    in_specs.append(pl.BlockSpec((N_GROUPS, 2, 2, BLOCK, 2 * BLOCK), lambda b, s, hp: (0, hp, 0, 0, 0)))
    scratch += [pltpu.VMEM((N_GROUPS, SPAN, LANES), F32)] * 4
    return pl.pallas_call(
        _attn_kernel,
        grid=(bsz, spans, HEAD_PAIRS),
        in_specs=in_specs,
        out_specs=pl.BlockSpec((SPAN, LANES), lambda b, s, hp: (b * spans + s, hp)),
        out_shape=jax.ShapeDtypeStruct((bsz * seq, ATTN_WIDTH), BF16),
        scratch_shapes=scratch,
        compiler_params=pltpu.CompilerParams(dimension_semantics=("arbitrary",) * 3,
                                             vmem_limit_bytes=VMEM_LIMIT),
        name="band_attn",
    )(*args, _attn_bias())


def _ssm_weights(lam_re, lam_im, log_dt, b_re, b_im, c_re, c_im):
    t = SSM_CHUNK
    lr, li = lam_re.astype(F32), lam_im.astype(F32)
    dt = jnp.exp(log_dt.astype(F32))[:, None]
    mag = jnp.exp(lr * dt)
    ab_re, ab_im = mag * jnp.cos(li * dt), mag * jnp.sin(li * dt)
    nr, ni = ab_re - 1.0, ab_im
    den = lr * lr + li * li
    coef_re = (nr * lr + ni * li) / den
    coef_im = (ni * lr - nr * li) / den
    br, bi = b_re.astype(F32), b_im.astype(F32)
    bb_re = coef_re[..., None] * br - coef_im[..., None] * bi
    bb_im = coef_re[..., None] * bi + coef_im[..., None] * br
    pw_re, pw_im = [jnp.ones_like(ab_re)], [jnp.zeros_like(ab_im)]
    for _ in range(t):
        pr, pi = pw_re[-1], pw_im[-1]
        pw_re.append(pr * ab_re - pi * ab_im)
        pw_im.append(pr * ab_im + pi * ab_re)
    pw_re, pw_im = jnp.stack(pw_re), jnp.stack(pw_im)
    cr, ci = c_re.astype(F32), c_im.astype(F32)
    hi = lax.Precision.HIGHEST

    ab_b_re = pw_re[..., None] * bb_re[None] - pw_im[..., None] * bb_im[None]
    ab_b_im = pw_re[..., None] * bb_im[None] + pw_im[..., None] * bb_re[None]
    taps = (jnp.einsum('gop,dgpi->dgio', cr, ab_b_re[:t], precision=hi)
            - jnp.einsum('gop,dgpi->dgio', ci, ab_b_im[:t], precision=hi))
    eye = jnp.eye(GROUPS_PER_SLAB, dtype=F32)
    s = SSM_SLABS
    taps = taps.reshape(t, s, GROUPS_PER_SLAB, SSM_GROUP, SSM_GROUP)
    taps_bd = jnp.einsum('dsgio,gh->dsgiho', taps, eye).reshape(t, s, LANES, LANES)
    lag = jnp.arange(t)[None, :] - jnp.arange(t)[:, None]
    toep = jnp.where((lag >= 0)[None, :, None, :, None],
                     taps_bd[jnp.clip(lag, 0, t - 1)].transpose(2, 0, 3, 1, 4), 0.0)
    toep = toep.reshape(s, t * LANES, t * LANES)

    def to_state(w):
        w = w.reshape(t, s, GROUPS_PER_SLAB, SSM_STATE, SSM_GROUP)
        w = jnp.einsum('dsgpc,gh->sdgchp', w, eye)
        return w.reshape(s, t * LANES, SLAB_STATE)
    w_st = jnp.concatenate([to_state(ab_b_re[:t][::-1]), to_state(ab_b_im[:t][::-1])], axis=2)

    ca_re = cr[None] * pw_re[1:, :, None, :] - ci[None] * pw_im[1:, :, None, :]
    ca_im = cr[None] * pw_im[1:, :, None, :] + ci[None] * pw_re[1:, :, None, :]

    def from_state(w):
        w = w.reshape(t, s, GROUPS_PER_SLAB, SSM_GROUP, SSM_STATE)
        w = jnp.einsum('dsgcp,gh->shpdgc', w, eye)
        return w.reshape(s, SLAB_STATE, t * LANES)
    w_y = jnp.concatenate([toep, from_state(ca_re), from_state(-ca_im)], axis=1)

    a_t = jnp.stack([pw_re[t].reshape(-1), pw_im[t].reshape(-1)])
    return w_y.astype(BF16), w_st.astype(BF16), a_t


def _ssm_kernel(x_ref, wu_ref, wz_ref, wy_ref, wst_ref, at_ref, dskip_ref, wglu_ref, bglu_ref,
                out_ref, u_st, z_scr, s_scr, xp_scr, y_st, carry_scr):
    bsz, tl, d_model = x_ref.shape
    t = SSM_CHUNK
    ch = tl // t
    rows = bsz * tl
    crows = bsz * ch
    tiles_per_slab = SLAB_STATE // LANES
    n_tiles = SSM_SLABS * tiles_per_slab

    @pl.when(pl.program_id(0) == 0)
    def _():
        carry_scr[...] = jnp.zeros_like(carry_scr)

    xb = x_ref[...].reshape(rows, d_model).astype(BF16)
    for c, tile in enumerate(_lane_tiles(jnp.dot(xb, wu_ref[...], preferred_element_type=F32))):
        u_st[c] = tile
    z_scr[...] = jnp.dot(xb, wz_ref[...], preferred_element_type=F32)

    def chunk_lhs(j):
        return jnp.concatenate([u_st[j, pl.ds(sg, crows, stride=t), :] for sg in range(t)], axis=1).astype(BF16)

    for j in range(SSM_SLABS):
        st = jnp.dot(chunk_lhs(j), wst_ref[j], preferred_element_type=F32)
        for c, tile in enumerate(_lane_tiles(st)):
            half, cc = divmod(c, tiles_per_slab)
            s_scr[half * n_tiles + j * tiles_per_slab + cc] = tile

    def chunk_step(k, carry):
        sel = pl.ds(k, bsz, stride=ch)
        new = []
        for c in range(n_tiles):
            c_re, c_im = carry[c], carry[n_tiles + c]
            xp_scr[c, sel, :] = c_re
            xp_scr[n_tiles + c, sel, :] = c_im
            a_re = at_ref[c:c + 1, :]
            a_im = at_ref[n_tiles + c:n_tiles + c + 1, :]
            new.append((a_re * c_re - a_im * c_im + s_scr[c, sel, :],
                        a_re * c_im + a_im * c_re + s_scr[n_tiles + c, sel, :]))
        return tuple(v[0] for v in new) + tuple(v[1] for v in new)

    carry = lax.fori_loop(0, ch, chunk_step, tuple(carry_scr[c] for c in range(2 * n_tiles)))
    for c in range(2 * n_tiles):
        carry_scr[c] = carry[c]

    for j in range(SSM_SLABS):
        xin = [xp_scr[half + j * tiles_per_slab + c] for half in (0, n_tiles) for c in range(tiles_per_slab)]
        lhs = jnp.concatenate([chunk_lhs(j)] + [v.astype(BF16) for v in xin], axis=1)
        yj = jnp.dot(lhs, wy_ref[j], preferred_element_type=F32)
        for tau, tile in enumerate(_lane_tiles(yj)):
            y_st[j, pl.ds(tau, crows, stride=t), :] = tile

    for b in range(bsz):
        rs = slice(b * tl, (b + 1) * tl)
        y = jnp.concatenate([y_st[c, rs, :] for c in range(SSM_SLABS)], axis=1)
        u = jnp.concatenate([u_st[c, rs, :] for c in range(SSM_SLABS)], axis=1)
        y = jax.nn.gelu(y + dskip_ref[...] * u)
        gate = jax.nn.sigmoid(jnp.dot(y.astype(BF16), wglu_ref[...], preferred_element_type=F32) + bglu_ref[...])
        out_ref[b] = (y * gate * jax.nn.silu(z_scr[rs, :])).astype(BF16)


def _ssm_branch(x, w_u, w_z, ssm_w, d_skip, w_glu, b_glu, tl=128):
    bsz, seq, d = x.shape
    t = SSM_CHUNK
    rows, crows = bsz * tl, bsz * tl // t
    w_y, w_st, a_t = ssm_w
    n_tiles2 = 2 * SSM_SLABS * SLAB_STATE // LANES
    a_t = a_t.reshape(n_tiles2, LANES)
    out = pl.pallas_call(
        _ssm_kernel,
        grid=(seq // tl,),
        in_specs=[pl.BlockSpec((bsz, tl, d), lambda i: (0, i, 0)),
                  _const_spec(w_u.shape), _const_spec(w_z.shape), _const_spec(w_y.shape),
                  _const_spec(w_st.shape), _const_spec(a_t.shape), _const_spec((1, SSM_WIDTH)),
                  _const_spec(w_glu.shape), _const_spec((1, SSM_WIDTH))],
        out_specs=pl.BlockSpec((bsz, tl, SSM_WIDTH), lambda i: (0, i, 0)),
        out_shape=jax.ShapeDtypeStruct((bsz, seq, SSM_WIDTH), BF16),
        scratch_shapes=[pltpu.VMEM((SSM_SLABS, rows, LANES), F32),
                        pltpu.VMEM((rows, SSM_WIDTH), F32),
                        pltpu.VMEM((n_tiles2, crows, LANES), F32),
                        pltpu.VMEM((n_tiles2, crows, LANES), F32),
                        pltpu.VMEM((SSM_SLABS, rows, LANES), F32),
                        pltpu.VMEM((n_tiles2, bsz, LANES), F32)],
        compiler_params=pltpu.CompilerParams(dimension_semantics=("arbitrary",),
                                             vmem_limit_bytes=VMEM_LIMIT),
        name="s5_branch",
    )(x, w_u, w_z, w_y, w_st, a_t, d_skip.reshape(1, SSM_WIDTH).astype(F32),
      w_glu, b_glu.reshape(1, SSM_WIDTH).astype(F32))
    return out.reshape(bsz * seq, SSM_WIDTH)


def _merge_kernel(alpha, x_ref, attn_ref, ys_ref, wzg_ref, wa_ref, ws_ref, wo_ref, g_ref, b_ref, out_ref):
    x = x_ref[...]
    d_model = x.shape[1]
    zg = jnp.dot(x.astype(BF16), wzg_ref[...], preferred_element_type=F32)
    z_a = zg[:, :ATTN_WIDTH]
    gate_a = zg[:, ATTN_WIDTH:ATTN_WIDTH + d_model]
    gate_s = zg[:, ATTN_WIDTH + d_model:]
    attn = attn_ref[...].astype(F32) * jax.nn.silu(z_a)
    y_a = jnp.dot(attn.astype(BF16), wa_ref[...], preferred_element_type=F32)
    y_s = jnp.dot(ys_ref[...], ws_ref[...], preferred_element_type=F32)
    merged = jax.nn.sigmoid(gate_a) * y_a + jax.nn.sigmoid(gate_s) * y_s
    out = jnp.dot(merged.astype(BF16), wo_ref[...], preferred_element_type=F32)
    h = alpha * x + out
    mu = jnp.mean(h, axis=-1, keepdims=True)
    hc = h - mu
    var = jnp.mean(hc * hc, axis=-1, keepdims=True)
    out_ref[...] = hc * lax.rsqrt(var + LN_EPS) * g_ref[...] + b_ref[...]


def _merge(x2, attn, ys, w_zg, w_attn_up, w_ssm_up, w_o, ln_g, ln_b, alpha, tm=512):
    n, d = x2.shape
    row = lambda width: pl.BlockSpec((tm, width), lambda i: (i, 0))
    return pl.pallas_call(
        functools.partial(_merge_kernel, alpha),
        grid=(n // tm,),
        in_specs=[row(d), row(ATTN_WIDTH), row(SSM_WIDTH),
                  _const_spec(w_zg.shape), _const_spec(w_attn_up.shape), _const_spec(w_ssm_up.shape),
                  _const_spec(w_o.shape), _const_spec((1, d)), _const_spec((1, d))],
        out_specs=row(d),
        out_shape=jax.ShapeDtypeStruct((n, d), F32),
        compiler_params=pltpu.CompilerParams(dimension_semantics=("arbitrary",),
                                             vmem_limit_bytes=VMEM_LIMIT),
        name="merge_out_ln",
    )(x2, attn, ys, w_zg, w_attn_up, w_ssm_up, w_o,
      ln_g.reshape(1, d).astype(F32), ln_b.reshape(1, d).astype(F32))


def _layer(h, w_in, lam_re, lam_im, log_dt, b_re, b_im, c_re, c_im, d_skip,
           w_glu, b_glu, w_attn_up, w_ssm_up, w_o, ln_g, ln_b, alpha):
    bsz, seq, d = h.shape
    assert seq % SPAN == 0 and all(w // r == BLOCK for w, r in GROUP_PATTERNS)
    x2 = h.reshape(bsz * seq, d)
    o0 = 3 * QKV_WIDTH
    o1 = o0 + ATTN_WIDTH
    w_qkv = jnp.concatenate([w_in[:, :QKV_WIDTH] * (LOG2E / math.sqrt(HEAD_DIM)), w_in[:, QKV_WIDTH:o0]],
                            axis=1).astype(BF16)
    w_u = w_in[:, o1:o1 + SSM_WIDTH].astype(BF16)
    w_z = w_in[:, o1 + SSM_WIDTH:o1 + 2 * SSM_WIDTH].astype(BF16)
    w_zg = jnp.concatenate([w_in[:, o0:o1], w_in[:, o1 + 2 * SSM_WIDTH:]], axis=1).astype(BF16)

    qkv = _qkv_proj(x2, w_qkv, bsz, seq)
    attn = _attention(qkv, bsz, seq)
    ssm_w = _ssm_weights(lam_re, lam_im, log_dt, b_re, b_im, c_re, c_im)
    ys = _ssm_branch(h, w_u, w_z, ssm_w, d_skip, w_glu.astype(BF16), b_glu)
    out = _merge(x2, attn, ys, w_zg, w_attn_up.astype(BF16), w_ssm_up.astype(BF16),
                 w_o.astype(BF16), ln_g, ln_b, alpha)
    return out.reshape(bsz, seq, d)


def kernel(x, w_in, lam_re, lam_im, log_dt, b_re, b_im, c_re, c_im, d_skip,
           w_glu, b_glu, w_attn_up, w_ssm_up, w_o, ln_g, ln_b):
    depth = w_in.shape[0]
    alpha = (2.0 * depth) ** 0.25
    h = x
    for layer in range(depth):
        h = _layer(h, w_in[layer], lam_re[layer], lam_im[layer], log_dt[layer], b_re[layer], b_im[layer],
                   c_re[layer], c_im[layer], d_skip[layer], w_glu[layer], b_glu[layer], w_attn_up[layer],
                   w_ssm_up[layer], w_o[layer], ln_g[layer], ln_b[layer], alpha)
    return h
```

```python
import functools
import math

import jax
import jax.numpy as jnp
import numpy as np
from jax import lax
from jax.experimental import pallas as pl
from jax.experimental.pallas import tpu as pltpu

F32 = jnp.float32
BF16 = jnp.bfloat16

HEAD_DIM = 64
N_SLOTS = 8
GROUP_PATTERNS = ((128, 1), (512, 4), (2048, 16))
N_GROUPS = len(GROUP_PATTERNS)
ATTN_WIDTH = N_SLOTS * HEAD_DIM
QKV_WIDTH = N_GROUPS * ATTN_WIDTH
BLOCK = 128
SPAN = GROUP_PATTERNS[-1][1] * BLOCK
UNITS = SPAN // BLOCK
SSM_WIDTH = 512
SSM_GROUP = 16
SSM_GROUPS = SSM_WIDTH // SSM_GROUP
SSM_STATE = 64
LN_EPS = 1e-5
MASK_VALUE = -1e30
LOG2E = math.log2(math.e)
Q_SCALE = LOG2E / math.sqrt(HEAD_DIM)

LANES = 128
HEAD_PAIRS = ATTN_WIDTH // LANES
PHASE_SPLIT = 4
UNIT_UNROLL = 16
SSM_CHUNK = 8
SSM_SLABS = SSM_WIDTH // LANES
GROUPS_PER_SLAB = LANES // SSM_GROUP
SLAB_STATE = GROUPS_PER_SLAB * SSM_STATE
VMEM_LIMIT = 56 * 1024 * 1024


def _const_spec(shape):
    nd = len(shape)
    return pl.BlockSpec(shape, lambda *_: (0,) * nd, pipeline_mode=pl.Buffered(1))


def _col_block_spec(rows, width, start):
    assert start % width == 0
    return pl.BlockSpec((rows, width), lambda *_: (0, start // width), pipeline_mode=pl.Buffered(1))


def _lane_tiles(x):
    return [x[:, c * LANES:(c + 1) * LANES] for c in range(x.shape[1] // LANES)]


def _qkv_kernel(x_ref, w_ref, *refs):
    out_refs, x_stage, stage = refs[:-2], refs[-2], refs[-1]
    tm = x_ref.shape[0]
    sub = tm // PHASE_SPLIT
    x = x_ref[...]
    for c, tile in enumerate(_lane_tiles(x)):
        x_stage[c] = tile
    xb = {1: x.astype(BF16),
          PHASE_SPLIT: jnp.concatenate(
              [jnp.concatenate([x_stage[c, pl.ds(q, sub, stride=PHASE_SPLIT), :] for c in range(x_stage.shape[0])],
                               axis=1) for q in range(PHASE_SPLIT)], axis=0).astype(BF16)}
    for kind in range(3):
        for g, (_, r) in enumerate(GROUP_PATTERNS):
            col = kind * QKV_WIDTH + g * ATTN_WIDTH
            res = jnp.dot(xb[min(r, PHASE_SPLIT)], w_ref[:, col:col + ATTN_WIDTH], preferred_element_type=F32)
            if kind == 0:
                res = res * Q_SCALE
            o_ref = out_refs[kind * N_GROUPS + g]
            if r == 1:
                o_ref[0] = res.astype(BF16)
            elif r == PHASE_SPLIT:
                for q in range(r):
                    o_ref[q] = res[q * sub:(q + 1) * sub].astype(BF16)
            else:
                for c, tile in enumerate(_lane_tiles(res)):
                    stage[c] = tile
                for q in range(PHASE_SPLIT):
                    for s in range(PHASE_SPLIT):
                        o_ref[q + PHASE_SPLIT * s] = jnp.concatenate(
                            [stage[c, pl.ds(q * sub + s, sub // PHASE_SPLIT, stride=PHASE_SPLIT), :]
                             for c in range(HEAD_PAIRS)], axis=1).astype(BF16)


def _qkv_proj(x2, w_in, bsz, seq, tm=512):
    n, d = x2.shape
    tiles = seq // tm
    out_specs, out_shape = [], []
    for _ in range(3):
        for _, r in GROUP_PATTERNS:
            out_specs.append(pl.BlockSpec((None, r, tm // r, ATTN_WIDTH),
                                          lambda i: (i // tiles, 0, i % tiles, 0)))
            out_shape.append(jax.ShapeDtypeStruct((bsz, r, seq // r, ATTN_WIDTH), BF16))
    return pl.pallas_call(
        _qkv_kernel,
        grid=(n // tm,),
        in_specs=[pl.BlockSpec((tm, d), lambda i: (i, 0)), _col_block_spec(d, 3 * QKV_WIDTH, 0)],
        out_specs=out_specs,
        out_shape=out_shape,
        scratch_shapes=[pltpu.VMEM((d // LANES, tm, LANES), F32), pltpu.VMEM((HEAD_PAIRS, tm, LANES), F32)],
        compiler_params=pltpu.CompilerParams(dimension_semantics=("arbitrary",),
                                             vmem_limit_bytes=VMEM_LIMIT),
        name="qkv_proj",
    )(x2, w_in)


def _attn_bias():
    slopes = 2.0 ** (-8.0 * np.arange(1, N_SLOTS + 1) / N_SLOTS)
    qi = np.arange(BLOCK)[:, None]
    kj = np.arange(2 * BLOCK)[None, :]
    dist = BLOCK + qi - kj
    valid = (dist >= 0) & (dist <= BLOCK)
    valid = np.stack([valid, valid & (kj >= BLOCK)])
    out = []
    for _, r in GROUP_PATTERNS:
        alibi = (-LOG2E * slopes)[:, None, None, None] * (dist * r)[None, None]
        out.append(np.where(valid[None], alibi, MASK_VALUE))
    return jnp.asarray(np.stack(out).astype(np.float32))


def _attn_kernel(*refs):
    ins, bias_ref, out_ref = refs[:5 * N_GROUPS], refs[5 * N_GROUPS], refs[5 * N_GROUPS + 1]
    scr = refs[5 * N_GROUPS + 2:]
    kv_scr, (o_stage, l_stage, m_stage, ms_stage) = scr[:3 * N_GROUPS], scr[3 * N_GROUPS:]
    first_span = pl.program_id(2) == 0
    lane = lax.broadcasted_iota(jnp.int32, (BLOCK, LANES), 1)
    low = lane < HEAD_DIM

    for g, (_, r) in enumerate(GROUP_PATTERNS):
        q_ref, kc_ref, kp_ref, vc_ref, vp_ref = ins[5 * g:5 * g + 5]
        k_scr, v0_scr, v1_scr = kv_scr[3 * g:3 * g + 3]
        nb = UNITS // r
        k_scr[:, :BLOCK, :] = kp_ref[...]
        k_scr[:, BLOCK:, :] = kc_ref[...]
        for dst, src in ((slice(0, BLOCK), vp_ref), (slice(BLOCK, None), vc_ref)):
            v = src[...]
            low_v = lax.broadcasted_iota(jnp.int32, v.shape, 2) < HEAD_DIM
            ones = jnp.ones_like(v)
            v0_scr[:, dst, :] = jnp.where(low_v, v, ones)
            v1_scr[:, dst, :] = jnp.where(low_v, ones, v)

        def unit(u, carry, g=g, r=r, nb=nb, q_ref=q_ref, k_scr=k_scr, v_scrs=(v0_scr, v1_scr)):
            p, n = u // nb, u % nb
            row0 = pl.multiple_of(n * BLOCK, BLOCK)
            q = q_ref[p, pl.ds(row0, BLOCK), :]
            kcat = k_scr[p, pl.ds(row0, 2 * BLOCK), :]
            variant = jnp.where(jnp.logical_and(first_span, n == 0), 1, 0)
            acc, tops = [], []
            for hh in range(2):
                keep = low if hh == 0 else jnp.logical_not(low)
                qm = jnp.where(keep, q, jnp.zeros_like(q))
                s = lax.dot_general(qm, kcat, (((1,), (1,)), ((), ())), preferred_element_type=F32)
                s = s + bias_ref[g, hh, variant]
                m = jnp.max(s, axis=1, keepdims=True)
                e = jnp.exp2(s - m).astype(BF16)
                acc.append(jnp.dot(e, v_scrs[hh][p, pl.ds(row0, 2 * BLOCK), :], preferred_element_type=F32))
                tops.append(m)
            start = n * (BLOCK * r) + p
            rows = pl.ds(start, BLOCK) if r == 1 else pl.ds(start, BLOCK, stride=r)
            o_stage[g, rows, :] = jnp.where(low, acc[0], acc[1])
            l_stage[g, rows, :] = jnp.where(low, acc[1], acc[0])
            m_stage[g, rows, :] = jnp.where(low, tops[0], tops[1])
            ms_stage[g, rows, :] = jnp.where(low, tops[1], tops[0])
            return carry

        lax.fori_loop(0, UNITS, unit, 0, unroll=UNIT_UNROLL)

    def mix(i, carry):
        rows = pl.ds(pl.multiple_of(i * BLOCK, BLOCK), BLOCK)

        def weighted_sum(top_stage, val_stage):
            tops = [top_stage[g, rows, :] for g in range(N_GROUPS)]
            top = functools.reduce(jnp.maximum, tops)
            return sum(jnp.exp2(tops[g] - top) * val_stage[g, rows, :] for g in range(N_GROUPS))

        num = weighted_sum(m_stage, o_stage)
        den = pltpu.roll(weighted_sum(ms_stage, l_stage), HEAD_DIM, 1)
        out_ref[rows, :] = (num / den).astype(BF16)
        return carry

    lax.fori_loop(0, UNITS, mix, 0, unroll=4)


def _attention(qkv, bsz, seq):
    spans = seq // SPAN
    in_specs, args, scratch = [], [], []
    for g, (_, r) in enumerate(GROUP_PATTERNS):
        nb = UNITS // r
        cur = pl.BlockSpec((None, r, SPAN // r, LANES), lambda hp, b, s: (b, 0, s, hp))
        prev = pl.BlockSpec((None, r, BLOCK, LANES),
                            lambda hp, b, s, nb=nb: (b, 0, jnp.maximum(nb * s - 1, 0), hp))
        q, k, v = qkv[g], qkv[N_GROUPS + g], qkv[2 * N_GROUPS + g]
        in_specs += [cur, cur, prev, cur, prev]
        args += [q, k, k, v, v]
        scratch += [pltpu.VMEM((r, SPAN // r + BLOCK, LANES), BF16)] * 3
    in_specs.append(pl.BlockSpec((N_GROUPS, 2, 2, BLOCK, 2 * BLOCK), lambda hp, b, s: (0, hp, 0, 0, 0)))
    scratch += [pltpu.VMEM((N_GROUPS, SPAN, LANES), F32)] * 4
    return pl.pallas_call(
        _attn_kernel,
        grid=(HEAD_PAIRS, bsz, spans),
        in_specs=in_specs,
        out_specs=pl.BlockSpec((SPAN, LANES), lambda hp, b, s: (b * spans + s, hp)),
        out_shape=jax.ShapeDtypeStruct((bsz * seq, ATTN_WIDTH), BF16),
        scratch_shapes=scratch,
        compiler_params=pltpu.CompilerParams(dimension_semantics=("arbitrary",) * 3,
                                             vmem_limit_bytes=VMEM_LIMIT),
        name="band_attn",
    )(*args, _attn_bias())


def _ssm_weights(lam_re, lam_im, log_dt, b_re, b_im, c_re, c_im):
    t = SSM_CHUNK
    lr, li = lam_re.astype(F32), lam_im.astype(F32)
    dt = jnp.exp(log_dt.astype(F32))[:, None]
    mag = jnp.exp(lr * dt)
    ab_re, ab_im = mag * jnp.cos(li * dt), mag * jnp.sin(li * dt)
    nr, ni = ab_re - 1.0, ab_im
    den = lr * lr + li * li
    coef_re = (nr * lr + ni * li) / den
    coef_im = (ni * lr - nr * li) / den
    br, bi = b_re.astype(F32), b_im.astype(F32)
    bb_re = coef_re[..., None] * br - coef_im[..., None] * bi
    bb_im = coef_re[..., None] * bi + coef_im[..., None] * br
    pw_re, pw_im = [jnp.ones_like(ab_re)], [jnp.zeros_like(ab_im)]
    for _ in range(t):
        pr, pi = pw_re[-1], pw_im[-1]
        pw_re.append(pr * ab_re - pi * ab_im)
        pw_im.append(pr * ab_im + pi * ab_re)
    pw_re, pw_im = jnp.stack(pw_re), jnp.stack(pw_im)
    cr, ci = c_re.astype(F32), c_im.astype(F32)
    n_ch = SSM_GROUPS * SSM_GROUP

    xb_re = pw_re[:t, :, :, None] * bb_re[None] - pw_im[:t, :, :, None] * bb_im[None]
    xb_im = pw_re[:t, :, :, None] * bb_im[None] + pw_im[:t, :, :, None] * bb_re[None]
    xb = jnp.stack([xb_re, xb_im]).transpose(0, 1, 2, 4, 3).reshape(2, t, n_ch, SSM_STATE)
    xb = jnp.concatenate([xb, xb], axis=-1)
    ca_re = cr[None] * pw_re[:, :, None, :] - ci[None] * pw_im[:, :, None, :]
    ca_im = cr[None] * pw_im[:, :, None, :] + ci[None] * pw_re[:, :, None, :]
    ca = jnp.stack([ca_re, -ca_im]).transpose(0, 1, 4, 2, 3).reshape(2, t + 1, SSM_STATE, n_ch)

    w_y, w_st = pl.pallas_call(
        _ssm_expand_kernel,
        grid=(SSM_SLABS,),
        in_specs=[pl.BlockSpec((2, t, LANES, LANES), lambda j: (0, 0, j, 0)),
                  pl.BlockSpec((2, t + 1, SSM_STATE, LANES), lambda j: (0, 0, 0, j))],
        out_specs=[pl.BlockSpec((None, t * LANES + 2 * SLAB_STATE, t * LANES), lambda j: (j, 0, 0)),
                   pl.BlockSpec((None, t * LANES, 2 * SLAB_STATE), lambda j: (j, 0, 0))],
        out_shape=[jax.ShapeDtypeStruct((SSM_SLABS, t * LANES + 2 * SLAB_STATE, t * LANES), BF16),
                   jax.ShapeDtypeStruct((SSM_SLABS, t * LANES, 2 * SLAB_STATE), BF16)],
        compiler_params=pltpu.CompilerParams(dimension_semantics=("arbitrary",),
                                             vmem_limit_bytes=VMEM_LIMIT),
        name="s5_expand",
    )(xb, ca)
    a_t = jnp.stack([pw_re[t].reshape(-1), pw_im[t].reshape(-1)])
    return w_y, w_st, a_t


def _ssm_expand_kernel(xb_ref, ca_ref, wy_ref, wst_ref):
    t = SSM_CHUNK
    keep_st = (lax.broadcasted_iota(jnp.int32, (LANES, SLAB_STATE), 0) // SSM_GROUP
               == lax.broadcasted_iota(jnp.int32, (LANES, SLAB_STATE), 1) // SSM_STATE)
    keep_out = (lax.broadcasted_iota(jnp.int32, (SLAB_STATE, LANES), 0) // SSM_STATE
                == lax.broadcasted_iota(jnp.int32, (SLAB_STATE, LANES), 1) // SSM_GROUP)

    def to_state(d):
        parts = [jnp.where(keep_st, jnp.concatenate([xb_ref[ri, d]] * (SLAB_STATE // LANES), axis=1), 0.0)
                 for ri in range(2)]
        return jnp.concatenate(parts, axis=1)

    def from_state(d):
        parts = [jnp.where(keep_out, jnp.concatenate([ca_ref[ri, d]] * GROUPS_PER_SLAB, axis=0), 0.0)
                 for ri in range(2)]
        return jnp.concatenate(parts, axis=0)

    c0 = from_state(0)
    zero = jnp.zeros((LANES, LANES), BF16)
    for d in range(t):
        st = to_state(d)
        sg = t - 1 - d
        wst_ref[sg * LANES:(sg + 1) * LANES, :] = st.astype(BF16)
        taps = jnp.dot(st, c0, preferred_element_type=F32, precision=lax.Precision.HIGHEST).astype(BF16)
        for sg in range(t - d):
            wy_ref[sg * LANES:(sg + 1) * LANES, (sg + d) * LANES:(sg + d + 1) * LANES] = taps
        if d:
            for tau in range(t - d):
                wy_ref[(tau + d) * LANES:(tau + d + 1) * LANES, tau * LANES:(tau + 1) * LANES] = zero
    for tau in range(t):
        wy_ref[t * LANES:, tau * LANES:(tau + 1) * LANES] = from_state(tau + 1).astype(BF16)


def _ssm_kernel(x_ref, wu_ref, wz_ref, wy_ref, wst_ref, at_ref, dskip_ref, wglu_ref, bglu_ref,
                out_ref, u_st, z_scr, s_scr, xp_scr, y_st, carry_scr):
    bsz, tl, d_model = x_ref.shape
    t = SSM_CHUNK
    ch = tl // t
    rows = bsz * tl
    crows = bsz * ch
    tiles_per_slab = SLAB_STATE // LANES
    n_tiles = SSM_SLABS * tiles_per_slab

    @pl.when(pl.program_id(0) == 0)
    def _():
        carry_scr[...] = jnp.zeros_like(carry_scr)

    xb = x_ref[...].reshape(rows, d_model).astype(BF16)
    for c, tile in enumerate(_lane_tiles(jnp.dot(xb, wu_ref[...], preferred_element_type=F32))):
        u_st[c] = tile
    z_scr[...] = jnp.dot(xb, wz_ref[...], preferred_element_type=F32)

    def chunk_lhs(j):
        return jnp.concatenate([u_st[j, pl.ds(sg, crows, stride=t), :] for sg in range(t)], axis=1).astype(BF16)

    for j in range(SSM_SLABS):
        st = jnp.dot(chunk_lhs(j), wst_ref[j], preferred_element_type=F32)
        for c, tile in enumerate(_lane_tiles(st)):
            half, cc = divmod(c, tiles_per_slab)
            s_scr[half * n_tiles + j * tiles_per_slab + cc] = tile

    def chunk_step(k, carry):
        sel = pl.ds(k, bsz, stride=ch)
        new = []
        for c in range(n_tiles):
            c_re, c_im = carry[c], carry[n_tiles + c]
            xp_scr[c, sel, :] = c_re
            xp_scr[n_tiles + c, sel, :] = c_im
            a_re = at_ref[c:c + 1, :]
            a_im = at_ref[n_tiles + c:n_tiles + c + 1, :]
            new.append((a_re * c_re - a_im * c_im + s_scr[c, sel, :],
                        a_re * c_im + a_im * c_re + s_scr[n_tiles + c, sel, :]))
        return tuple(v[0] for v in new) + tuple(v[1] for v in new)

    carry = lax.fori_loop(0, ch, chunk_step, tuple(carry_scr[c] for c in range(2 * n_tiles)))
    for c in range(2 * n_tiles):
        carry_scr[c] = carry[c]

    for j in range(SSM_SLABS):
        xin = [xp_scr[half + j * tiles_per_slab + c] for half in (0, n_tiles) for c in range(tiles_per_slab)]
        lhs = jnp.concatenate([chunk_lhs(j)] + [v.astype(BF16) for v in xin], axis=1)
        yj = jnp.dot(lhs, wy_ref[j], preferred_element_type=F32)
        for tau, tile in enumerate(_lane_tiles(yj)):
            y_st[j, pl.ds(tau, crows, stride=t), :] = tile

    for b in range(bsz):
        rs = slice(b * tl, (b + 1) * tl)
        y = jnp.concatenate([y_st[c, rs, :] for c in range(SSM_SLABS)], axis=1)
        u = jnp.concatenate([u_st[c, rs, :] for c in range(SSM_SLABS)], axis=1)
        y = jax.nn.gelu(y + dskip_ref[...] * u)
        gate = jax.nn.sigmoid(jnp.dot(y.astype(BF16), wglu_ref[...], preferred_element_type=F32) + bglu_ref[...])
        out_ref[b] = (y * gate * jax.nn.silu(z_scr[rs, :])).astype(BF16)


def _ssm_branch(x, w_in, ssm_w, d_skip, w_glu, b_glu, tl=128):
    bsz, seq, d = x.shape
    u_col = 3 * QKV_WIDTH + ATTN_WIDTH
    t = SSM_CHUNK
    rows, crows = bsz * tl, bsz * tl // t
    w_y, w_st, a_t = ssm_w
    n_tiles2 = 2 * SSM_SLABS * SLAB_STATE // LANES
    a_t = a_t.reshape(n_tiles2, LANES)
    out = pl.pallas_call(
        _ssm_kernel,
        grid=(seq // tl,),
        in_specs=[pl.BlockSpec((bsz, tl, d), lambda i: (0, i, 0)),
                  _col_block_spec(d, SSM_WIDTH, u_col), _col_block_spec(d, SSM_WIDTH, u_col + SSM_WIDTH),
                  _const_spec(w_y.shape),
                  _const_spec(w_st.shape), _const_spec(a_t.shape), _const_spec((1, SSM_WIDTH)),
                  _const_spec(w_glu.shape), _const_spec((1, SSM_WIDTH))],
        out_specs=pl.BlockSpec((bsz, tl, SSM_WIDTH), lambda i: (0, i, 0)),
        out_shape=jax.ShapeDtypeStruct((bsz, seq, SSM_WIDTH), BF16),
        scratch_shapes=[pltpu.VMEM((SSM_SLABS, rows, LANES), F32),
                        pltpu.VMEM((rows, SSM_WIDTH), F32),
                        pltpu.VMEM((n_tiles2, crows, LANES), F32),
                        pltpu.VMEM((n_tiles2, crows, LANES), F32),
                        pltpu.VMEM((SSM_SLABS, rows, LANES), F32),
                        pltpu.VMEM((n_tiles2, bsz, LANES), F32)],
        compiler_params=pltpu.CompilerParams(dimension_semantics=("arbitrary",),
                                             vmem_limit_bytes=VMEM_LIMIT),
        name="s5_branch",
    )(x, w_in, w_in, w_y, w_st, a_t, d_skip.reshape(1, SSM_WIDTH).astype(F32),
      w_glu, b_glu.reshape(1, SSM_WIDTH).astype(F32))
    return out.reshape(bsz * seq, SSM_WIDTH)


def _merge_kernel(alpha, x_ref, attn_ref, ys_ref, wz_ref, wg_ref, wa_ref, ws_ref, wo_ref, g_ref, b_ref, out_ref):
    x = x_ref[...]
    d_model = x.shape[1]
    xb = x.astype(BF16)
    z_a = jnp.dot(xb, wz_ref[...], preferred_element_type=F32)
    gates = jnp.dot(xb, wg_ref[...], preferred_element_type=F32)
    gate_a = gates[:, :d_model]
    gate_s = gates[:, d_model:]
    attn = attn_ref[...].astype(F32) * jax.nn.silu(z_a)
    y_a = jnp.dot(attn.astype(BF16), wa_ref[...], preferred_element_type=F32)
    y_s = jnp.dot(ys_ref[...], ws_ref[...], preferred_element_type=F32)
    merged = jax.nn.sigmoid(gate_a) * y_a + jax.nn.sigmoid(gate_s) * y_s
    out = jnp.dot(merged.astype(BF16), wo_ref[...], preferred_element_type=F32)
    h = alpha * x + out
    mu = jnp.mean(h, axis=-1, keepdims=True)
    hc = h - mu
    var = jnp.mean(hc * hc, axis=-1, keepdims=True)
    out_ref[...] = hc * lax.rsqrt(var + LN_EPS) * g_ref[...] + b_ref[...]


def _merge(x2, attn, ys, w_in, w_attn_up, w_ssm_up, w_o, ln_g, ln_b, alpha, tm=512):
    n, d = x2.shape
    row = lambda width: pl.BlockSpec((tm, width), lambda i: (i, 0))
    z_col = 3 * QKV_WIDTH
    gate_col = z_col + ATTN_WIDTH + 2 * SSM_WIDTH
    return pl.pallas_call(
        functools.partial(_merge_kernel, alpha),
        grid=(n // tm,),
        in_specs=[row(d), row(ATTN_WIDTH), row(SSM_WIDTH),
                  _col_block_spec(d, ATTN_WIDTH, z_col), _col_block_spec(d, 2 * d, gate_col),
                  _const_spec(w_attn_up.shape), _const_spec(w_ssm_up.shape),
                  _const_spec(w_o.shape), _const_spec((1, d)), _const_spec((1, d))],
        out_specs=row(d),
        out_shape=jax.ShapeDtypeStruct((n, d), F32),
        compiler_params=pltpu.CompilerParams(dimension_semantics=("arbitrary",),
                                             vmem_limit_bytes=VMEM_LIMIT),
        name="merge_out_ln",
    )(x2, attn, ys, w_in, w_in, w_attn_up, w_ssm_up, w_o,
      ln_g.reshape(1, d).astype(F32), ln_b.reshape(1, d).astype(F32))


def _layer(h, w_in, lam_re, lam_im, log_dt, b_re, b_im, c_re, c_im, d_skip,
           w_glu, b_glu, w_attn_up, w_ssm_up, w_o, ln_g, ln_b, alpha):
    bsz, seq, d = h.shape
    assert seq % SPAN == 0 and all(w // r == BLOCK for w, r in GROUP_PATTERNS)
    assert tuple(r for _, r in GROUP_PATTERNS) == (1, PHASE_SPLIT, PHASE_SPLIT * PHASE_SPLIT)
    x2 = h.reshape(bsz * seq, d)
    w_in = w_in.astype(BF16)

    qkv = _qkv_proj(x2, w_in, bsz, seq)
    attn = _attention(qkv, bsz, seq)
    ssm_w = _ssm_weights(lam_re, lam_im, log_dt, b_re, b_im, c_re, c_im)
    ys = _ssm_branch(h, w_in, ssm_w, d_skip, w_glu.astype(BF16), b_glu)
    out = _merge(x2, attn, ys, w_in, w_attn_up.astype(BF16), w_ssm_up.astype(BF16),
                 w_o.astype(BF16), ln_g, ln_b, alpha)
    return out.reshape(bsz, seq, d)


def kernel(x, w_in, lam_re, lam_im, log_dt, b_re, b_im, c_re, c_im, d_skip,
           w_glu, b_glu, w_attn_up, w_ssm_up, w_o, ln_g, ln_b):
    depth = w_in.shape[0]
    alpha = (2.0 * depth) ** 0.25
    h = x
    for layer in range(depth):
        h = _layer(h, w_in[layer], lam_re[layer], lam_im[layer], log_dt[layer], b_re[layer], b_im[layer],
                   c_re[layer], c_im[layer], d_skip[layer], w_glu[layer], b_glu[layer], w_attn_up[layer],
                   w_ssm_up[layer], w_o[layer], ln_g[layer], ln_b[layer], alpha)
    return h
```

```python
import functools
import math

import jax
import jax.numpy as jnp
import numpy as np
from jax import lax
from jax.experimental import pallas as pl
from jax.experimental.pallas import tpu as pltpu

F32 = jnp.float32
BF16 = jnp.bfloat16

HEAD_DIM = 64
N_SLOTS = 8
GROUP_PATTERNS = ((128, 1), (512, 4), (2048, 16))
N_GROUPS = len(GROUP_PATTERNS)
ATTN_WIDTH = N_SLOTS * HEAD_DIM
QKV_WIDTH = N_GROUPS * ATTN_WIDTH
BLOCK = 128
SPAN = GROUP_PATTERNS[-1][1] * BLOCK
UNITS = SPAN // BLOCK
SSM_WIDTH = 512
SSM_GROUP = 16
SSM_GROUPS = SSM_WIDTH // SSM_GROUP
SSM_STATE = 64
LN_EPS = 1e-5
MASK_VALUE = -1e30
LOG2E = math.log2(math.e)
Q_SCALE = LOG2E / math.sqrt(HEAD_DIM)

LANES = 128
SUBLANES = 8
HEAD_PAIRS = ATTN_WIDTH // LANES
PHASE_SPLIT = 4
SSM_CHUNK = 8
SSM_SLABS = SSM_WIDTH // LANES
GROUPS_PER_SLAB = LANES // SSM_GROUP
SLAB_STATE = GROUPS_PER_SLAB * SSM_STATE
VMEM_LIMIT = 56 * 1024 * 1024


def _const_spec(shape):
    nd = len(shape)
    return pl.BlockSpec(shape, lambda *_: (0,) * nd, pipeline_mode=pl.Buffered(1))


def _col_block_spec(rows, width, start):
    assert start % width == 0
    return pl.BlockSpec((rows, width), lambda *_: (0, start // width), pipeline_mode=pl.Buffered(1))


def _lane_tiles(x):
    return [x[:, c * LANES:(c + 1) * LANES] for c in range(x.shape[1] // LANES)]


def _qkv_kernel(x_ref, w_ref, *refs):
    out_refs, x_stage, stage = refs[:-2], refs[-2], refs[-1]
    tm = x_ref.shape[0]
    sub = tm // PHASE_SPLIT
    x = x_ref[...]
    for c, tile in enumerate(_lane_tiles(x)):
        x_stage[c] = tile
    xb = {1: x.astype(BF16),
          PHASE_SPLIT: jnp.concatenate(
              [jnp.concatenate([x_stage[c, pl.ds(q, sub, stride=PHASE_SPLIT), :] for c in range(x_stage.shape[0])],
                               axis=1) for q in range(PHASE_SPLIT)], axis=0).astype(BF16)}
    for kind in range(3):
        for g, (_, r) in enumerate(GROUP_PATTERNS):
            col = kind * QKV_WIDTH + g * ATTN_WIDTH
            res = jnp.dot(xb[min(r, PHASE_SPLIT)], w_ref[:, col:col + ATTN_WIDTH], preferred_element_type=F32)
            if kind == 0:
                res = res * Q_SCALE
            o_ref = out_refs[kind * N_GROUPS + g]
            if r == 1:
                o_ref[0] = res.astype(BF16)
            elif r == PHASE_SPLIT:
                for q in range(r):
                    o_ref[q] = res[q * sub:(q + 1) * sub].astype(BF16)
            else:
                for c, tile in enumerate(_lane_tiles(res)):
                    stage[c] = tile
                for q in range(PHASE_SPLIT):
                    for s in range(PHASE_SPLIT):
                        o_ref[q + PHASE_SPLIT * s] = jnp.concatenate(
                            [stage[c, pl.ds(q * sub + s, sub // PHASE_SPLIT, stride=PHASE_SPLIT), :]
                             for c in range(HEAD_PAIRS)], axis=1).astype(BF16)


def _qkv_proj(x2, w_in, bsz, seq, tm=512):
    n, d = x2.shape
    tiles = seq // tm
    out_specs, out_shape = [], []
    for _ in range(3):
        for _, r in GROUP_PATTERNS:
            out_specs.append(pl.BlockSpec((None, r, tm // r, ATTN_WIDTH),
                                          lambda i: (i // tiles, 0, i % tiles, 0)))
            out_shape.append(jax.ShapeDtypeStruct((bsz, r, seq // r, ATTN_WIDTH), BF16))
    return pl.pallas_call(
        _qkv_kernel,
        grid=(n // tm,),
        in_specs=[pl.BlockSpec((tm, d), lambda i: (i, 0)), _col_block_spec(d, 3 * QKV_WIDTH, 0)],
        out_specs=out_specs,
        out_shape=out_shape,
        scratch_shapes=[pltpu.VMEM((d // LANES, tm, LANES), F32), pltpu.VMEM((HEAD_PAIRS, tm, LANES), F32)],
        compiler_params=pltpu.CompilerParams(dimension_semantics=("arbitrary",),
                                             vmem_limit_bytes=VMEM_LIMIT),
        name="qkv_proj",
    )(x2, w_in)


def _attn_bias():
    slopes = 2.0 ** (-8.0 * np.arange(1, N_SLOTS + 1) / N_SLOTS)
    qi = np.arange(BLOCK)[:, None]
    kj = np.arange(2 * BLOCK)[None, :]
    dist = BLOCK + qi - kj
    valid = (dist >= 0) & (dist <= BLOCK)
    valid = np.stack([valid, valid & (kj >= BLOCK)])
    out = []
    for _, r in GROUP_PATTERNS:
        alibi = (-LOG2E * slopes)[:, None, None, None] * (dist * r)[None, None]
        out.append(np.where(valid[None], alibi, MASK_VALUE))
    return jnp.asarray(np.stack(out).astype(np.float32))


def _attn_kernel(*refs):
    ins, bias_ref, out_ref = refs[:5 * N_GROUPS], refs[5 * N_GROUPS], refs[5 * N_GROUPS + 1]
    scr = refs[5 * N_GROUPS + 2:]
    kv_scr, (o_stage, l_stage, m_stage, ms_stage) = scr[:3 * N_GROUPS], scr[3 * N_GROUPS:]
    first_span = pl.program_id(2) == 0
    lane = lax.broadcasted_iota(jnp.int32, (BLOCK, LANES), 1)
    low = lane < HEAD_DIM

    for g, (_, r) in enumerate(GROUP_PATTERNS):
        q_ref, kc_ref, kp_ref, vc_ref, vp_ref = ins[5 * g:5 * g + 5]
        k_scr, v0_scr, v1_scr = kv_scr[3 * g:3 * g + 3]
        nb = UNITS // r
        for pp in range(r):
            k_scr[pp, :, :BLOCK] = kp_ref[pp].T
            for blk in range(nb):
                k_scr[pp, :, (blk + 1) * BLOCK:(blk + 2) * BLOCK] = kc_ref[pp, blk * BLOCK:(blk + 1) * BLOCK, :].T
        for dst, src in ((slice(0, BLOCK), vp_ref), (slice(BLOCK, None), vc_ref)):
            v = src[...]
            low_v = lax.broadcasted_iota(jnp.int32, v.shape, 2) < HEAD_DIM
            ones = jnp.ones_like(v)
            v0_scr[:, dst, :] = jnp.where(low_v, v, ones)
            v1_scr[:, dst, :] = jnp.where(low_v, ones, v)

        for u in range(UNITS):
            p, n = divmod(u, nb)
            row0 = n * BLOCK
            q = q_ref[p, row0:row0 + BLOCK, :]
            kt = k_scr[p, :, row0:row0 + 2 * BLOCK]
            variant = jnp.where(first_span, 1, 0) if n == 0 else 0
            acc, tops = [], []
            for hh in range(2):
                keep = low if hh == 0 else jnp.logical_not(low)
                qm = jnp.where(keep, q, jnp.zeros_like(q))
                s = jnp.dot(qm, kt, preferred_element_type=F32)
                s = s + bias_ref[g, hh, variant]
                m = jnp.max(s, axis=1, keepdims=True)
                e = jnp.exp2(s - m).astype(BF16)
                v_scr = (v0_scr, v1_scr)[hh]
                acc.append(jnp.dot(e, v_scr[p, row0:row0 + 2 * BLOCK, :], preferred_element_type=F32))
                tops.append(m)
            start = n * (BLOCK * r) + p
            rows = pl.ds(start, BLOCK) if r == 1 else pl.ds(start, BLOCK, stride=r)
            o_stage[g, rows, :] = jnp.where(low, acc[0], acc[1])
            l_stage[g, rows, :] = jnp.where(low, acc[1], acc[0])
            m_stage[g, rows, :] = jnp.where(low, tops[0], tops[1])
            ms_stage[g, rows, :] = jnp.where(low, tops[1], tops[0])

    def mix(i, carry):
        rows = pl.ds(pl.multiple_of(i * BLOCK, BLOCK), BLOCK)

        def weighted_sum(top_stage, val_stage):
            tops = [top_stage[g, rows, :] for g in range(N_GROUPS)]
            top = functools.reduce(jnp.maximum, tops)
            return sum(jnp.exp2(tops[g] - top) * val_stage[g, rows, :] for g in range(N_GROUPS))

        num = weighted_sum(m_stage, o_stage)
        den = pltpu.roll(weighted_sum(ms_stage, l_stage), HEAD_DIM, 1)
        out_ref[rows, :] = (num / den).astype(BF16)
        return carry

    lax.fori_loop(0, UNITS, mix, 0, unroll=4)


def _attention(qkv, bsz, seq):
    spans = seq // SPAN
    in_specs, args, scratch = [], [], []
    for g, (_, r) in enumerate(GROUP_PATTERNS):
        nb = UNITS // r
        cur = pl.BlockSpec((None, r, SPAN // r, LANES), lambda hp, b, s: (b, 0, s, hp))
        prev = pl.BlockSpec((None, r, BLOCK, LANES),
                            lambda hp, b, s, nb=nb: (b, 0, jnp.maximum(nb * s - 1, 0), hp))
        q, k, v = qkv[g], qkv[N_GROUPS + g], qkv[2 * N_GROUPS + g]
        in_specs += [cur, cur, prev, cur, prev]
        args += [q, k, k, v, v]
        scratch += [pltpu.VMEM((r, LANES, SPAN // r + BLOCK), BF16)]
        scratch += [pltpu.VMEM((r, SPAN // r + BLOCK, LANES), BF16)] * 2
    in_specs.append(pl.BlockSpec((N_GROUPS, 2, 2, BLOCK, 2 * BLOCK), lambda hp, b, s: (0, hp, 0, 0, 0)))
    scratch += [pltpu.VMEM((N_GROUPS, SPAN, LANES), F32)] * 4
    return pl.pallas_call(
        _attn_kernel,
        grid=(HEAD_PAIRS, bsz, spans),
        in_specs=in_specs,
        out_specs=pl.BlockSpec((SPAN, LANES), lambda hp, b, s: (b * spans + s, hp)),
        out_shape=jax.ShapeDtypeStruct((bsz * seq, ATTN_WIDTH), BF16),
        scratch_shapes=scratch,
        compiler_params=pltpu.CompilerParams(dimension_semantics=("arbitrary",) * 3,
                                             vmem_limit_bytes=VMEM_LIMIT),
        name="band_attn",
    )(*args, _attn_bias())


def _ssm_weights(lam_re, lam_im, log_dt, b_re, b_im, c_re, c_im):
    t = SSM_CHUNK
    lr, li = lam_re.astype(F32), lam_im.astype(F32)
    dt = jnp.exp(log_dt.astype(F32))[:, None]
    mag = jnp.exp(lr * dt)
    ab_re, ab_im = mag * jnp.cos(li * dt), mag * jnp.sin(li * dt)
    nr, ni = ab_re - 1.0, ab_im
    den = lr * lr + li * li
    coef_re = (nr * lr + ni * li) / den
    coef_im = (ni * lr - nr * li) / den
    br, bi = b_re.astype(F32), b_im.astype(F32)
    bb_re = coef_re[..., None] * br - coef_im[..., None] * bi
    bb_im = coef_re[..., None] * bi + coef_im[..., None] * br
    pw_re, pw_im = [jnp.ones_like(ab_re)], [jnp.zeros_like(ab_im)]
    for _ in range(t):
        pr, pi = pw_re[-1], pw_im[-1]
        pw_re.append(pr * ab_re - pi * ab_im)
        pw_im.append(pr * ab_im + pi * ab_re)
    pw_re, pw_im = jnp.stack(pw_re), jnp.stack(pw_im)
    cr, ci = c_re.astype(F32), c_im.astype(F32)
    n_ch = SSM_GROUPS * SSM_GROUP

    xb_re = pw_re[:t, :, :, None] * bb_re[None] - pw_im[:t, :, :, None] * bb_im[None]
    xb_im = pw_re[:t, :, :, None] * bb_im[None] + pw_im[:t, :, :, None] * bb_re[None]
    xb = jnp.stack([xb_re, xb_im]).transpose(0, 1, 2, 4, 3).reshape(2, t, n_ch, SSM_STATE)
    xb = jnp.concatenate([xb, xb], axis=-1)
    ca_re = cr[None] * pw_re[:, :, None, :] - ci[None] * pw_im[:, :, None, :]
    ca_im = cr[None] * pw_im[:, :, None, :] + ci[None] * pw_re[:, :, None, :]
    ca = jnp.stack([ca_re, -ca_im]).transpose(0, 1, 4, 2, 3).reshape(2, t + 1, SSM_STATE, n_ch)

    w_y, w_st = pl.pallas_call(
        _ssm_expand_kernel,
        grid=(SSM_SLABS,),
        in_specs=[pl.BlockSpec((2, t, LANES, LANES), lambda j: (0, 0, j, 0)),
                  pl.BlockSpec((2, t + 1, SSM_STATE, LANES), lambda j: (0, 0, 0, j))],
        out_specs=[pl.BlockSpec((None, t * LANES + 2 * SLAB_STATE, t * LANES), lambda j: (j, 0, 0)),
                   pl.BlockSpec((None, t * LANES, 2 * SLAB_STATE), lambda j: (j, 0, 0))],
        out_shape=[jax.ShapeDtypeStruct((SSM_SLABS, t * LANES + 2 * SLAB_STATE, t * LANES), BF16),
                   jax.ShapeDtypeStruct((SSM_SLABS, t * LANES, 2 * SLAB_STATE), BF16)],
        compiler_params=pltpu.CompilerParams(dimension_semantics=("arbitrary",),
                                             vmem_limit_bytes=VMEM_LIMIT),
        name="s5_expand",
    )(xb, ca)
    a_t = jnp.stack([pw_re[t].reshape(-1), pw_im[t].reshape(-1)])
    return w_y, w_st, a_t


def _ssm_expand_kernel(xb_ref, ca_ref, wy_ref, wst_ref):
    t = SSM_CHUNK
    keep_st = (lax.broadcasted_iota(jnp.int32, (LANES, SLAB_STATE), 0) // SSM_GROUP
               == lax.broadcasted_iota(jnp.int32, (LANES, SLAB_STATE), 1) // SSM_STATE)
    keep_out = (lax.broadcasted_iota(jnp.int32, (SLAB_STATE, LANES), 0) // SSM_STATE
                == lax.broadcasted_iota(jnp.int32, (SLAB_STATE, LANES), 1) // SSM_GROUP)

    def to_state(d):
        parts = [jnp.where(keep_st, jnp.concatenate([xb_ref[ri, d]] * (SLAB_STATE // LANES), axis=1), 0.0)
                 for ri in range(2)]
        return jnp.concatenate(parts, axis=1)

    def from_state(d):
        parts = [jnp.where(keep_out, jnp.concatenate([ca_ref[ri, d]] * GROUPS_PER_SLAB, axis=0), 0.0)
                 for ri in range(2)]
        return jnp.concatenate(parts, axis=0)

    c0 = from_state(0)
    zero = jnp.zeros((LANES, LANES), BF16)
    for d in range(t):
        st = to_state(d)
        sg = t - 1 - d
        wst_ref[sg * LANES:(sg + 1) * LANES, :] = st.astype(BF16)
        taps = jnp.dot(st, c0, preferred_element_type=F32, precision=lax.Precision.HIGHEST).astype(BF16)
        for sg in range(t - d):
            wy_ref[sg * LANES:(sg + 1) * LANES, (sg + d) * LANES:(sg + d + 1) * LANES] = taps
        if d:
            for tau in range(t - d):
                wy_ref[(tau + d) * LANES:(tau + d + 1) * LANES, tau * LANES:(tau + 1) * LANES] = zero
    for tau in range(t):
        wy_ref[t * LANES:, tau * LANES:(tau + 1) * LANES] = from_state(tau + 1).astype(BF16)


def _ssm_kernel(x_ref, wu_ref, wz_ref, wy_ref, wst_ref, at_ref, dskip_ref, wglu_ref, bglu_ref,
                out_ref, u_st, z_scr, s_scr, xp_scr, y_st, carry_scr):
    bsz, tl, d_model = x_ref.shape
    t = SSM_CHUNK
    ch = tl // t
    rows = bsz * tl
    crows = bsz * ch
    tiles_per_slab = SLAB_STATE // LANES
    n_tiles = SSM_SLABS * tiles_per_slab
    pitch = s_scr.shape[1] // bsz

    @pl.when(pl.program_id(0) == 0)
    def _():
        carry_scr[...] = jnp.zeros_like(carry_scr)

    xb = x_ref[...].reshape(rows, d_model).astype(BF16)
    for c, tile in enumerate(_lane_tiles(jnp.dot(xb, wu_ref[...], preferred_element_type=F32))):
        u_st[c] = tile
    z_scr[...] = jnp.dot(xb, wz_ref[...], preferred_element_type=F32)

    def chunk_lhs(j):
        return jnp.concatenate([u_st[j, pl.ds(sg, crows, stride=t), :] for sg in range(t)], axis=1).astype(BF16)

    for j in range(SSM_SLABS):
        st = jnp.dot(chunk_lhs(j), wst_ref[j], preferred_element_type=F32)
        for c, tile in enumerate(_lane_tiles(st)):
            half, cc = divmod(c, tiles_per_slab)
            for b in range(bsz):
                s_scr[half * n_tiles + j * tiles_per_slab + cc, b * pitch:b * pitch + ch, :] = tile[b * ch:(b + 1) * ch]

    def chunk_step(k, carry):
        sel = pl.ds(k, bsz, stride=pitch)
        new = []
        for c in range(n_tiles):
            c_re, c_im = carry[c], carry[n_tiles + c]
            xp_scr[c, sel, :] = c_re
            xp_scr[n_tiles + c, sel, :] = c_im
            a_re = at_ref[c:c + 1, :]
            a_im = at_ref[n_tiles + c:n_tiles + c + 1, :]
            new.append((a_re * c_re - a_im * c_im + s_scr[c, sel, :],
                        a_re * c_im + a_im * c_re + s_scr[n_tiles + c, sel, :]))
        return tuple(v[0] for v in new) + tuple(v[1] for v in new)

    carry = lax.fori_loop(0, ch, chunk_step, tuple(carry_scr[c] for c in range(2 * n_tiles)))
    for c in range(2 * n_tiles):
        carry_scr[c] = carry[c]

    for j in range(SSM_SLABS):
        xin = [jnp.concatenate([xp_scr[half + j * tiles_per_slab + c, b * pitch:b * pitch + ch, :] for b in range(bsz)],
                               axis=0)
               for half in (0, n_tiles) for c in range(tiles_per_slab)]
        lhs = jnp.concatenate([chunk_lhs(j)] + [v.astype(BF16) for v in xin], axis=1)
        yj = jnp.dot(lhs, wy_ref[j], preferred_element_type=F32)
        for tau, tile in enumerate(_lane_tiles(yj)):
            y_st[j, pl.ds(tau, crows, stride=t), :] = tile

    for b in range(bsz):
        rs = slice(b * tl, (b + 1) * tl)
        y = jnp.concatenate([y_st[c, rs, :] for c in range(SSM_SLABS)], axis=1)
        u = jnp.concatenate([u_st[c, rs, :] for c in range(SSM_SLABS)], axis=1)
        y = jax.nn.gelu(y + dskip_ref[...] * u)
        gate = jax.nn.sigmoid(jnp.dot(y.astype(BF16), wglu_ref[...], preferred_element_type=F32) + bglu_ref[...])
        out_ref[b] = (y * gate * jax.nn.silu(z_scr[rs, :])).astype(BF16)


def _ssm_branch(x, w_in, ssm_w, d_skip, w_glu, b_glu, tl=128):
    bsz, seq, d = x.shape
    u_col = 3 * QKV_WIDTH + ATTN_WIDTH
    t = SSM_CHUNK
    rows, ch = bsz * tl, tl // t
    pitch = -(-ch // SUBLANES) * SUBLANES
    pitch += SUBLANES * (1 - (pitch // SUBLANES) % 2)
    w_y, w_st, a_t = ssm_w
    n_tiles2 = 2 * SSM_SLABS * SLAB_STATE // LANES
    a_t = a_t.reshape(n_tiles2, LANES)
    out = pl.pallas_call(
        _ssm_kernel,
        grid=(seq // tl,),
        in_specs=[pl.BlockSpec((bsz, tl, d), lambda i: (0, i, 0)),
                  _col_block_spec(d, SSM_WIDTH, u_col), _col_block_spec(d, SSM_WIDTH, u_col + SSM_WIDTH),
                  _const_spec(w_y.shape),
                  _const_spec(w_st.shape), _const_spec(a_t.shape), _const_spec((1, SSM_WIDTH)),
                  _const_spec(w_glu.shape), _const_spec((1, SSM_WIDTH))],
        out_specs=pl.BlockSpec((bsz, tl, SSM_WIDTH), lambda i: (0, i, 0)),
        out_shape=jax.ShapeDtypeStruct((bsz, seq, SSM_WIDTH), BF16),
        scratch_shapes=[pltpu.VMEM((SSM_SLABS, rows, LANES), F32),
                        pltpu.VMEM((rows, SSM_WIDTH), F32),
                        pltpu.VMEM((n_tiles2, bsz * pitch, LANES), F32),
                        pltpu.VMEM((n_tiles2, bsz * pitch, LANES), F32),
                        pltpu.VMEM((SSM_SLABS, rows, LANES), F32),
                        pltpu.VMEM((n_tiles2, bsz, LANES), F32)],
        compiler_params=pltpu.CompilerParams(dimension_semantics=("arbitrary",),
                                             vmem_limit_bytes=VMEM_LIMIT),
        name="s5_branch",
    )(x, w_in, w_in, w_y, w_st, a_t, d_skip.reshape(1, SSM_WIDTH).astype(F32),
      w_glu, b_glu.reshape(1, SSM_WIDTH).astype(F32))
    return out.reshape(bsz * seq, SSM_WIDTH)


def _merge_kernel(alpha, x_ref, attn_ref, ys_ref, wz_ref, wg_ref, wa_ref, ws_ref, wo_ref, g_ref, b_ref, out_ref):
    x = x_ref[...]
    d_model = x.shape[1]
    xb = x.astype(BF16)
    z_a = jnp.dot(xb, wz_ref[...], preferred_element_type=F32)
    gates = jnp.dot(xb, wg_ref[...], preferred_element_type=F32)
    gate_a = gates[:, :d_model]
    gate_s = gates[:, d_model:]
    attn = attn_ref[...].astype(F32) * jax.nn.silu(z_a)
    y_a = jnp.dot(attn.astype(BF16), wa_ref[...], preferred_element_type=F32)
    y_s = jnp.dot(ys_ref[...], ws_ref[...], preferred_element_type=F32)
    merged = jax.nn.sigmoid(gate_a) * y_a + jax.nn.sigmoid(gate_s) * y_s
    out = jnp.dot(merged.astype(BF16), wo_ref[...], preferred_element_type=F32)
    h = alpha * x + out
    mu = jnp.mean(h, axis=-1, keepdims=True)
    hc = h - mu
    var = jnp.mean(hc * hc, axis=-1, keepdims=True)
    out_ref[...] = hc * lax.rsqrt(var + LN_EPS) * g_ref[...] + b_ref[...]


def _merge(x2, attn, ys, w_in, w_attn_up, w_ssm_up, w_o, ln_g, ln_b, alpha, tm=512):
    n, d = x2.shape
    row = lambda width: pl.BlockSpec((tm, width), lambda i: (i, 0))
    z_col = 3 * QKV_WIDTH
    gate_col = z_col + ATTN_WIDTH + 2 * SSM_WIDTH
    return pl.pallas_call(
        functools.partial(_merge_kernel, alpha),
        grid=(n // tm,),
        in_specs=[row(d), row(ATTN_WIDTH), row(SSM_WIDTH),
                  _col_block_spec(d, ATTN_WIDTH, z_col), _col_block_spec(d, 2 * d, gate_col),
                  _const_spec(w_attn_up.shape), _const_spec(w_ssm_up.shape),
                  _const_spec(w_o.shape), _const_spec((1, d)), _const_spec((1, d))],
        out_specs=row(d),
        out_shape=jax.ShapeDtypeStruct((n, d), F32),
        compiler_params=pltpu.CompilerParams(dimension_semantics=("arbitrary",),
                                             vmem_limit_bytes=VMEM_LIMIT),
        name="merge_out_ln",
    )(x2, attn, ys, w_in, w_in, w_attn_up, w_ssm_up, w_o,
      ln_g.reshape(1, d).astype(F32), ln_b.reshape(1, d).astype(F32))


def _layer(h, w_in, lam_re, lam_im, log_dt, b_re, b_im, c_re, c_im, d_skip,
           w_glu, b_glu, w_attn_up, w_ssm_up, w_o, ln_g, ln_b, alpha):
    bsz, seq, d = h.shape
    assert seq % SPAN == 0 and all(w // r == BLOCK for w, r in GROUP_PATTERNS)
    assert tuple(r for _, r in GROUP_PATTERNS) == (1, PHASE_SPLIT, PHASE_SPLIT * PHASE_SPLIT)
    x2 = h.reshape(bsz * seq, d)
    w_in = w_in.astype(BF16)

    qkv = _qkv_proj(x2, w_in, bsz, seq)
    attn = _attention(qkv, bsz, seq)
    ssm_w = _ssm_weights(lam_re, lam_im, log_dt, b_re, b_im, c_re, c_im)
    ys = _ssm_branch(h, w_in, ssm_w, d_skip, w_glu.astype(BF16), b_glu)
    out = _merge(x2, attn, ys, w_in, w_attn_up.astype(BF16), w_ssm_up.astype(BF16),
                 w_o.astype(BF16), ln_g, ln_b, alpha)
    return out.reshape(bsz, seq, d)


def kernel(x, w_in, lam_re, lam_im, log_dt, b_re, b_im, c_re, c_im, d_skip,
           w_glu, b_glu, w_attn_up, w_ssm_up, w_o, ln_g, ln_b):
    depth = w_in.shape[0]
    alpha = (2.0 * depth) ** 0.25
    h = x
    for layer in range(depth):
        h = _layer(h, w_in[layer], lam_re[layer], lam_im[layer], log_dt[layer], b_re[layer], b_im[layer],
                   c_re[layer], c_im[layer], d_skip[layer], w_glu[layer], b_glu[layer], w_attn_up[layer],
                   w_ssm_up[layer], w_o[layer], ln_g[layer], ln_b[layer], alpha)
    return h
```

```python
import functools
import math

import jax
import jax.numpy as jnp
import numpy as np
from jax import lax
from jax.experimental import pallas as pl
from jax.experimental.pallas import tpu as pltpu

F32 = jnp.float32
BF16 = jnp.bfloat16

HEAD_DIM = 64
N_SLOTS = 8
GROUP_PATTERNS = ((128, 1), (512, 4), (2048, 16))
N_GROUPS = len(GROUP_PATTERNS)
ATTN_WIDTH = N_SLOTS * HEAD_DIM
QKV_WIDTH = N_GROUPS * ATTN_WIDTH
BLOCK = 128
SPAN = GROUP_PATTERNS[-1][1] * BLOCK
UNITS = SPAN // BLOCK
SSM_WIDTH = 512
SSM_GROUP = 16
SSM_GROUPS = SSM_WIDTH // SSM_GROUP
SSM_STATE = 64
LN_EPS = 1e-5
MASK_VALUE = -1e30
LOG2E = math.log2(math.e)
Q_SCALE = LOG2E / math.sqrt(HEAD_DIM)

LANES = 128
SUBLANES = 8
HEAD_PAIRS = ATTN_WIDTH // LANES
PHASE_SPLIT = 4
QUARTER = SPAN // PHASE_SPLIT
BLOCK_ROW_TOKEN = [PHASE_SPLIT * k + ph for ph in range(PHASE_SPLIT) for k in range(BLOCK // PHASE_SPLIT)]
SSM_CHUNK = PHASE_SPLIT
SSM_SLABS = SSM_WIDTH // LANES
GROUPS_PER_SLAB = LANES // SSM_GROUP
SLAB_STATE = GROUPS_PER_SLAB * SSM_STATE
VMEM_LIMIT = 56 * 1024 * 1024


def _const_spec(shape):
    nd = len(shape)
    return pl.BlockSpec(shape, lambda *_: (0,) * nd, pipeline_mode=pl.Buffered(1))


def _col_block_spec(rows, width, start):
    assert start % width == 0
    return pl.BlockSpec((rows, width), lambda *_: (0, start // width), pipeline_mode=pl.Buffered(1))


def _lane_tiles(x):
    return [x[:, c * LANES:(c + 1) * LANES] for c in range(x.shape[1] // LANES)]


def _qkv_kernel(x_ref, w_ref, *refs):
    out_refs, x_stage, stage = refs[:-2], refs[-2], refs[-1]
    tm = x_ref.shape[0]
    sub = tm // PHASE_SPLIT
    x = x_ref[...]
    for c, tile in enumerate(_lane_tiles(x)):
        x_stage[c] = tile
    xb = {1: x.astype(BF16),
          PHASE_SPLIT: jnp.concatenate(
              [jnp.concatenate([x_stage[c, pl.ds(q, sub, stride=PHASE_SPLIT), :] for c in range(x_stage.shape[0])],
                               axis=1) for q in range(PHASE_SPLIT)], axis=0).astype(BF16)}
    piece = BLOCK // PHASE_SPLIT
    xb_block_split = jnp.concatenate(
        [jnp.concatenate([x_stage[c, pl.ds(blk * BLOCK + ph, piece, stride=PHASE_SPLIT), :]
                          for c in range(x_stage.shape[0])], axis=1)
         for blk in range(tm // BLOCK) for ph in range(PHASE_SPLIT)], axis=0).astype(BF16)
    for kind in range(3):
        for g, (_, r) in enumerate(GROUP_PATTERNS):
            col = kind * QKV_WIDTH + g * ATTN_WIDTH
            lhs = xb_block_split if (kind == 0 and r == 1) else xb[min(r, PHASE_SPLIT)]
            res = jnp.dot(lhs, w_ref[:, col:col + ATTN_WIDTH], preferred_element_type=F32)
            if kind == 0:
                res = res * Q_SCALE
            o_ref = out_refs[kind * N_GROUPS + g]
            if r == 1:
                o_ref[0] = res.astype(BF16)
            elif r == PHASE_SPLIT:
                for q in range(r):
                    o_ref[q] = res[q * sub:(q + 1) * sub].astype(BF16)
            else:
                for c, tile in enumerate(_lane_tiles(res)):
                    stage[c] = tile
                for q in range(PHASE_SPLIT):
                    for s in range(PHASE_SPLIT):
                        o_ref[q + PHASE_SPLIT * s] = jnp.concatenate(
                            [stage[c, pl.ds(q * sub + s, sub // PHASE_SPLIT, stride=PHASE_SPLIT), :]
                             for c in range(HEAD_PAIRS)], axis=1).astype(BF16)


def _qkv_proj(x2, w_in, bsz, seq, tm=512):
    n, d = x2.shape
    tiles = seq // tm
    out_specs, out_shape = [], []
    for _ in range(3):
        for _, r in GROUP_PATTERNS:
            out_specs.append(pl.BlockSpec((None, r, tm // r, ATTN_WIDTH),
                                          lambda i: (i // tiles, 0, i % tiles, 0)))
            out_shape.append(jax.ShapeDtypeStruct((bsz, r, seq // r, ATTN_WIDTH), BF16))
    return pl.pallas_call(
        _qkv_kernel,
        grid=(n // tm,),
        in_specs=[pl.BlockSpec((tm, d), lambda i: (i, 0)), _col_block_spec(d, 3 * QKV_WIDTH, 0)],
        out_specs=out_specs,
        out_shape=out_shape,
        scratch_shapes=[pltpu.VMEM((d // LANES, tm, LANES), F32), pltpu.VMEM((HEAD_PAIRS, tm, LANES), F32)],
        compiler_params=pltpu.CompilerParams(dimension_semantics=("arbitrary",),
                                             vmem_limit_bytes=VMEM_LIMIT),
        name="qkv_proj",
    )(x2, w_in)


def _attn_bias():
    slopes = 2.0 ** (-8.0 * np.arange(1, N_SLOTS + 1) / N_SLOTS)
    qi = np.arange(BLOCK)[:, None]
    kj = np.arange(2 * BLOCK)[None, :]
    dist = BLOCK + qi - kj
    valid = (dist >= 0) & (dist <= BLOCK)
    valid = np.stack([valid, valid & (kj >= BLOCK)])
    out = []
    for _, r in GROUP_PATTERNS:
        alibi = (-LOG2E * slopes)[:, None, None, None] * (dist * r)[None, None]
        bias = np.where(valid[None], alibi, MASK_VALUE)
        out.append(bias[:, :, BLOCK_ROW_TOKEN, :] if r == 1 else bias)
    return jnp.asarray(np.stack(out).astype(np.float32))


def _attn_kernel(*refs):
    ins, bias_ref, out_ref = refs[:5 * N_GROUPS], refs[5 * N_GROUPS], refs[5 * N_GROUPS + 1]
    scr = refs[5 * N_GROUPS + 2:]
    kv_scr, (o_stage, l_stage, m_stage, ms_stage, nat_stage) = scr[:3 * N_GROUPS], scr[3 * N_GROUPS:]
    first_span = pl.program_id(2) == 0
    lane = lax.broadcasted_iota(jnp.int32, (BLOCK, LANES), 1)
    low = lane < HEAD_DIM

    for g, (_, r) in enumerate(GROUP_PATTERNS):
        q_ref, kc_ref, kp_ref, vc_ref, vp_ref = ins[5 * g:5 * g + 5]
        k_scr, v0_scr, v1_scr = kv_scr[3 * g:3 * g + 3]
        nb = UNITS // r
        k_scr[:, :BLOCK, :] = kp_ref[...]
        k_scr[:, BLOCK:, :] = kc_ref[...]
        for dst, src in ((slice(0, BLOCK), vp_ref), (slice(BLOCK, None), vc_ref)):
            v = src[...]
            low_v = lax.broadcasted_iota(jnp.int32, v.shape, 2) < HEAD_DIM
            ones = jnp.ones_like(v)
            v0_scr[:, dst, :] = jnp.where(low_v, v, ones)
            v1_scr[:, dst, :] = jnp.where(low_v, ones, v)

        def unit(u, carry, g=g, r=r, nb=nb, q_ref=q_ref, k_scr=k_scr, v_scrs=(v0_scr, v1_scr)):
            p, n = u // nb, u % nb
            row0 = pl.multiple_of(n * BLOCK, BLOCK)
            q = q_ref[p, pl.ds(row0, BLOCK), :]
            kcat = k_scr[p, pl.ds(row0, 2 * BLOCK), :]
            variant = jnp.where(jnp.logical_and(first_span, n == 0), 1, 0)
            acc, tops = [], []
            for hh in range(2):
                keep = low if hh == 0 else jnp.logical_not(low)
                qm = jnp.where(keep, q, jnp.zeros_like(q))
                s = lax.dot_general(qm, kcat, (((1,), (1,)), ((), ())), preferred_element_type=F32)
                s = s + bias_ref[g, hh, variant]
                m = jnp.max(s, axis=1, keepdims=True)
                e = jnp.exp2(s - m).astype(BF16)
                acc.append(jnp.dot(e, v_scrs[hh][p, pl.ds(row0, 2 * BLOCK), :], preferred_element_type=F32))
                tops.append(m)
            staged = ((o_stage, jnp.where(low, acc[0], acc[1])),
                      (l_stage, jnp.where(low, acc[1], acc[0])),
                      (m_stage, jnp.where(low, tops[0], tops[1])),
                      (ms_stage, jnp.where(low, tops[1], tops[0])))
            for stage, val in staged:
                if r == 1:
                    piece = BLOCK // PHASE_SPLIT
                    for ph in range(PHASE_SPLIT):
                        dst = pl.multiple_of(ph * QUARTER + n * piece, piece)
                        stage[g, pl.ds(dst, piece), :] = val[ph * piece:(ph + 1) * piece]
                elif r == PHASE_SPLIT:
                    stage[g, pl.ds(pl.multiple_of(p * QUARTER + n * BLOCK, BLOCK), BLOCK), :] = val
                else:
                    start = (p % PHASE_SPLIT) * QUARTER + p // PHASE_SPLIT
                    stage[g, pl.ds(start, BLOCK, stride=PHASE_SPLIT), :] = val
            return carry

        lax.fori_loop(0, UNITS, unit, 0, unroll=UNITS)

    def mix(i, carry):
        rows = pl.ds(pl.multiple_of(i * BLOCK, BLOCK), BLOCK)

        def weighted_sum(top_stage, val_stage):
            tops = [top_stage[g, rows, :] for g in range(N_GROUPS)]
            top = functools.reduce(jnp.maximum, tops)
            return sum(jnp.exp2(tops[g] - top) * val_stage[g, rows, :] for g in range(N_GROUPS))

        num = weighted_sum(m_stage, o_stage)
        den = pltpu.roll(weighted_sum(ms_stage, l_stage), HEAD_DIM, 1)
        start = (i % PHASE_SPLIT) * (PHASE_SPLIT * BLOCK) + i // PHASE_SPLIT
        nat_stage[pl.ds(start, BLOCK, stride=PHASE_SPLIT), :] = num / den
        return carry

    lax.fori_loop(0, UNITS, mix, 0, unroll=4)
    out_ref[...] = nat_stage[...].astype(BF16)


def _attention(qkv, bsz, seq):
    spans = seq // SPAN
    in_specs, args, scratch = [], [], []
    for g, (_, r) in enumerate(GROUP_PATTERNS):
        nb = UNITS // r
        cur = pl.BlockSpec((None, r, SPAN // r, LANES), lambda hp, b, s: (b, 0, s, hp))
        prev = pl.BlockSpec((None, r, BLOCK, LANES),
                            lambda hp, b, s, nb=nb: (b, 0, jnp.maximum(nb * s - 1, 0), hp))
        q, k, v = qkv[g], qkv[N_GROUPS + g], qkv[2 * N_GROUPS + g]
        in_specs += [cur, cur, prev, cur, prev]
        args += [q, k, k, v, v]
        scratch += [pltpu.VMEM((r, SPAN // r + BLOCK, LANES), BF16)] * 3
    in_specs.append(pl.BlockSpec((N_GROUPS, 2, 2, BLOCK, 2 * BLOCK), lambda hp, b, s: (0, hp, 0, 0, 0)))
    scratch += [pltpu.VMEM((N_GROUPS, SPAN, LANES), F32)] * 4
    scratch += [pltpu.VMEM((SPAN, LANES), F32)]
    return pl.pallas_call(
        _attn_kernel,
        grid=(HEAD_PAIRS, bsz, spans),
        in_specs=in_specs,
        out_specs=pl.BlockSpec((SPAN, LANES), lambda hp, b, s: (b * spans + s, hp)),
        out_shape=jax.ShapeDtypeStruct((bsz * seq, ATTN_WIDTH), BF16),
        scratch_shapes=scratch,
        compiler_params=pltpu.CompilerParams(dimension_semantics=("arbitrary",) * 3,
                                             vmem_limit_bytes=VMEM_LIMIT),
        name="band_attn",
    )(*args, _attn_bias())


def _ssm_weights(lam_re, lam_im, log_dt, b_re, b_im, c_re, c_im):
    t = SSM_CHUNK
    lr, li = lam_re.astype(F32), lam_im.astype(F32)
    dt = jnp.exp(log_dt.astype(F32))[:, None]
    mag = jnp.exp(lr * dt)
    ab_re, ab_im = mag * jnp.cos(li * dt), mag * jnp.sin(li * dt)
    nr, ni = ab_re - 1.0, ab_im
    den = lr * lr + li * li
    coef_re = (nr * lr + ni * li) / den
    coef_im = (ni * lr - nr * li) / den
    br, bi = b_re.astype(F32), b_im.astype(F32)
    bb_re = coef_re[..., None] * br - coef_im[..., None] * bi
    bb_im = coef_re[..., None] * bi + coef_im[..., None] * br
    pw_re, pw_im = [jnp.ones_like(ab_re)], [jnp.zeros_like(ab_im)]
    for _ in range(t):
        pr, pi = pw_re[-1], pw_im[-1]
        pw_re.append(pr * ab_re - pi * ab_im)
        pw_im.append(pr * ab_im + pi * ab_re)
    pw_re, pw_im = jnp.stack(pw_re), jnp.stack(pw_im)
    cr, ci = c_re.astype(F32), c_im.astype(F32)
    n_ch = SSM_GROUPS * SSM_GROUP

    xb_re = pw_re[:t, :, :, None] * bb_re[None] - pw_im[:t, :, :, None] * bb_im[None]
    xb_im = pw_re[:t, :, :, None] * bb_im[None] + pw_im[:t, :, :, None] * bb_re[None]
    xb = jnp.stack([xb_re, xb_im]).transpose(0, 1, 2, 4, 3).reshape(2, t, n_ch, SSM_STATE)
    xb = jnp.concatenate([xb, xb], axis=-1)
    ca_re = cr[None] * pw_re[:, :, None, :] - ci[None] * pw_im[:, :, None, :]
    ca_im = cr[None] * pw_im[:, :, None, :] + ci[None] * pw_re[:, :, None, :]
    ca = jnp.stack([ca_re, -ca_im]).transpose(0, 1, 4, 2, 3).reshape(2, t + 1, SSM_STATE, n_ch)

    w_y, w_st = pl.pallas_call(
        _ssm_expand_kernel,
        grid=(SSM_SLABS,),
        in_specs=[pl.BlockSpec((2, t, LANES, LANES), lambda j: (0, 0, j, 0)),
                  pl.BlockSpec((2, t + 1, SSM_STATE, LANES), lambda j: (0, 0, 0, j))],
        out_specs=[pl.BlockSpec((None, t * LANES + 2 * SLAB_STATE, t * LANES), lambda j: (j, 0, 0)),
                   pl.BlockSpec((None, t * LANES, 2 * SLAB_STATE), lambda j: (j, 0, 0))],
        out_shape=[jax.ShapeDtypeStruct((SSM_SLABS, t * LANES + 2 * SLAB_STATE, t * LANES), BF16),
                   jax.ShapeDtypeStruct((SSM_SLABS, t * LANES, 2 * SLAB_STATE), BF16)],
        compiler_params=pltpu.CompilerParams(dimension_semantics=("arbitrary",),
                                             vmem_limit_bytes=VMEM_LIMIT),
        name="s5_expand",
    )(xb, ca)
    a_t = jnp.stack([pw_re[t].reshape(-1), pw_im[t].reshape(-1)])
    return w_y, w_st, a_t


def _ssm_expand_kernel(xb_ref, ca_ref, wy_ref, wst_ref):
    t = SSM_CHUNK
    keep_st = (lax.broadcasted_iota(jnp.int32, (LANES, SLAB_STATE), 0) // SSM_GROUP
               == lax.broadcasted_iota(jnp.int32, (LANES, SLAB_STATE), 1) // SSM_STATE)
    keep_out = (lax.broadcasted_iota(jnp.int32, (SLAB_STATE, LANES), 0) // SSM_STATE
                == lax.broadcasted_iota(jnp.int32, (SLAB_STATE, LANES), 1) // SSM_GROUP)

    def to_state(d):
        parts = [jnp.where(keep_st, jnp.concatenate([xb_ref[ri, d]] * (SLAB_STATE // LANES), axis=1), 0.0)
                 for ri in range(2)]
        return jnp.concatenate(parts, axis=1)

    def from_state(d):
        parts = [jnp.where(keep_out, jnp.concatenate([ca_ref[ri, d]] * GROUPS_PER_SLAB, axis=0), 0.0)
                 for ri in range(2)]
        return jnp.concatenate(parts, axis=0)

    c0 = from_state(0)
    zero = jnp.zeros((LANES, LANES), BF16)
    for d in range(t):
        st = to_state(d)
        sg = t - 1 - d
        wst_ref[sg * LANES:(sg + 1) * LANES, :] = st.astype(BF16)
        taps = jnp.dot(st, c0, preferred_element_type=F32, precision=lax.Precision.HIGHEST).astype(BF16)
        for sg in range(t - d):
            wy_ref[sg * LANES:(sg + 1) * LANES, (sg + d) * LANES:(sg + d + 1) * LANES] = taps
        if d:
            for tau in range(t - d):
                wy_ref[(tau + d) * LANES:(tau + d + 1) * LANES, tau * LANES:(tau + 1) * LANES] = zero
    for tau in range(t):
        wy_ref[t * LANES:, tau * LANES:(tau + 1) * LANES] = from_state(tau + 1).astype(BF16)


def _ssm_kernel(x_ref, wu_ref, wz_ref, wy_ref, wst_ref, at_ref, dskip_ref, wglu_ref, bglu_ref,
                out_ref, u_st, z_scr, s_scr, xp_scr, y_st, carry_scr):
    bsz, tl, d_model = x_ref.shape
    t = SSM_CHUNK
    ch = tl // t
    rows = bsz * tl
    crows = bsz * ch
    tiles_per_slab = SLAB_STATE // LANES
    n_tiles = SSM_SLABS * tiles_per_slab
    pitch = s_scr.shape[1] // bsz

    @pl.when(pl.program_id(0) == 0)
    def _():
        carry_scr[...] = jnp.zeros_like(carry_scr)

    xb = x_ref[...].reshape(rows, d_model).astype(BF16)
    for c, tile in enumerate(_lane_tiles(jnp.dot(xb, wu_ref[...], preferred_element_type=F32))):
        u_st[c] = tile
    z_scr[...] = jnp.dot(xb, wz_ref[...], preferred_element_type=F32)

    def chunk_lhs(j):
        return jnp.concatenate([u_st[j, pl.ds(sg, crows, stride=t), :] for sg in range(t)], axis=1).astype(BF16)

    for j in range(SSM_SLABS):
        st = jnp.dot(chunk_lhs(j), wst_ref[j], preferred_element_type=F32)
        for c, tile in enumerate(_lane_tiles(st)):
            half, cc = divmod(c, tiles_per_slab)
            for b in range(bsz):
                s_scr[half * n_tiles + j * tiles_per_slab + cc, b * pitch:b * pitch + ch, :] = tile[b * ch:(b + 1) * ch]

    def chunk_step(k, carry):
        sel = pl.ds(k, bsz, stride=pitch)
        new = []
        for c in range(n_tiles):
            c_re, c_im = carry[c], carry[n_tiles + c]
            xp_scr[c, sel, :] = c_re
            xp_scr[n_tiles + c, sel, :] = c_im
            a_re = at_ref[c:c + 1, :]
            a_im = at_ref[n_tiles + c:n_tiles + c + 1, :]
            new.append((a_re * c_re - a_im * c_im + s_scr[c, sel, :],
                        a_re * c_im + a_im * c_re + s_scr[n_tiles + c, sel, :]))
        return tuple(v[0] for v in new) + tuple(v[1] for v in new)

    carry = lax.fori_loop(0, ch, chunk_step, tuple(carry_scr[c] for c in range(2 * n_tiles)))
    for c in range(2 * n_tiles):
        carry_scr[c] = carry[c]

    for j in range(SSM_SLABS):
        xin = [jnp.concatenate([xp_scr[half + j * tiles_per_slab + c, b * pitch:b * pitch + ch, :] for b in range(bsz)],
                               axis=0)
               for half in (0, n_tiles) for c in range(tiles_per_slab)]
        lhs = jnp.concatenate([chunk_lhs(j)] + [v.astype(BF16) for v in xin], axis=1)
        yj = jnp.dot(lhs, wy_ref[j], preferred_element_type=F32)
        for tau, tile in enumerate(_lane_tiles(yj)):
            y_st[j, pl.ds(tau, crows, stride=t), :] = tile

    for b in range(bsz):
        rs = slice(b * tl, (b + 1) * tl)
        y = jnp.concatenate([y_st[c, rs, :] for c in range(SSM_SLABS)], axis=1)
        u = jnp.concatenate([u_st[c, rs, :] for c in range(SSM_SLABS)], axis=1)
        y = jax.nn.gelu(y + dskip_ref[...] * u)
        gate = jax.nn.sigmoid(jnp.dot(y.astype(BF16), wglu_ref[...], preferred_element_type=F32) + bglu_ref[...])
        out_ref[b] = (y * gate * jax.nn.silu(z_scr[rs, :])).astype(BF16)


def _ssm_branch(x, w_in, ssm_w, d_skip, w_glu, b_glu, tl=128):
    bsz, seq, d = x.shape
    u_col = 3 * QKV_WIDTH + ATTN_WIDTH
    t = SSM_CHUNK
    rows, ch = bsz * tl, tl // t
    pitch = -(-ch // SUBLANES) * SUBLANES
    pitch += SUBLANES * (1 - (pitch // SUBLANES) % 2)
    w_y, w_st, a_t = ssm_w
    n_tiles2 = 2 * SSM_SLABS * SLAB_STATE // LANES
    a_t = a_t.reshape(n_tiles2, LANES)
    out = pl.pallas_call(
        _ssm_kernel,
        grid=(seq // tl,),
        in_specs=[pl.BlockSpec((bsz, tl, d), lambda i: (0, i, 0)),
                  _col_block_spec(d, SSM_WIDTH, u_col), _col_block_spec(d, SSM_WIDTH, u_col + SSM_WIDTH),
                  _const_spec(w_y.shape),
                  _const_spec(w_st.shape), _const_spec(a_t.shape), _const_spec((1, SSM_WIDTH)),
                  _const_spec(w_glu.shape), _const_spec((1, SSM_WIDTH))],
        out_specs=pl.BlockSpec((bsz, tl, SSM_WIDTH), lambda i: (0, i, 0)),
        out_shape=jax.ShapeDtypeStruct((bsz, seq, SSM_WIDTH), BF16),
        scratch_shapes=[pltpu.VMEM((SSM_SLABS, rows, LANES), F32),
                        pltpu.VMEM((rows, SSM_WIDTH), F32),
                        pltpu.VMEM((n_tiles2, bsz * pitch, LANES), F32),
                        pltpu.VMEM((n_tiles2, bsz * pitch, LANES), F32),
                        pltpu.VMEM((SSM_SLABS, rows, LANES), F32),
                        pltpu.VMEM((n_tiles2, bsz, LANES), F32)],
        compiler_params=pltpu.CompilerParams(dimension_semantics=("arbitrary",),
                                             vmem_limit_bytes=VMEM_LIMIT),
        name="s5_branch",
    )(x, w_in, w_in, w_y, w_st, a_t, d_skip.reshape(1, SSM_WIDTH).astype(F32),
      w_glu, b_glu.reshape(1, SSM_WIDTH).astype(F32))
    return out.reshape(bsz * seq, SSM_WIDTH)


def _merge_kernel(alpha, x_ref, attn_ref, ys_ref, wz_ref, wg_ref, wa_ref, ws_ref, wo_ref, g_ref, b_ref, out_ref):
    x = x_ref[...]
    d_model = x.shape[1]
    xb = x.astype(BF16)
    z_a = jnp.dot(xb, wz_ref[...], preferred_element_type=F32)
    gates = jnp.dot(xb, wg_ref[...], preferred_element_type=F32)
    gate_a = gates[:, :d_model]
    gate_s = gates[:, d_model:]
    attn = attn_ref[...].astype(F32) * jax.nn.silu(z_a)
    y_a = jnp.dot(attn.astype(BF16), wa_ref[...], preferred_element_type=F32)
    y_s = jnp.dot(ys_ref[...], ws_ref[...], preferred_element_type=F32)
    merged = jax.nn.sigmoid(gate_a) * y_a + jax.nn.sigmoid(gate_s) * y_s
    out = jnp.dot(merged.astype(BF16), wo_ref[...], preferred_element_type=F32)
    h = alpha * x + out
    mu = jnp.mean(h, axis=-1, keepdims=True)
    hc = h - mu
    var = jnp.mean(hc * hc, axis=-1, keepdims=True)
    out_ref[...] = hc * lax.rsqrt(var + LN_EPS) * g_ref[...] + b_ref[...]


def _merge(x2, attn, ys, w_in, w_attn_up, w_ssm_up, w_o, ln_g, ln_b, alpha, tm=512):
    n, d = x2.shape
    row = lambda width: pl.BlockSpec((tm, width), lambda i: (i, 0))
    z_col = 3 * QKV_WIDTH
    gate_col = z_col + ATTN_WIDTH + 2 * SSM_WIDTH
    return pl.pallas_call(
        functools.partial(_merge_kernel, alpha),
        grid=(n // tm,),
        in_specs=[row(d), row(ATTN_WIDTH), row(SSM_WIDTH),
                  _col_block_spec(d, ATTN_WIDTH, z_col), _col_block_spec(d, 2 * d, gate_col),
                  _const_spec(w_attn_up.shape), _const_spec(w_ssm_up.shape),
                  _const_spec(w_o.shape), _const_spec((1, d)), _const_spec((1, d))],
        out_specs=row(d),
        out_shape=jax.ShapeDtypeStruct((n, d), F32),
        compiler_params=pltpu.CompilerParams(dimension_semantics=("arbitrary",),
                                             vmem_limit_bytes=VMEM_LIMIT),
        name="merge_out_ln",
    )(x2, attn, ys, w_in, w_in, w_attn_up, w_ssm_up, w_o,
      ln_g.reshape(1, d).astype(F32), ln_b.reshape(1, d).astype(F32))


def _layer(h, w_in, lam_re, lam_im, log_dt, b_re, b_im, c_re, c_im, d_skip,
           w_glu, b_glu, w_attn_up, w_ssm_up, w_o, ln_g, ln_b, alpha):
    bsz, seq, d = h.shape
    assert seq % SPAN == 0 and all(w // r == BLOCK for w, r in GROUP_PATTERNS)
    assert tuple(r for _, r in GROUP_PATTERNS) == (1, PHASE_SPLIT, PHASE_SPLIT * PHASE_SPLIT)
    x2 = h.reshape(bsz * seq, d)
    w_in = w_in.astype(BF16)

    qkv = _qkv_proj(x2, w_in, bsz, seq)
    attn = _attention(qkv, bsz, seq)
    ssm_w = _ssm_weights(lam_re, lam_im, log_dt, b_re, b_im, c_re, c_im)
    ys = _ssm_branch(h, w_in, ssm_w, d_skip, w_glu.astype(BF16), b_glu)
    out = _merge(x2, attn, ys, w_in, w_attn_up.astype(BF16), w_ssm_up.astype(BF16),
                 w_o.astype(BF16), ln_g, ln_b, alpha)
    return out.reshape(bsz, seq, d)


def kernel(x, w_in, lam_re, lam_im, log_dt, b_re, b_im, c_re, c_im, d_skip,
           w_glu, b_glu, w_attn_up, w_ssm_up, w_o, ln_g, ln_b):
    depth = w_in.shape[0]
    alpha = (2.0 * depth) ** 0.25
    h = x
    for layer in range(depth):
        h = _layer(h, w_in[layer], lam_re[layer], lam_im[layer], log_dt[layer], b_re[layer], b_im[layer],
                   c_re[layer], c_im[layer], d_skip[layer], w_glu[layer], b_glu[layer], w_attn_up[layer],
                   w_ssm_up[layer], w_o[layer], ln_g[layer], ln_b[layer], alpha)
    return h
```

```python
import functools
import math

import jax
import jax.numpy as jnp
import numpy as np
from jax import lax
from jax.experimental import pallas as pl
from jax.experimental.pallas import tpu as pltpu

F32 = jnp.float32
BF16 = jnp.bfloat16

HEAD_DIM = 64
N_SLOTS = 8
GROUP_PATTERNS = ((128, 1), (512, 4), (2048, 16))
N_GROUPS = len(GROUP_PATTERNS)
ATTN_WIDTH = N_SLOTS * HEAD_DIM
QKV_WIDTH = N_GROUPS * ATTN_WIDTH
BLOCK = 128
SPAN = GROUP_PATTERNS[-1][1] * BLOCK
UNITS = SPAN // BLOCK
SSM_WIDTH = 512
SSM_GROUP = 16
SSM_GROUPS = SSM_WIDTH // SSM_GROUP
SSM_STATE = 64
LN_EPS = 1e-5
MASK_VALUE = -1e30
LOG2E = math.log2(math.e)
Q_SCALE = LOG2E / math.sqrt(HEAD_DIM)

LANES = 128
SUBLANES = 8
HEAD_PAIRS = ATTN_WIDTH // LANES
PHASE_SPLIT = 4
QUARTER = SPAN // PHASE_SPLIT
BLOCK_ROW_TOKEN = [PHASE_SPLIT * k + ph for ph in range(PHASE_SPLIT) for k in range(BLOCK // PHASE_SPLIT)]
SSM_CHUNK = PHASE_SPLIT
SSM_SLABS = SSM_WIDTH // LANES
GROUPS_PER_SLAB = LANES // SSM_GROUP
SLAB_STATE = GROUPS_PER_SLAB * SSM_STATE
VMEM_LIMIT = 56 * 1024 * 1024


def _const_spec(shape):
    nd = len(shape)
    return pl.BlockSpec(shape, lambda *_: (0,) * nd, pipeline_mode=pl.Buffered(1))


def _col_block_spec(rows, width, start):
    assert start % width == 0
    return pl.BlockSpec((rows, width), lambda *_: (0, start // width), pipeline_mode=pl.Buffered(1))


def _sigmoid(x):
    return 0.5 * jnp.tanh(0.5 * x) + 0.5


def _silu(x):
    return x * _sigmoid(x)


def _lane_tiles(x):
    return [x[:, c * LANES:(c + 1) * LANES] for c in range(x.shape[1] // LANES)]


def _qkv_kernel(x_ref, w_ref, *refs):
    out_refs, x_stage, stage = refs[:-2], refs[-2], refs[-1]
    tm = x_ref.shape[0]
    sub = tm // PHASE_SPLIT
    x = x_ref[...]
    for c, tile in enumerate(_lane_tiles(x)):
        x_stage[c] = tile
    xb = {1: x.astype(BF16),
          PHASE_SPLIT: jnp.concatenate(
              [jnp.concatenate([x_stage[c, pl.ds(q, sub, stride=PHASE_SPLIT), :] for c in range(x_stage.shape[0])],
                               axis=1) for q in range(PHASE_SPLIT)], axis=0).astype(BF16)}
    piece = BLOCK // PHASE_SPLIT
    xb_block_split = jnp.concatenate(
        [jnp.concatenate([x_stage[c, pl.ds(blk * BLOCK + ph, piece, stride=PHASE_SPLIT), :]
                          for c in range(x_stage.shape[0])], axis=1)
         for blk in range(tm // BLOCK) for ph in range(PHASE_SPLIT)], axis=0).astype(BF16)
    for kind in range(3):
        for g, (_, r) in enumerate(GROUP_PATTERNS):
            col = kind * QKV_WIDTH + g * ATTN_WIDTH
            lhs = xb_block_split if (kind == 0 and r == 1) else xb[min(r, PHASE_SPLIT)]
            res = jnp.dot(lhs, w_ref[:, col:col + ATTN_WIDTH], preferred_element_type=F32)
            if kind == 0:
                res = res * Q_SCALE
            o_ref = out_refs[kind * N_GROUPS + g]
            if r == 1:
                o_ref[0] = res.astype(BF16)
            elif r == PHASE_SPLIT:
                for q in range(r):
                    o_ref[q] = res[q * sub:(q + 1) * sub].astype(BF16)
            else:
                for c, tile in enumerate(_lane_tiles(res)):
                    stage[c] = tile
                for q in range(PHASE_SPLIT):
                    for s in range(PHASE_SPLIT):
                        o_ref[q + PHASE_SPLIT * s] = jnp.concatenate(
                            [stage[c, pl.ds(q * sub + s, sub // PHASE_SPLIT, stride=PHASE_SPLIT), :]
                             for c in range(HEAD_PAIRS)], axis=1).astype(BF16)


def _qkv_proj(x2, w_in, bsz, seq, tm=512):
    n, d = x2.shape
    tiles = seq // tm
    out_specs, out_shape = [], []
    for _ in range(3):
        for _, r in GROUP_PATTERNS:
            out_specs.append(pl.BlockSpec((None, r, tm // r, ATTN_WIDTH),
                                          lambda i: (i // tiles, 0, i % tiles, 0)))
            out_shape.append(jax.ShapeDtypeStruct((bsz, r, seq // r, ATTN_WIDTH), BF16))
    return pl.pallas_call(
        _qkv_kernel,
        grid=(n // tm,),
        in_specs=[pl.BlockSpec((tm, d), lambda i: (i, 0)), _col_block_spec(d, 3 * QKV_WIDTH, 0)],
        out_specs=out_specs,
        out_shape=out_shape,
        scratch_shapes=[pltpu.VMEM((d // LANES, tm, LANES), F32), pltpu.VMEM((HEAD_PAIRS, tm, LANES), F32)],
        compiler_params=pltpu.CompilerParams(dimension_semantics=("arbitrary",),
                                             vmem_limit_bytes=VMEM_LIMIT),
        name="qkv_proj",
    )(x2, w_in)


def _attn_bias():
    slopes = 2.0 ** (-8.0 * np.arange(1, N_SLOTS + 1) / N_SLOTS)
    qi = np.arange(BLOCK)[:, None]
    kj = np.arange(2 * BLOCK)[None, :]
    dist = BLOCK + qi - kj
    valid = (dist >= 0) & (dist <= BLOCK)
    valid = np.stack([valid, valid & (kj >= BLOCK)])
    out = []
    for _, r in GROUP_PATTERNS:
        alibi = (-LOG2E * slopes)[:, None, None, None] * (dist * r)[None, None]
        bias = np.where(valid[None], alibi, MASK_VALUE)
        out.append(bias[:, :, BLOCK_ROW_TOKEN, :] if r == 1 else bias)
    return jnp.asarray(np.stack(out).astype(np.float32))


def _attn_kernel(*refs):
    ins, bias_ref, out_ref = refs[:5 * N_GROUPS], refs[5 * N_GROUPS], refs[5 * N_GROUPS + 1]
    scr = refs[5 * N_GROUPS + 2:]
    kv_scr, (o_stage, l_stage, m_stage, ms_stage, nat_stage) = scr[:3 * N_GROUPS], scr[3 * N_GROUPS:]
    first_span = pl.program_id(2) == 0
    lane = lax.broadcasted_iota(jnp.int32, (BLOCK, LANES), 1)
    low = lane < HEAD_DIM

    for g, (_, r) in enumerate(GROUP_PATTERNS):
        q_ref, kc_ref, kp_ref, vc_ref, vp_ref = ins[5 * g:5 * g + 5]
        k_scr, v0_scr, v1_scr = kv_scr[3 * g:3 * g + 3]
        nb = UNITS // r
        k_scr[:, :BLOCK, :] = kp_ref[...]
        k_scr[:, BLOCK:, :] = kc_ref[...]
        for dst, src in ((slice(0, BLOCK), vp_ref), (slice(BLOCK, None), vc_ref)):
            v = src[...]
            low_v = lax.broadcasted_iota(jnp.int32, v.shape, 2) < HEAD_DIM
            ones = jnp.ones_like(v)
            v0_scr[:, dst, :] = jnp.where(low_v, v, ones)
            v1_scr[:, dst, :] = jnp.where(low_v, ones, v)

        def unit(u, carry, g=g, r=r, nb=nb, q_ref=q_ref, k_scr=k_scr, v_scrs=(v0_scr, v1_scr)):
            p, n = u // nb, u % nb
            row0 = pl.multiple_of(n * BLOCK, BLOCK)
            q = q_ref[p, pl.ds(row0, BLOCK), :]
            kcat = k_scr[p, pl.ds(row0, 2 * BLOCK), :]
            variant = jnp.where(jnp.logical_and(first_span, n == 0), 1, 0)
            acc, tops = [], []
            for hh in range(2):
                keep = low if hh == 0 else jnp.logical_not(low)
                qm = jnp.where(keep, q, jnp.zeros_like(q))
                s = lax.dot_general(qm, kcat, (((1,), (1,)), ((), ())), preferred_element_type=F32)
                s = s + bias_ref[g, hh, variant]
                m = jnp.max(s, axis=1, keepdims=True)
                e = jnp.exp2(s - m).astype(BF16)
                acc.append(jnp.dot(e, v_scrs[hh][p, pl.ds(row0, 2 * BLOCK), :], preferred_element_type=F32))
                tops.append(m)
            staged = ((o_stage, jnp.where(low, acc[0], acc[1])),
                      (l_stage, jnp.where(low, acc[1], acc[0])),
                      (m_stage, jnp.where(low, tops[0], tops[1])),
                      (ms_stage, jnp.where(low, tops[1], tops[0])))
            for stage, val in staged:
                if r == 1:
                    piece = BLOCK // PHASE_SPLIT
                    for ph in range(PHASE_SPLIT):
                        dst = pl.multiple_of(ph * QUARTER + n * piece, piece)
                        stage[g, pl.ds(dst, piece), :] = val[ph * piece:(ph + 1) * piece]
                elif r == PHASE_SPLIT:
                    stage[g, pl.ds(pl.multiple_of(p * QUARTER + n * BLOCK, BLOCK), BLOCK), :] = val
                else:
                    start = (p % PHASE_SPLIT) * QUARTER + p // PHASE_SPLIT
                    stage[g, pl.ds(start, BLOCK, stride=PHASE_SPLIT), :] = val
            return carry

        lax.fori_loop(0, UNITS, unit, 0, unroll=UNITS)

    def mix(i, carry):
        rows = pl.ds(pl.multiple_of(i * BLOCK, BLOCK), BLOCK)

        def weighted_sum(top_stage, val_stage):
            tops = [top_stage[g, rows, :] for g in range(N_GROUPS)]
            top = functools.reduce(jnp.maximum, tops)
            return sum(jnp.exp2(tops[g] - top) * val_stage[g, rows, :] for g in range(N_GROUPS))

        num = weighted_sum(m_stage, o_stage)
        den = pltpu.roll(weighted_sum(ms_stage, l_stage), HEAD_DIM, 1)
        start = (i % PHASE_SPLIT) * (PHASE_SPLIT * BLOCK) + i // PHASE_SPLIT
        nat_stage[pl.ds(start, BLOCK, stride=PHASE_SPLIT), :] = num / den
        return carry

    lax.fori_loop(0, UNITS, mix, 0, unroll=4)
    out_ref[...] = nat_stage[...].astype(BF16)


def _attention(qkv, bsz, seq):
    spans = seq // SPAN
    in_specs, args, scratch = [], [], []
    for g, (_, r) in enumerate(GROUP_PATTERNS):
        nb = UNITS // r
        cur = pl.BlockSpec((None, r, SPAN // r, LANES), lambda hp, b, s: (b, 0, s, hp))
        prev = pl.BlockSpec((None, r, BLOCK, LANES),
                            lambda hp, b, s, nb=nb: (b, 0, jnp.maximum(nb * s - 1, 0), hp))
        q, k, v = qkv[g], qkv[N_GROUPS + g], qkv[2 * N_GROUPS + g]
        in_specs += [cur, cur, prev, cur, prev]
        args += [q, k, k, v, v]
        scratch += [pltpu.VMEM((r, SPAN // r + BLOCK, LANES), BF16)] * 3
    in_specs.append(pl.BlockSpec((N_GROUPS, 2, 2, BLOCK, 2 * BLOCK), lambda hp, b, s: (0, hp, 0, 0, 0)))
    scratch += [pltpu.VMEM((N_GROUPS, SPAN, LANES), F32)] * 4
    scratch += [pltpu.VMEM((SPAN, LANES), F32)]
    return pl.pallas_call(
        _attn_kernel,
        grid=(HEAD_PAIRS, bsz, spans),
        in_specs=in_specs,
        out_specs=pl.BlockSpec((SPAN, LANES), lambda hp, b, s: (b * spans + s, hp)),
        out_shape=jax.ShapeDtypeStruct((bsz * seq, ATTN_WIDTH), BF16),
        scratch_shapes=scratch,
        compiler_params=pltpu.CompilerParams(dimension_semantics=("arbitrary",) * 3,
                                             vmem_limit_bytes=VMEM_LIMIT),
        name="band_attn",
    )(*args, _attn_bias())


def _ssm_weights(lam_re, lam_im, log_dt, b_re, b_im, c_re, c_im):
    t = SSM_CHUNK
    lr, li = lam_re.astype(F32), lam_im.astype(F32)
    dt = jnp.exp(log_dt.astype(F32))[:, None]
    mag = jnp.exp(lr * dt)
    ab_re, ab_im = mag * jnp.cos(li * dt), mag * jnp.sin(li * dt)
    nr, ni = ab_re - 1.0, ab_im
    den = lr * lr + li * li
    coef_re = (nr * lr + ni * li) / den
    coef_im = (ni * lr - nr * li) / den
    br, bi = b_re.astype(F32), b_im.astype(F32)
    bb_re = coef_re[..., None] * br - coef_im[..., None] * bi
    bb_im = coef_re[..., None] * bi + coef_im[..., None] * br
    pw_re, pw_im = [jnp.ones_like(ab_re)], [jnp.zeros_like(ab_im)]
    for _ in range(t):
        pr, pi = pw_re[-1], pw_im[-1]
        pw_re.append(pr * ab_re - pi * ab_im)
        pw_im.append(pr * ab_im + pi * ab_re)
    pw_re, pw_im = jnp.stack(pw_re), jnp.stack(pw_im)
    cr, ci = c_re.astype(F32), c_im.astype(F32)
    n_ch = SSM_GROUPS * SSM_GROUP

    xb_re = pw_re[:t, :, :, None] * bb_re[None] - pw_im[:t, :, :, None] * bb_im[None]
    xb_im = pw_re[:t, :, :, None] * bb_im[None] + pw_im[:t, :, :, None] * bb_re[None]
    xb = jnp.stack([xb_re, xb_im]).transpose(0, 1, 2, 4, 3).reshape(2, t, n_ch, SSM_STATE)
    xb = jnp.concatenate([xb, xb], axis=-1)
    ca_re = cr[None] * pw_re[:, :, None, :] - ci[None] * pw_im[:, :, None, :]
    ca_im = cr[None] * pw_im[:, :, None, :] + ci[None] * pw_re[:, :, None, :]
    ca = jnp.stack([ca_re, -ca_im]).transpose(0, 1, 4, 2, 3).reshape(2, t + 1, SSM_STATE, n_ch)

    w_y, w_st = pl.pallas_call(
        _ssm_expand_kernel,
        grid=(SSM_SLABS,),
        in_specs=[pl.BlockSpec((2, t, LANES, LANES), lambda j: (0, 0, j, 0)),
                  pl.BlockSpec((2, t + 1, SSM_STATE, LANES), lambda j: (0, 0, 0, j))],
        out_specs=[pl.BlockSpec((None, t * LANES + 2 * SLAB_STATE, t * LANES), lambda j: (j, 0, 0)),
                   pl.BlockSpec((None, t * LANES, 2 * SLAB_STATE), lambda j: (j, 0, 0))],
        out_shape=[jax.ShapeDtypeStruct((SSM_SLABS, t * LANES + 2 * SLAB_STATE, t * LANES), BF16),
                   jax.ShapeDtypeStruct((SSM_SLABS, t * LANES, 2 * SLAB_STATE), BF16)],
        compiler_params=pltpu.CompilerParams(dimension_semantics=("arbitrary",),
                                             vmem_limit_bytes=VMEM_LIMIT),
        name="s5_expand",
    )(xb, ca)
    a_t = jnp.stack([pw_re[t].reshape(-1), pw_im[t].reshape(-1)])
    return w_y, w_st, a_t


def _ssm_expand_kernel(xb_ref, ca_ref, wy_ref, wst_ref):
    t = SSM_CHUNK
    keep_st = (lax.broadcasted_iota(jnp.int32, (LANES, SLAB_STATE), 0) // SSM_GROUP
               == lax.broadcasted_iota(jnp.int32, (LANES, SLAB_STATE), 1) // SSM_STATE)
    keep_out = (lax.broadcasted_iota(jnp.int32, (SLAB_STATE, LANES), 0) // SSM_STATE
                == lax.broadcasted_iota(jnp.int32, (SLAB_STATE, LANES), 1) // SSM_GROUP)

    def to_state(d):
        parts = [jnp.where(keep_st, jnp.concatenate([xb_ref[ri, d]] * (SLAB_STATE // LANES), axis=1), 0.0)
                 for ri in range(2)]
        return jnp.concatenate(parts, axis=1)

    def from_state(d):
        parts = [jnp.where(keep_out, jnp.concatenate([ca_ref[ri, d]] * GROUPS_PER_SLAB, axis=0), 0.0)
                 for ri in range(2)]
        return jnp.concatenate(parts, axis=0)

    c0 = from_state(0)
    zero = jnp.zeros((LANES, LANES), BF16)
    for d in range(t):
        st = to_state(d)
        sg = t - 1 - d
        wst_ref[sg * LANES:(sg + 1) * LANES, :] = st.astype(BF16)
        taps = jnp.dot(st, c0, preferred_element_type=F32, precision=lax.Precision.HIGHEST).astype(BF16)
        for sg in range(t - d):
            wy_ref[sg * LANES:(sg + 1) * LANES, (sg + d) * LANES:(sg + d + 1) * LANES] = taps
        if d:
            for tau in range(t - d):
                wy_ref[(tau + d) * LANES:(tau + d + 1) * LANES, tau * LANES:(tau + 1) * LANES] = zero
    for tau in range(t):
        wy_ref[t * LANES:, tau * LANES:(tau + 1) * LANES] = from_state(tau + 1).astype(BF16)


def _ssm_kernel(x_ref, wu_ref, wz_ref, wy_ref, wst_ref, at_ref, dskip_ref, wglu_ref, bglu_ref,
                out_ref, u_st, z_scr, y_st, carry_scr, *state_scrs):
    s_scrs, xp_scrs = state_scrs[:SSM_SLABS], state_scrs[SSM_SLABS:]
    bsz, tl, d_model = x_ref.shape
    t = SSM_CHUNK
    ch = tl // t
    rows = bsz * tl
    crows = bsz * ch
    tiles_per_slab = SLAB_STATE // LANES
    n_tiles = SSM_SLABS * tiles_per_slab
    pitch = s_scrs[0].shape[1] // bsz

    @pl.when(pl.program_id(0) == 0)
    def _():
        carry_scr[...] = jnp.zeros_like(carry_scr)

    xb = x_ref[...].reshape(rows, d_model).astype(BF16)
    for c, tile in enumerate(_lane_tiles(jnp.dot(xb, wu_ref[...], preferred_element_type=F32))):
        u_st[c] = tile
    z_scr[...] = jnp.dot(xb, wz_ref[...], preferred_element_type=F32)

    def chunk_lhs(j):
        return jnp.concatenate([u_st[j, pl.ds(sg, crows, stride=t), :] for sg in range(t)], axis=1).astype(BF16)

    n_slab_tiles = 2 * tiles_per_slab

    def state_increments(j):
        st = jnp.dot(chunk_lhs(j), wst_ref[j], preferred_element_type=F32)
        for c, tile in enumerate(_lane_tiles(st)):
            for b in range(bsz):
                s_scrs[j][c, b * pitch:b * pitch + ch, :] = tile[b * ch:(b + 1) * ch]

    def recurrence(j):
        s_scr, xp_scr = s_scrs[j], xp_scrs[j]
        for c_r in range(tiles_per_slab):
            c_i = tiles_per_slab + c_r
            g_r, g_i = j * tiles_per_slab + c_r, n_tiles + j * tiles_per_slab + c_r
            a_re, a_im = at_ref[g_r:g_r + 1, :], at_ref[g_i:g_i + 1, :]
            x_re, x_im = carry_scr[g_r], carry_scr[g_i]
            for k in range(ch):
                sel = pl.ds(k, bsz, stride=pitch)
                xp_scr[c_r, sel, :] = x_re
                xp_scr[c_i, sel, :] = x_im
                x_re, x_im = (a_re * x_re - a_im * x_im + s_scr[c_r, sel, :],
                              a_re * x_im + a_im * x_re + s_scr[c_i, sel, :])
            carry_scr[g_r] = x_re
            carry_scr[g_i] = x_im

    def response(j):
        xin = [jnp.concatenate([xp_scrs[j][c, b * pitch:b * pitch + ch, :] for b in range(bsz)],
                               axis=0).astype(BF16) for c in range(n_slab_tiles)]
        yj = jnp.dot(jnp.concatenate([chunk_lhs(j)] + xin, axis=1), wy_ref[j], preferred_element_type=F32)
        for tau, tile in enumerate(_lane_tiles(yj)):
            y_st[j, pl.ds(tau, crows, stride=t), :] = tile

    state_increments(0)
    for j in range(SSM_SLABS):
        if j + 1 < SSM_SLABS:
            state_increments(j + 1)
        recurrence(j)
        if j:
            response(j - 1)
    response(SSM_SLABS - 1)

    for b in range(bsz):
        rs = slice(b * tl, (b + 1) * tl)
        y = jnp.concatenate([y_st[c, rs, :] for c in range(SSM_SLABS)], axis=1)
        u = jnp.concatenate([u_st[c, rs, :] for c in range(SSM_SLABS)], axis=1)
        y = jax.nn.gelu(y + dskip_ref[...] * u)
        gate = _sigmoid(jnp.dot(y.astype(BF16), wglu_ref[...], preferred_element_type=F32) + bglu_ref[...])
        out_ref[b] = (y * gate * _silu(z_scr[rs, :])).astype(BF16)


def _ssm_branch(x, w_in, ssm_w, d_skip, w_glu, b_glu, tl=128):
    bsz, seq, d = x.shape
    u_col = 3 * QKV_WIDTH + ATTN_WIDTH
    t = SSM_CHUNK
    rows, ch = bsz * tl, tl // t
    pitch = -(-ch // SUBLANES) * SUBLANES
    pitch += SUBLANES * (1 - (pitch // SUBLANES) % 2)
    w_y, w_st, a_t = ssm_w
    n_tiles2 = 2 * SSM_SLABS * SLAB_STATE // LANES
    a_t = a_t.reshape(n_tiles2, LANES)
    out = pl.pallas_call(
        _ssm_kernel,
        grid=(seq // tl,),
        in_specs=[pl.BlockSpec((bsz, tl, d), lambda i: (0, i, 0)),
                  _col_block_spec(d, SSM_WIDTH, u_col), _col_block_spec(d, SSM_WIDTH, u_col + SSM_WIDTH),
                  _const_spec(w_y.shape),
                  _const_spec(w_st.shape), _const_spec(a_t.shape), _const_spec((1, SSM_WIDTH)),
                  _const_spec(w_glu.shape), _const_spec((1, SSM_WIDTH))],
        out_specs=pl.BlockSpec((bsz, tl, SSM_WIDTH), lambda i: (0, i, 0)),
        out_shape=jax.ShapeDtypeStruct((bsz, seq, SSM_WIDTH), BF16),
        scratch_shapes=[pltpu.VMEM((SSM_SLABS, rows, LANES), F32),
                        pltpu.VMEM((rows, SSM_WIDTH), F32),
                        pltpu.VMEM((SSM_SLABS, rows, LANES), F32),
                        pltpu.VMEM((n_tiles2, bsz, LANES), F32)]
        + [pltpu.VMEM((n_tiles2 // SSM_SLABS, bsz * pitch, LANES), F32)] * (2 * SSM_SLABS),
        compiler_params=pltpu.CompilerParams(dimension_semantics=("arbitrary",),
                                             vmem_limit_bytes=VMEM_LIMIT),
        name="s5_branch",
    )(x, w_in, w_in, w_y, w_st, a_t, d_skip.reshape(1, SSM_WIDTH).astype(F32),
      w_glu, b_glu.reshape(1, SSM_WIDTH).astype(F32))
    return out.reshape(bsz * seq, SSM_WIDTH)


def _merge_kernel(alpha, x_ref, attn_ref, ys_ref, wz_ref, wg_ref, wa_ref, ws_ref, wo_ref, g_ref, b_ref, out_ref):
    x = x_ref[...]
    d_model = x.shape[1]
    xb = x.astype(BF16)
    z_a = jnp.dot(xb, wz_ref[...], preferred_element_type=F32)
    gates = jnp.dot(xb, wg_ref[...], preferred_element_type=F32)
    gate_a = gates[:, :d_model]
    gate_s = gates[:, d_model:]
    attn = attn_ref[...].astype(F32) * _silu(z_a)
    y_a = jnp.dot(attn.astype(BF16), wa_ref[...], preferred_element_type=F32)
    y_s = jnp.dot(ys_ref[...], ws_ref[...], preferred_element_type=F32)
    merged = _sigmoid(gate_a) * y_a + _sigmoid(gate_s) * y_s
    out = jnp.dot(merged.astype(BF16), wo_ref[...], preferred_element_type=F32)
    h = alpha * x + out
    mu = jnp.mean(h, axis=-1, keepdims=True)
    hc = h - mu
    var = jnp.mean(hc * hc, axis=-1, keepdims=True)
    out_ref[...] = hc * lax.rsqrt(var + LN_EPS) * g_ref[...] + b_ref[...]


def _merge(x2, attn, ys, w_in, w_attn_up, w_ssm_up, w_o, ln_g, ln_b, alpha, tm=512):
    n, d = x2.shape
    row = lambda width: pl.BlockSpec((tm, width), lambda i: (i, 0))
    z_col = 3 * QKV_WIDTH
    gate_col = z_col + ATTN_WIDTH + 2 * SSM_WIDTH
    return pl.pallas_call(
        functools.partial(_merge_kernel, alpha),
        grid=(n // tm,),
        in_specs=[row(d), row(ATTN_WIDTH), row(SSM_WIDTH),
                  _col_block_spec(d, ATTN_WIDTH, z_col), _col_block_spec(d, 2 * d, gate_col),
                  _const_spec(w_attn_up.shape), _const_spec(w_ssm_up.shape),
                  _const_spec(w_o.shape), _const_spec((1, d)), _const_spec((1, d))],
        out_specs=row(d),
        out_shape=jax.ShapeDtypeStruct((n, d), F32),
        compiler_params=pltpu.CompilerParams(dimension_semantics=("arbitrary",),
                                             vmem_limit_bytes=VMEM_LIMIT),
        name="merge_out_ln",
    )(x2, attn, ys, w_in, w_in, w_attn_up, w_ssm_up, w_o,
      ln_g.reshape(1, d).astype(F32), ln_b.reshape(1, d).astype(F32))


def _layer(h, w_in, lam_re, lam_im, log_dt, b_re, b_im, c_re, c_im, d_skip,
           w_glu, b_glu, w_attn_up, w_ssm_up, w_o, ln_g, ln_b, alpha):
    bsz, seq, d = h.shape
    assert seq % SPAN == 0 and all(w // r == BLOCK for w, r in GROUP_PATTERNS)
    assert tuple(r for _, r in GROUP_PATTERNS) == (1, PHASE_SPLIT, PHASE_SPLIT * PHASE_SPLIT)
    x2 = h.reshape(bsz * seq, d)
    w_in = w_in.astype(BF16)

    qkv = _qkv_proj(x2, w_in, bsz, seq)
    attn = _attention(qkv, bsz, seq)
    ssm_w = _ssm_weights(lam_re, lam_im, log_dt, b_re, b_im, c_re, c_im)
    ys = _ssm_branch(h, w_in, ssm_w, d_skip, w_glu.astype(BF16), b_glu)
    out = _merge(x2, attn, ys, w_in, w_attn_up.astype(BF16), w_ssm_up.astype(BF16),
                 w_o.astype(BF16), ln_g, ln_b, alpha)
    return out.reshape(bsz, seq, d)


def kernel(x, w_in, lam_re, lam_im, log_dt, b_re, b_im, c_re, c_im, d_skip,
           w_glu, b_glu, w_attn_up, w_ssm_up, w_o, ln_g, ln_b):
    depth = w_in.shape[0]
    alpha = (2.0 * depth) ** 0.25
    h = x
    for layer in range(depth):
        h = _layer(h, w_in[layer], lam_re[layer], lam_im[layer], log_dt[layer], b_re[layer], b_im[layer],
                   c_re[layer], c_im[layer], d_skip[layer], w_glu[layer], b_glu[layer], w_attn_up[layer],
                   w_ssm_up[layer], w_o[layer], ln_g[layer], ln_b[layer], alpha)
    return h
```

```python
import functools
import math

import jax
import jax.numpy as jnp
import numpy as np
from jax import lax
from jax.experimental import pallas as pl
from jax.experimental.pallas import tpu as pltpu

F32 = jnp.float32
BF16 = jnp.bfloat16

HEAD_DIM = 64
N_SLOTS = 8
GROUP_PATTERNS = ((128, 1), (512, 4), (2048, 16))
N_GROUPS = len(GROUP_PATTERNS)
ATTN_WIDTH = N_SLOTS * HEAD_DIM
QKV_WIDTH = N_GROUPS * ATTN_WIDTH
BLOCK = 128
SPAN = GROUP_PATTERNS[-1][1] * BLOCK
UNITS = SPAN // BLOCK
SSM_WIDTH = 512
SSM_GROUP = 16
SSM_GROUPS = SSM_WIDTH // SSM_GROUP
SSM_STATE = 64
LN_EPS = 1e-5
MASK_VALUE = -1e30
LOG2E = math.log2(math.e)
Q_SCALE = LOG2E / math.sqrt(HEAD_DIM)

LANES = 128
SUBLANES = 8
HEAD_PAIRS = ATTN_WIDTH // LANES
PHASE_SPLIT = 4
QUARTER = SPAN // PHASE_SPLIT
BLOCK_ROW_TOKEN = [PHASE_SPLIT * k + ph for ph in range(PHASE_SPLIT) for k in range(BLOCK // PHASE_SPLIT)]
SSM_CHUNK = PHASE_SPLIT
SSM_SLABS = SSM_WIDTH // LANES
GROUPS_PER_SLAB = LANES // SSM_GROUP
SLAB_STATE = GROUPS_PER_SLAB * SSM_STATE
VMEM_LIMIT = 56 * 1024 * 1024


def _const_spec(shape):
    nd = len(shape)
    return pl.BlockSpec(shape, lambda *_: (0,) * nd, pipeline_mode=pl.Buffered(1))


def _col_block_spec(rows, width, start):
    assert start % width == 0
    return pl.BlockSpec((rows, width), lambda *_: (0, start // width), pipeline_mode=pl.Buffered(1))


def _sigmoid(x):
    return 0.5 * jnp.tanh(0.5 * x) + 0.5


def _silu(x):
    return x * _sigmoid(x)


def _lane_tiles(x):
    return [x[:, c * LANES:(c + 1) * LANES] for c in range(x.shape[1] // LANES)]


def _qkv_kernel(x_ref, w_ref, *refs):
    out_refs, x_stage, stage = refs[:-2], refs[-2], refs[-1]
    tm = x_ref.shape[0]
    sub = tm // PHASE_SPLIT
    x = x_ref[...]
    for c, tile in enumerate(_lane_tiles(x)):
        x_stage[c] = tile
    xb = {1: x.astype(BF16),
          PHASE_SPLIT: jnp.concatenate(
              [jnp.concatenate([x_stage[c, pl.ds(q, sub, stride=PHASE_SPLIT), :] for c in range(x_stage.shape[0])],
                               axis=1) for q in range(PHASE_SPLIT)], axis=0).astype(BF16)}
    piece = BLOCK // PHASE_SPLIT
    xb_block_split = jnp.concatenate(
        [jnp.concatenate([x_stage[c, pl.ds(blk * BLOCK + ph, piece, stride=PHASE_SPLIT), :]
                          for c in range(x_stage.shape[0])], axis=1)
         for blk in range(tm // BLOCK) for ph in range(PHASE_SPLIT)], axis=0).astype(BF16)
    for kind in range(3):
        for g, (_, r) in enumerate(GROUP_PATTERNS):
            col = kind * QKV_WIDTH + g * ATTN_WIDTH
            lhs = xb_block_split if (kind == 0 and r == 1) else xb[min(r, PHASE_SPLIT)]
            res = jnp.dot(lhs, w_ref[:, col:col + ATTN_WIDTH], preferred_element_type=F32)
            if kind == 0:
                res = res * Q_SCALE
            o_ref = out_refs[kind * N_GROUPS + g]
            if r == 1:
                o_ref[0] = res.astype(BF16)
            elif r == PHASE_SPLIT:
                for q in range(r):
                    o_ref[q] = res[q * sub:(q + 1) * sub].astype(BF16)
            else:
                for c, tile in enumerate(_lane_tiles(res)):
                    stage[c] = tile
                for q in range(PHASE_SPLIT):
                    for s in range(PHASE_SPLIT):
                        o_ref[q + PHASE_SPLIT * s] = jnp.concatenate(
                            [stage[c, pl.ds(q * sub + s, sub // PHASE_SPLIT, stride=PHASE_SPLIT), :]
                             for c in range(HEAD_PAIRS)], axis=1).astype(BF16)


def _qkv_proj(x2, w_in, bsz, seq, tm=1024):
    n, d = x2.shape
    tiles = seq // tm
    out_specs, out_shape = [], []
    for _ in range(3):
        for _, r in GROUP_PATTERNS:
            out_specs.append(pl.BlockSpec((None, r, tm // r, ATTN_WIDTH),
                                          lambda i: (i // tiles, 0, i % tiles, 0)))
            out_shape.append(jax.ShapeDtypeStruct((bsz, r, seq // r, ATTN_WIDTH), BF16))
    return pl.pallas_call(
        _qkv_kernel,
        grid=(n // tm,),
        in_specs=[pl.BlockSpec((tm, d), lambda i: (i, 0)), _col_block_spec(d, 3 * QKV_WIDTH, 0)],
        out_specs=out_specs,
        out_shape=out_shape,
        scratch_shapes=[pltpu.VMEM((d // LANES, tm, LANES), F32), pltpu.VMEM((HEAD_PAIRS, tm, LANES), F32)],
        compiler_params=pltpu.CompilerParams(dimension_semantics=("arbitrary",),
                                             vmem_limit_bytes=VMEM_LIMIT),
        name="qkv_proj",
    )(x2, w_in)


def _attn_bias():
    slopes = 2.0 ** (-8.0 * np.arange(1, N_SLOTS + 1) / N_SLOTS)
    qi = np.arange(BLOCK)[:, None]
    kj = np.arange(2 * BLOCK)[None, :]
    dist = BLOCK + qi - kj
    valid = (dist >= 0) & (dist <= BLOCK)
    valid = np.stack([valid, valid & (kj >= BLOCK)])
    out = []
    for _, r in GROUP_PATTERNS:
        alibi = (-LOG2E * slopes)[:, None, None, None] * (dist * r)[None, None]
        bias = np.where(valid[None], alibi, MASK_VALUE)
        out.append(bias[:, :, BLOCK_ROW_TOKEN, :] if r == 1 else bias)
    return jnp.asarray(np.stack(out).astype(np.float32))


def _attn_kernel(*refs):
    ins, bias_ref, out_ref = refs[:5 * N_GROUPS], refs[5 * N_GROUPS], refs[5 * N_GROUPS + 1]
    scr = refs[5 * N_GROUPS + 2:]
    kv_scr, (o_stage, l_stage, m_stage, ms_stage, nat_stage) = scr[:3 * N_GROUPS], scr[3 * N_GROUPS:]
    first_span = pl.program_id(2) == 0
    lane = lax.broadcasted_iota(jnp.int32, (BLOCK, LANES), 1)
    low = lane < HEAD_DIM

    for g, (_, r) in enumerate(GROUP_PATTERNS):
        q_ref, kc_ref, kp_ref, vc_ref, vp_ref = ins[5 * g:5 * g + 5]
        k_scr, v0_scr, v1_scr = kv_scr[3 * g:3 * g + 3]
        nb = UNITS // r
        k_scr[:, :BLOCK, :] = kp_ref[...]
        k_scr[:, BLOCK:, :] = kc_ref[...]
        for dst, src in ((slice(0, BLOCK), vp_ref), (slice(BLOCK, None), vc_ref)):
            v = src[...]
            low_v = lax.broadcasted_iota(jnp.int32, v.shape, 2) < HEAD_DIM
            ones = jnp.ones_like(v)
            v0_scr[:, dst, :] = jnp.where(low_v, v, ones)
            v1_scr[:, dst, :] = jnp.where(low_v, ones, v)

        def unit(u, carry, g=g, r=r, nb=nb, q_ref=q_ref, k_scr=k_scr, v_scrs=(v0_scr, v1_scr)):
            p, n = u // nb, u % nb
            row0 = pl.multiple_of(n * BLOCK, BLOCK)
            q = q_ref[p, pl.ds(row0, BLOCK), :]
            kcat = k_scr[p, pl.ds(row0, 2 * BLOCK), :]
            variant = jnp.where(jnp.logical_and(first_span, n == 0), 1, 0)
            acc, tops = [], []
            for hh in range(2):
                keep = low if hh == 0 else jnp.logical_not(low)
                qm = jnp.where(keep, q, jnp.zeros_like(q))
                s = lax.dot_general(qm, kcat, (((1,), (1,)), ((), ())), preferred_element_type=F32)
                s = s + bias_ref[g, hh, variant]
                m = jnp.max(s, axis=1, keepdims=True)
                e = jnp.exp2(s - m).astype(BF16)
                acc.append(jnp.dot(e, v_scrs[hh][p, pl.ds(row0, 2 * BLOCK), :], preferred_element_type=F32))
                tops.append(m)
            staged = ((o_stage, jnp.where(low, acc[0], acc[1])),
                      (l_stage, jnp.where(low, acc[1], acc[0])),
                      (m_stage, jnp.where(low, tops[0], tops[1])),
                      (ms_stage, jnp.where(low, tops[1], tops[0])))
            for stage, val in staged:
                if r == 1:
                    piece = BLOCK // PHASE_SPLIT
                    for ph in range(PHASE_SPLIT):
                        dst = pl.multiple_of(ph * QUARTER + n * piece, piece)
                        stage[g, pl.ds(dst, piece), :] = val[ph * piece:(ph + 1) * piece]
                elif r == PHASE_SPLIT:
                    stage[g, pl.ds(pl.multiple_of(p * QUARTER + n * BLOCK, BLOCK), BLOCK), :] = val
                else:
                    start = (p % PHASE_SPLIT) * QUARTER + p // PHASE_SPLIT
                    stage[g, pl.ds(start, BLOCK, stride=PHASE_SPLIT), :] = val
            return carry

        lax.fori_loop(0, UNITS, unit, 0, unroll=UNITS)

    def mix(i, carry):
        rows = pl.ds(pl.multiple_of(i * BLOCK, BLOCK), BLOCK)

        def weighted_sum(top_stage, val_stage):
            tops = [top_stage[g, rows, :] for g in range(N_GROUPS)]
            top = functools.reduce(jnp.maximum, tops)
            return sum(jnp.exp2(tops[g] - top) * val_stage[g, rows, :] for g in range(N_GROUPS))

        num = weighted_sum(m_stage, o_stage)
        den = pltpu.roll(weighted_sum(ms_stage, l_stage), HEAD_DIM, 1)
        start = (i % PHASE_SPLIT) * (PHASE_SPLIT * BLOCK) + i // PHASE_SPLIT
        nat_stage[pl.ds(start, BLOCK, stride=PHASE_SPLIT), :] = num / den
        return carry

    lax.fori_loop(0, UNITS, mix, 0, unroll=4)
    out_ref[...] = nat_stage[...].astype(BF16)


def _attention(qkv, bsz, seq):
    spans = seq // SPAN
    in_specs, args, scratch = [], [], []
    for g, (_, r) in enumerate(GROUP_PATTERNS):
        nb = UNITS // r
        cur = pl.BlockSpec((None, r, SPAN // r, LANES), lambda hp, b, s: (b, 0, s, hp))
        prev = pl.BlockSpec((None, r, BLOCK, LANES),
                            lambda hp, b, s, nb=nb: (b, 0, jnp.maximum(nb * s - 1, 0), hp))
        q, k, v = qkv[g], qkv[N_GROUPS + g], qkv[2 * N_GROUPS + g]
        in_specs += [cur, cur, prev, cur, prev]
        args += [q, k, k, v, v]
        scratch += [pltpu.VMEM((r, SPAN // r + BLOCK, LANES), BF16)] * 3
    in_specs.append(pl.BlockSpec((N_GROUPS, 2, 2, BLOCK, 2 * BLOCK), lambda hp, b, s: (0, hp, 0, 0, 0)))
    scratch += [pltpu.VMEM((N_GROUPS, SPAN, LANES), F32)] * 4
    scratch += [pltpu.VMEM((SPAN, LANES), F32)]
    return pl.pallas_call(
        _attn_kernel,
        grid=(HEAD_PAIRS, bsz, spans),
        in_specs=in_specs,
        out_specs=pl.BlockSpec((SPAN, LANES), lambda hp, b, s: (b * spans + s, hp)),
        out_shape=jax.ShapeDtypeStruct((bsz * seq, ATTN_WIDTH), BF16),
        scratch_shapes=scratch,
        compiler_params=pltpu.CompilerParams(dimension_semantics=("arbitrary",) * 3,
                                             vmem_limit_bytes=VMEM_LIMIT),
        name="band_attn",
    )(*args, _attn_bias())


def _ssm_weights(lam_re, lam_im, log_dt, b_re, b_im, c_re, c_im):
    t = SSM_CHUNK
    lr, li = lam_re.astype(F32), lam_im.astype(F32)
    dt = jnp.exp(log_dt.astype(F32))[:, None]
    mag = jnp.exp(lr * dt)
    ab_re, ab_im = mag * jnp.cos(li * dt), mag * jnp.sin(li * dt)
    nr, ni = ab_re - 1.0, ab_im
    den = lr * lr + li * li
    coef_re = (nr * lr + ni * li) / den
    coef_im = (ni * lr - nr * li) / den
    br, bi = b_re.astype(F32), b_im.astype(F32)
    bb_re = coef_re[..., None] * br - coef_im[..., None] * bi
    bb_im = coef_re[..., None] * bi + coef_im[..., None] * br
    pw_re, pw_im = [jnp.ones_like(ab_re)], [jnp.zeros_like(ab_im)]
    for _ in range(t):
        pr, pi = pw_re[-1], pw_im[-1]
        pw_re.append(pr * ab_re - pi * ab_im)
        pw_im.append(pr * ab_im + pi * ab_re)
    pw_re, pw_im = jnp.stack(pw_re), jnp.stack(pw_im)
    cr, ci = c_re.astype(F32), c_im.astype(F32)
    n_ch = SSM_GROUPS * SSM_GROUP

    xb_re = pw_re[:t, :, :, None] * bb_re[None] - pw_im[:t, :, :, None] * bb_im[None]
    xb_im = pw_re[:t, :, :, None] * bb_im[None] + pw_im[:t, :, :, None] * bb_re[None]
    xb = jnp.stack([xb_re, xb_im]).transpose(0, 1, 2, 4, 3).reshape(2, t, n_ch, SSM_STATE)
    xb = jnp.concatenate([xb, xb], axis=-1)
    ca_re = cr[None] * pw_re[:, :, None, :] - ci[None] * pw_im[:, :, None, :]
    ca_im = cr[None] * pw_im[:, :, None, :] + ci[None] * pw_re[:, :, None, :]
    ca = jnp.stack([ca_re, -ca_im]).transpose(0, 1, 4, 2, 3).reshape(2, t + 1, SSM_STATE, n_ch)

    w_y, w_st = pl.pallas_call(
        _ssm_expand_kernel,
        grid=(SSM_SLABS,),
        in_specs=[pl.BlockSpec((2, t, LANES, LANES), lambda j: (0, 0, j, 0)),
                  pl.BlockSpec((2, t + 1, SSM_STATE, LANES), lambda j: (0, 0, 0, j))],
        out_specs=[pl.BlockSpec((None, t * LANES + 2 * SLAB_STATE, t * LANES), lambda j: (j, 0, 0)),
                   pl.BlockSpec((None, t * LANES, 2 * SLAB_STATE), lambda j: (j, 0, 0))],
        out_shape=[jax.ShapeDtypeStruct((SSM_SLABS, t * LANES + 2 * SLAB_STATE, t * LANES), BF16),
                   jax.ShapeDtypeStruct((SSM_SLABS, t * LANES, 2 * SLAB_STATE), BF16)],
        compiler_params=pltpu.CompilerParams(dimension_semantics=("arbitrary",),
                                             vmem_limit_bytes=VMEM_LIMIT),
        name="s5_expand",
    )(xb, ca)
    a_t = jnp.stack([pw_re[t].reshape(-1), pw_im[t].reshape(-1)])
    return w_y, w_st, a_t


def _ssm_expand_kernel(xb_ref, ca_ref, wy_ref, wst_ref):
    t = SSM_CHUNK
    keep_st = (lax.broadcasted_iota(jnp.int32, (LANES, SLAB_STATE), 0) // SSM_GROUP
               == lax.broadcasted_iota(jnp.int32, (LANES, SLAB_STATE), 1) // SSM_STATE)
    keep_out = (lax.broadcasted_iota(jnp.int32, (SLAB_STATE, LANES), 0) // SSM_STATE
                == lax.broadcasted_iota(jnp.int32, (SLAB_STATE, LANES), 1) // SSM_GROUP)

    def to_state(d):
        parts = [jnp.where(keep_st, jnp.concatenate([xb_ref[ri, d]] * (SLAB_STATE // LANES), axis=1), 0.0)
                 for ri in range(2)]
        return jnp.concatenate(parts, axis=1)

    def from_state(d):
        parts = [jnp.where(keep_out, jnp.concatenate([ca_ref[ri, d]] * GROUPS_PER_SLAB, axis=0), 0.0)
                 for ri in range(2)]
        return jnp.concatenate(parts, axis=0)

    c0 = from_state(0)
    zero = jnp.zeros((LANES, LANES), BF16)
    for d in range(t):
        st = to_state(d)
        sg = t - 1 - d
        wst_ref[sg * LANES:(sg + 1) * LANES, :] = st.astype(BF16)
        taps = jnp.dot(st, c0, preferred_element_type=F32, precision=lax.Precision.HIGHEST).astype(BF16)
        for sg in range(t - d):
            wy_ref[sg * LANES:(sg + 1) * LANES, (sg + d) * LANES:(sg + d + 1) * LANES] = taps
        if d:
            for tau in range(t - d):
                wy_ref[(tau + d) * LANES:(tau + d + 1) * LANES, tau * LANES:(tau + 1) * LANES] = zero
    for tau in range(t):
        wy_ref[t * LANES:, tau * LANES:(tau + 1) * LANES] = from_state(tau + 1).astype(BF16)


def _ssm_kernel(x_ref, wu_ref, wz_ref, wy_ref, wst_ref, at_ref, dskip_ref, wglu_ref, bglu_ref,
                out_ref, u_st, z_scr, y_st, carry_scr, *state_scrs):
    s_scrs, xp_scrs = state_scrs[:SSM_SLABS], state_scrs[SSM_SLABS:]
    bsz, tl, d_model = x_ref.shape
    t = SSM_CHUNK
    ch = tl // t
    rows = bsz * tl
    crows = bsz * ch
    tiles_per_slab = SLAB_STATE // LANES
    n_tiles = SSM_SLABS * tiles_per_slab
    pitch = s_scrs[0].shape[1] // bsz

    @pl.when(pl.program_id(0) == 0)
    def _():
        carry_scr[...] = jnp.zeros_like(carry_scr)

    xb = x_ref[...].reshape(rows, d_model).astype(BF16)
    for c, tile in enumerate(_lane_tiles(jnp.dot(xb, wu_ref[...], preferred_element_type=F32))):
        u_st[c] = tile
    z_scr[...] = jnp.dot(xb, wz_ref[...], preferred_element_type=F32)

    def chunk_lhs(j):
        return jnp.concatenate([u_st[j, pl.ds(sg, crows, stride=t), :] for sg in range(t)], axis=1).astype(BF16)

    n_slab_tiles = 2 * tiles_per_slab

    def state_increments(j):
        st = jnp.dot(chunk_lhs(j), wst_ref[j], preferred_element_type=F32)
        for c, tile in enumerate(_lane_tiles(st)):
            for b in range(bsz):
                s_scrs[j][c, b * pitch:b * pitch + ch, :] = tile[b * ch:(b + 1) * ch]

    def recurrence(j):
        s_scr, xp_scr = s_scrs[j], xp_scrs[j]
        for c_r in range(tiles_per_slab):
            c_i = tiles_per_slab + c_r
            g_r, g_i = j * tiles_per_slab + c_r, n_tiles + j * tiles_per_slab + c_r
            a_re, a_im = at_ref[g_r:g_r + 1, :], at_ref[g_i:g_i + 1, :]
            x_re, x_im = carry_scr[g_r], carry_scr[g_i]
            for k in range(ch):
                sel = pl.ds(k, bsz, stride=pitch)
                xp_scr[c_r, sel, :] = x_re
                xp_scr[c_i, sel, :] = x_im
                x_re, x_im = (a_re * x_re - a_im * x_im + s_scr[c_r, sel, :],
                              a_re * x_im + a_im * x_re + s_scr[c_i, sel, :])
            carry_scr[g_r] = x_re
            carry_scr[g_i] = x_im

    def response(j):
        xin = [jnp.concatenate([xp_scrs[j][c, b * pitch:b * pitch + ch, :] for b in range(bsz)],
                               axis=0).astype(BF16) for c in range(n_slab_tiles)]
        yj = jnp.dot(jnp.concatenate([chunk_lhs(j)] + xin, axis=1), wy_ref[j], preferred_element_type=F32)
        for tau, tile in enumerate(_lane_tiles(yj)):
            y_st[j, pl.ds(tau, crows, stride=t), :] = tile

    state_increments(0)
    for j in range(SSM_SLABS):
        if j + 1 < SSM_SLABS:
            state_increments(j + 1)
        recurrence(j)
        if j:
            response(j - 1)
    response(SSM_SLABS - 1)

    for b in range(bsz):
        rs = slice(b * tl, (b + 1) * tl)
        y = jnp.concatenate([y_st[c, rs, :] for c in range(SSM_SLABS)], axis=1)
        u = jnp.concatenate([u_st[c, rs, :] for c in range(SSM_SLABS)], axis=1)
        y = jax.nn.gelu(y + dskip_ref[...] * u)
        gate = _sigmoid(jnp.dot(y.astype(BF16), wglu_ref[...], preferred_element_type=F32) + bglu_ref[...])
        out_ref[b] = (y * gate * _silu(z_scr[rs, :])).astype(BF16)


def _ssm_branch(x, w_in, ssm_w, d_skip, w_glu, b_glu, tl=128):
    bsz, seq, d = x.shape
    u_col = 3 * QKV_WIDTH + ATTN_WIDTH
    t = SSM_CHUNK
    rows, ch = bsz * tl, tl // t
    pitch = -(-ch // SUBLANES) * SUBLANES
    pitch += SUBLANES * (1 - (pitch // SUBLANES) % 2)
    w_y, w_st, a_t = ssm_w
    n_tiles2 = 2 * SSM_SLABS * SLAB_STATE // LANES
    a_t = a_t.reshape(n_tiles2, LANES)
    out = pl.pallas_call(
        _ssm_kernel,
        grid=(seq // tl,),
        in_specs=[pl.BlockSpec((bsz, tl, d), lambda i: (0, i, 0)),
                  _col_block_spec(d, SSM_WIDTH, u_col), _col_block_spec(d, SSM_WIDTH, u_col + SSM_WIDTH),
                  _const_spec(w_y.shape),
                  _const_spec(w_st.shape), _const_spec(a_t.shape), _const_spec((1, SSM_WIDTH)),
                  _const_spec(w_glu.shape), _const_spec((1, SSM_WIDTH))],
        out_specs=pl.BlockSpec((bsz, tl, SSM_WIDTH), lambda i: (0, i, 0)),
        out_shape=jax.ShapeDtypeStruct((bsz, seq, SSM_WIDTH), BF16),
        scratch_shapes=[pltpu.VMEM((SSM_SLABS, rows, LANES), F32),
                        pltpu.VMEM((rows, SSM_WIDTH), F32),
                        pltpu.VMEM((SSM_SLABS, rows, LANES), F32),
                        pltpu.VMEM((n_tiles2, bsz, LANES), F32)]
        + [pltpu.VMEM((n_tiles2 // SSM_SLABS, bsz * pitch, LANES), F32)] * (2 * SSM_SLABS),
        compiler_params=pltpu.CompilerParams(dimension_semantics=("arbitrary",),
                                             vmem_limit_bytes=VMEM_LIMIT),
        name="s5_branch",
    )(x, w_in, w_in, w_y, w_st, a_t, d_skip.reshape(1, SSM_WIDTH).astype(F32),
      w_glu, b_glu.reshape(1, SSM_WIDTH).astype(F32))
    return out.reshape(bsz * seq, SSM_WIDTH)


def _merge_kernel(alpha, x_ref, attn_ref, ys_ref, wz_ref, wg_ref, wa_ref, ws_ref, wo_ref, g_ref, b_ref, out_ref):
    x = x_ref[...]
    d_model = x.shape[1]
    xb = x.astype(BF16)
    z_a = jnp.dot(xb, wz_ref[...], preferred_element_type=F32)
    gates = jnp.dot(xb, wg_ref[...], preferred_element_type=F32)
    gate_a = gates[:, :d_model]
    gate_s = gates[:, d_model:]
    attn = attn_ref[...].astype(F32) * _silu(z_a)
    y_a = jnp.dot(attn.astype(BF16), wa_ref[...], preferred_element_type=F32)
    y_s = jnp.dot(ys_ref[...], ws_ref[...], preferred_element_type=F32)
    merged = _sigmoid(gate_a) * y_a + _sigmoid(gate_s) * y_s
    out = jnp.dot(merged.astype(BF16), wo_ref[...], preferred_element_type=F32)
    h = alpha * x + out
    mu = jnp.mean(h, axis=-1, keepdims=True)
    hc = h - mu
    var = jnp.mean(hc * hc, axis=-1, keepdims=True)
    out_ref[...] = hc * lax.rsqrt(var + LN_EPS) * g_ref[...] + b_ref[...]


def _merge(x2, attn, ys, w_in, w_attn_up, w_ssm_up, w_o, ln_g, ln_b, alpha, tm=1024):
    n, d = x2.shape
    row = lambda width: pl.BlockSpec((tm, width), lambda i: (i, 0))
    z_col = 3 * QKV_WIDTH
    gate_col = z_col + ATTN_WIDTH + 2 * SSM_WIDTH
    return pl.pallas_call(
        functools.partial(_merge_kernel, alpha),
        grid=(n // tm,),
        in_specs=[row(d), row(ATTN_WIDTH), row(SSM_WIDTH),
                  _col_block_spec(d, ATTN_WIDTH, z_col), _col_block_spec(d, 2 * d, gate_col),
                  _const_spec(w_attn_up.shape), _const_spec(w_ssm_up.shape),
                  _const_spec(w_o.shape), _const_spec((1, d)), _const_spec((1, d))],
        out_specs=row(d),
        out_shape=jax.ShapeDtypeStruct((n, d), F32),
        compiler_params=pltpu.CompilerParams(dimension_semantics=("arbitrary",),
                                             vmem_limit_bytes=VMEM_LIMIT),
        name="merge_out_ln",
    )(x2, attn, ys, w_in, w_in, w_attn_up, w_ssm_up, w_o,
      ln_g.reshape(1, d).astype(F32), ln_b.reshape(1, d).astype(F32))


def _layer(h, w_in, lam_re, lam_im, log_dt, b_re, b_im, c_re, c_im, d_skip,
           w_glu, b_glu, w_attn_up, w_ssm_up, w_o, ln_g, ln_b, alpha):
    bsz, seq, d = h.shape
    assert seq % SPAN == 0 and all(w // r == BLOCK for w, r in GROUP_PATTERNS)
    assert tuple(r for _, r in GROUP_PATTERNS) == (1, PHASE_SPLIT, PHASE_SPLIT * PHASE_SPLIT)
    x2 = h.reshape(bsz * seq, d)
    w_in = w_in.astype(BF16)

    qkv = _qkv_proj(x2, w_in, bsz, seq)
    attn = _attention(qkv, bsz, seq)
    ssm_w = _ssm_weights(lam_re, lam_im, log_dt, b_re, b_im, c_re, c_im)
    ys = _ssm_branch(h, w_in, ssm_w, d_skip, w_glu.astype(BF16), b_glu)
    out = _merge(x2, attn, ys, w_in, w_attn_up.astype(BF16), w_ssm_up.astype(BF16),
                 w_o.astype(BF16), ln_g, ln_b, alpha)
    return out.reshape(bsz, seq, d)


def kernel(x, w_in, lam_re, lam_im, log_dt, b_re, b_im, c_re, c_im, d_skip,
           w_glu, b_glu, w_attn_up, w_ssm_up, w_o, ln_g, ln_b):
    depth = w_in.shape[0]
    alpha = (2.0 * depth) ** 0.25
    h = x
    for layer in range(depth):
        h = _layer(h, w_in[layer], lam_re[layer], lam_im[layer], log_dt[layer], b_re[layer], b_im[layer],
                   c_re[layer], c_im[layer], d_skip[layer], w_glu[layer], b_glu[layer], w_attn_up[layer],
                   w_ssm_up[layer], w_o[layer], ln_g[layer], ln_b[layer], alpha)
    return h
```

```python
import functools
import math

import jax
import jax.numpy as jnp
import numpy as np
from jax import lax
from jax.experimental import pallas as pl
from jax.experimental.pallas import tpu as pltpu

F32 = jnp.float32
BF16 = jnp.bfloat16

HEAD_DIM = 64
N_SLOTS = 8
GROUP_PATTERNS = ((128, 1), (512, 4), (2048, 16))
N_GROUPS = len(GROUP_PATTERNS)
ATTN_WIDTH = N_SLOTS * HEAD_DIM
QKV_WIDTH = N_GROUPS * ATTN_WIDTH
BLOCK = 128
SPAN = GROUP_PATTERNS[-1][1] * BLOCK
UNITS = SPAN // BLOCK
SSM_WIDTH = 512
SSM_GROUP = 16
SSM_GROUPS = SSM_WIDTH // SSM_GROUP
SSM_STATE = 64
LN_EPS = 1e-5
MASK_VALUE = -1e30
LOG2E = math.log2(math.e)
Q_SCALE = LOG2E / math.sqrt(HEAD_DIM)

LANES = 128
SUBLANES = 8
HEAD_PAIRS = ATTN_WIDTH // LANES
PHASE_SPLIT = 4
QUARTER = SPAN // PHASE_SPLIT
BLOCK_ROW_TOKEN = [PHASE_SPLIT * k + ph for ph in range(PHASE_SPLIT) for k in range(BLOCK // PHASE_SPLIT)]
SSM_CHUNK = PHASE_SPLIT
SSM_SLABS = SSM_WIDTH // LANES
GROUPS_PER_SLAB = LANES // SSM_GROUP
SLAB_STATE = GROUPS_PER_SLAB * SSM_STATE
VMEM_LIMIT = 56 * 1024 * 1024


def _const_spec(shape):
    nd = len(shape)
    return pl.BlockSpec(shape, lambda *_: (0,) * nd, pipeline_mode=pl.Buffered(1))


def _col_block_spec(rows, width, start):
    assert start % width == 0
    return pl.BlockSpec((rows, width), lambda *_: (0, start // width), pipeline_mode=pl.Buffered(1))


def _sigmoid(x):
    return 0.5 * jnp.tanh(0.5 * x) + 0.5


def _silu(x):
    return x * _sigmoid(x)


def _lane_tiles(x):
    return [x[:, c * LANES:(c + 1) * LANES] for c in range(x.shape[1] // LANES)]


def _qkv_kernel(x_ref, w_ref, *refs):
    out_refs, x_stage, stage = refs[:-2], refs[-2], refs[-1]
    tm = x_ref.shape[0]
    sub = tm // PHASE_SPLIT
    x = x_ref[...]
    for c, tile in enumerate(_lane_tiles(x)):
        x_stage[c] = tile
    xb = {1: x.astype(BF16),
          PHASE_SPLIT: jnp.concatenate(
              [jnp.concatenate([x_stage[c, pl.ds(q, sub, stride=PHASE_SPLIT), :] for c in range(x_stage.shape[0])],
                               axis=1) for q in range(PHASE_SPLIT)], axis=0).astype(BF16)}
    piece = BLOCK // PHASE_SPLIT
    xb_block_split = jnp.concatenate(
        [jnp.concatenate([x_stage[c, pl.ds(blk * BLOCK + ph, piece, stride=PHASE_SPLIT), :]
                          for c in range(x_stage.shape[0])], axis=1)
         for blk in range(tm // BLOCK) for ph in range(PHASE_SPLIT)], axis=0).astype(BF16)
    for kind in range(3):
        for g, (_, r) in enumerate(GROUP_PATTERNS):
            col = kind * QKV_WIDTH + g * ATTN_WIDTH
            lhs = xb_block_split if (kind == 0 and r == 1) else xb[min(r, PHASE_SPLIT)]
            res = jnp.dot(lhs, w_ref[:, col:col + ATTN_WIDTH], preferred_element_type=F32)
            if kind == 0:
                res = res * Q_SCALE
            o_ref = out_refs[kind * N_GROUPS + g]
            if r == 1:
                o_ref[0] = res.astype(BF16)
            elif r == PHASE_SPLIT:
                for q in range(r):
                    o_ref[q] = res[q * sub:(q + 1) * sub].astype(BF16)
            else:
                for c, tile in enumerate(_lane_tiles(res)):
                    stage[c] = tile
                for q in range(PHASE_SPLIT):
                    for s in range(PHASE_SPLIT):
                        o_ref[q + PHASE_SPLIT * s] = jnp.concatenate(
                            [stage[c, pl.ds(q * sub + s, sub // PHASE_SPLIT, stride=PHASE_SPLIT), :]
                             for c in range(HEAD_PAIRS)], axis=1).astype(BF16)


def _qkv_proj(x2, w_in, bsz, seq, tm=1024):
    n, d = x2.shape
    tiles = seq // tm
    out_specs, out_shape = [], []
    for _ in range(3):
        for _, r in GROUP_PATTERNS:
            out_specs.append(pl.BlockSpec((None, r, tm // r, ATTN_WIDTH),
                                          lambda i: (i // tiles, 0, i % tiles, 0)))
            out_shape.append(jax.ShapeDtypeStruct((bsz, r, seq // r, ATTN_WIDTH), BF16))
    return pl.pallas_call(
        _qkv_kernel,
        grid=(n // tm,),
        in_specs=[pl.BlockSpec((tm, d), lambda i: (i, 0)), _col_block_spec(d, 3 * QKV_WIDTH, 0)],
        out_specs=out_specs,
        out_shape=out_shape,
        scratch_shapes=[pltpu.VMEM((d // LANES, tm, LANES), F32), pltpu.VMEM((HEAD_PAIRS, tm, LANES), F32)],
        compiler_params=pltpu.CompilerParams(dimension_semantics=("arbitrary",),
                                             vmem_limit_bytes=VMEM_LIMIT),
        name="qkv_proj",
    )(x2, w_in)


def _attn_bias():
    slopes = 2.0 ** (-8.0 * np.arange(1, N_SLOTS + 1) / N_SLOTS)
    qi = np.arange(BLOCK)[:, None]
    kj = np.arange(2 * BLOCK)[None, :]
    dist = BLOCK + qi - kj
    valid = (dist >= 0) & (dist <= BLOCK)
    valid = np.stack([valid, valid & (kj >= BLOCK)])
    out = []
    for _, r in GROUP_PATTERNS:
        alibi = (-LOG2E * slopes)[:, None, None, None] * (dist * r)[None, None]
        bias = np.where(valid[None], alibi, MASK_VALUE)
        out.append(bias[:, :, BLOCK_ROW_TOKEN, :] if r == 1 else bias)
    return jnp.asarray(np.stack(out).astype(np.float32))


def _attn_kernel(*refs):
    ins, bias_ref, out_ref = refs[:5 * N_GROUPS], refs[5 * N_GROUPS], refs[5 * N_GROUPS + 1]
    scr = refs[5 * N_GROUPS + 2:]
    kv_scr, (o_stage, l_stage, m_stage, ms_stage, nat_stage) = scr[:2 * N_GROUPS], scr[2 * N_GROUPS:]
    first_span = pl.program_id(2) == 0
    lane = lax.broadcasted_iota(jnp.int32, (BLOCK, LANES), 1)
    low = lane < HEAD_DIM

    for g, (_, r) in enumerate(GROUP_PATTERNS):
        q_ref, kc_ref, kp_ref, vc_ref, vp_ref = ins[5 * g:5 * g + 5]
        k_scr, v_scr = kv_scr[2 * g:2 * g + 2]
        nb = UNITS // r
        k_scr[:, :BLOCK, :] = kp_ref[...]
        k_scr[:, BLOCK:, :] = kc_ref[...]
        for dst, src in ((slice(0, BLOCK), vp_ref), (slice(BLOCK, None), vc_ref)):
            v = src[...]
            low_v = lax.broadcasted_iota(jnp.int32, v.shape, 2) < HEAD_DIM
            ones = jnp.ones_like(v)
            v_scr[:, dst, :LANES] = jnp.where(low_v, v, ones)
            v_scr[:, dst, LANES:] = jnp.where(low_v, ones, v)

        def unit(u, carry, g=g, r=r, nb=nb, q_ref=q_ref, k_scr=k_scr, v_scrs=(v_scr,)):
            p, n = u // nb, u % nb
            row0 = pl.multiple_of(n * BLOCK, BLOCK)
            q = q_ref[p, pl.ds(row0, BLOCK), :]
            kcat = k_scr[p, pl.ds(row0, 2 * BLOCK), :]
            variant = jnp.where(jnp.logical_and(first_span, n == 0), 1, 0)
            qm = jnp.concatenate([jnp.where(low, q, jnp.zeros_like(q)),
                                  jnp.where(low, jnp.zeros_like(q), q)], axis=0)
            s_both = lax.dot_general(qm, kcat, (((1,), (1,)), ((), ())), preferred_element_type=F32)
            es, tops = [], []
            for hh in range(2):
                s = s_both[hh * BLOCK:(hh + 1) * BLOCK] + bias_ref[g, hh, variant]
                m = jnp.max(s, axis=1, keepdims=True)
                es.append(jnp.exp2(s - m).astype(BF16))
                tops.append(m)
            pv = jnp.dot(jnp.concatenate(es, axis=0), v_scrs[0][p, pl.ds(row0, 2 * BLOCK), :],
                         preferred_element_type=F32)
            acc = [pv[:BLOCK, :LANES], pv[BLOCK:, LANES:]]
            staged = ((o_stage, jnp.where(low, acc[0], acc[1])),
                      (l_stage, jnp.where(low, acc[1], acc[0])),
                      (m_stage, jnp.where(low, tops[0], tops[1])),
                      (ms_stage, jnp.where(low, tops[1], tops[0])))
            for stage, val in staged:
                if r == 1:
                    piece = BLOCK // PHASE_SPLIT
                    for ph in range(PHASE_SPLIT):
                        dst = pl.multiple_of(ph * QUARTER + n * piece, piece)
                        stage[g, pl.ds(dst, piece), :] = val[ph * piece:(ph + 1) * piece]
                elif r == PHASE_SPLIT:
                    stage[g, pl.ds(pl.multiple_of(p * QUARTER + n * BLOCK, BLOCK), BLOCK), :] = val
                else:
                    start = (p % PHASE_SPLIT) * QUARTER + p // PHASE_SPLIT
                    stage[g, pl.ds(start, BLOCK, stride=PHASE_SPLIT), :] = val
            return carry

        lax.fori_loop(0, UNITS, unit, 0, unroll=UNITS)

    def mix(i, carry):
        rows = pl.ds(pl.multiple_of(i * BLOCK, BLOCK), BLOCK)

        def weighted_sum(top_stage, val_stage):
            tops = [top_stage[g, rows, :] for g in range(N_GROUPS)]
            top = functools.reduce(jnp.maximum, tops)
            return sum(jnp.exp2(tops[g] - top) * val_stage[g, rows, :] for g in range(N_GROUPS))

        num = weighted_sum(m_stage, o_stage)
        den = pltpu.roll(weighted_sum(ms_stage, l_stage), HEAD_DIM, 1)
        start = (i % PHASE_SPLIT) * (PHASE_SPLIT * BLOCK) + i // PHASE_SPLIT
        nat_stage[pl.ds(start, BLOCK, stride=PHASE_SPLIT), :] = num / den
        return carry

    lax.fori_loop(0, UNITS, mix, 0, unroll=4)
    out_ref[...] = nat_stage[...].astype(BF16)


def _attention(qkv, bsz, seq):
    spans = seq // SPAN
    in_specs, args, scratch = [], [], []
    for g, (_, r) in enumerate(GROUP_PATTERNS):
        nb = UNITS // r
        cur = pl.BlockSpec((None, r, SPAN // r, LANES), lambda hp, b, s: (b, 0, s, hp))
        prev = pl.BlockSpec((None, r, BLOCK, LANES),
                            lambda hp, b, s, nb=nb: (b, 0, jnp.maximum(nb * s - 1, 0), hp))
        q, k, v = qkv[g], qkv[N_GROUPS + g], qkv[2 * N_GROUPS + g]
        in_specs += [cur, cur, prev, cur, prev]
        args += [q, k, k, v, v]
        scratch += [pltpu.VMEM((r, SPAN // r + BLOCK, LANES), BF16),
                    pltpu.VMEM((r, SPAN // r + BLOCK, 2 * LANES), BF16)]
    in_specs.append(pl.BlockSpec((N_GROUPS, 2, 2, BLOCK, 2 * BLOCK), lambda hp, b, s: (0, hp, 0, 0, 0)))
    scratch += [pltpu.VMEM((N_GROUPS, SPAN, LANES), F32)] * 4
    scratch += [pltpu.VMEM((SPAN, LANES), F32)]
    return pl.pallas_call(
        _attn_kernel,
        grid=(HEAD_PAIRS, bsz, spans),
        in_specs=in_specs,
        out_specs=pl.BlockSpec((SPAN, LANES), lambda hp, b, s: (b * spans + s, hp)),
        out_shape=jax.ShapeDtypeStruct((bsz * seq, ATTN_WIDTH), BF16),
        scratch_shapes=scratch,
        compiler_params=pltpu.CompilerParams(dimension_semantics=("arbitrary",) * 3,
                                             vmem_limit_bytes=VMEM_LIMIT),
        name="band_attn",
    )(*args, _attn_bias())


def _ssm_weights(lam_re, lam_im, log_dt, b_re, b_im, c_re, c_im):
    t = SSM_CHUNK
    lr, li = lam_re.astype(F32), lam_im.astype(F32)
    dt = jnp.exp(log_dt.astype(F32))[:, None]
    mag = jnp.exp(lr * dt)
    ab_re, ab_im = mag * jnp.cos(li * dt), mag * jnp.sin(li * dt)
    nr, ni = ab_re - 1.0, ab_im
    den = lr * lr + li * li
    coef_re = (nr * lr + ni * li) / den
    coef_im = (ni * lr - nr * li) / den
    br, bi = b_re.astype(F32), b_im.astype(F32)
    bb_re = coef_re[..., None] * br - coef_im[..., None] * bi
    bb_im = coef_re[..., None] * bi + coef_im[..., None] * br
    pw_re, pw_im = [jnp.ones_like(ab_re)], [jnp.zeros_like(ab_im)]
    for _ in range(t):
        pr, pi = pw_re[-1], pw_im[-1]
        pw_re.append(pr * ab_re - pi * ab_im)
        pw_im.append(pr * ab_im + pi * ab_re)
    pw_re, pw_im = jnp.stack(pw_re), jnp.stack(pw_im)
    cr, ci = c_re.astype(F32), c_im.astype(F32)
    n_ch = SSM_GROUPS * SSM_GROUP

    xb_re = pw_re[:t, :, :, None] * bb_re[None] - pw_im[:t, :, :, None] * bb_im[None]
    xb_im = pw_re[:t, :, :, None] * bb_im[None] + pw_im[:t, :, :, None] * bb_re[None]
    xb = jnp.stack([xb_re, xb_im]).transpose(0, 1, 2, 4, 3).reshape(2, t, n_ch, SSM_STATE)
    xb = jnp.concatenate([xb, xb], axis=-1)
    ca_re = cr[None] * pw_re[:, :, None, :] - ci[None] * pw_im[:, :, None, :]
    ca_im = cr[None] * pw_im[:, :, None, :] + ci[None] * pw_re[:, :, None, :]
    ca = jnp.stack([ca_re, -ca_im]).transpose(0, 1, 4, 2, 3).reshape(2, t + 1, SSM_STATE, n_ch)

    w_y, w_st = pl.pallas_call(
        _ssm_expand_kernel,
        grid=(SSM_SLABS,),
        in_specs=[pl.BlockSpec((2, t, LANES, LANES), lambda j: (0, 0, j, 0)),
                  pl.BlockSpec((2, t + 1, SSM_STATE, LANES), lambda j: (0, 0, 0, j))],
        out_specs=[pl.BlockSpec((None, t * LANES + 2 * SLAB_STATE, t * LANES), lambda j: (j, 0, 0)),
                   pl.BlockSpec((None, t * LANES, 2 * SLAB_STATE), lambda j: (j, 0, 0))],
        out_shape=[jax.ShapeDtypeStruct((SSM_SLABS, t * LANES + 2 * SLAB_STATE, t * LANES), BF16),
                   jax.ShapeDtypeStruct((SSM_SLABS, t * LANES, 2 * SLAB_STATE), BF16)],
        compiler_params=pltpu.CompilerParams(dimension_semantics=("arbitrary",),
                                             vmem_limit_bytes=VMEM_LIMIT),
        name="s5_expand",
    )(xb, ca)
    a_t = jnp.stack([pw_re[t].reshape(-1), pw_im[t].reshape(-1)])
    return w_y, w_st, a_t


def _ssm_expand_kernel(xb_ref, ca_ref, wy_ref, wst_ref):
    t = SSM_CHUNK
    keep_st = (lax.broadcasted_iota(jnp.int32, (LANES, SLAB_STATE), 0) // SSM_GROUP
               == lax.broadcasted_iota(jnp.int32, (LANES, SLAB_STATE), 1) // SSM_STATE)
    keep_out = (lax.broadcasted_iota(jnp.int32, (SLAB_STATE, LANES), 0) // SSM_STATE
                == lax.broadcasted_iota(jnp.int32, (SLAB_STATE, LANES), 1) // SSM_GROUP)

    def to_state(d):
        parts = [jnp.where(keep_st, jnp.concatenate([xb_ref[ri, d]] * (SLAB_STATE // LANES), axis=1), 0.0)
                 for ri in range(2)]
        return jnp.concatenate(parts, axis=1)

    def from_state(d):
        parts = [jnp.where(keep_out, jnp.concatenate([ca_ref[ri, d]] * GROUPS_PER_SLAB, axis=0), 0.0)
                 for ri in range(2)]
        return jnp.concatenate(parts, axis=0)

    c0 = from_state(0)
    zero = jnp.zeros((LANES, LANES), BF16)
    for d in range(t):
        st = to_state(d)
        sg = t - 1 - d
        wst_ref[sg * LANES:(sg + 1) * LANES, :] = st.astype(BF16)
        taps = jnp.dot(st, c0, preferred_element_type=F32, precision=lax.Precision.HIGHEST).astype(BF16)
        for sg in range(t - d):
            wy_ref[sg * LANES:(sg + 1) * LANES, (sg + d) * LANES:(sg + d + 1) * LANES] = taps
        if d:
            for tau in range(t - d):
                wy_ref[(tau + d) * LANES:(tau + d + 1) * LANES, tau * LANES:(tau + 1) * LANES] = zero
    for tau in range(t):
        wy_ref[t * LANES:, tau * LANES:(tau + 1) * LANES] = from_state(tau + 1).astype(BF16)


def _ssm_kernel(x_ref, wu_ref, wz_ref, wy_ref, wst_ref, at_ref, dskip_ref, wglu_ref, bglu_ref,
                out_ref, u_st, z_scr, y_st, carry_scr, *state_scrs):
    s_scrs, xp_scrs = state_scrs[:SSM_SLABS], state_scrs[SSM_SLABS:]
    bsz, tl, d_model = x_ref.shape
    t = SSM_CHUNK
    ch = tl // t
    rows = bsz * tl
    crows = bsz * ch
    tiles_per_slab = SLAB_STATE // LANES
    n_tiles = SSM_SLABS * tiles_per_slab
    pitch = s_scrs[0].shape[1] // bsz

    @pl.when(pl.program_id(0) == 0)
    def _():
        carry_scr[...] = jnp.zeros_like(carry_scr)

    xb = x_ref[...].reshape(rows, d_model).astype(BF16)
    for c, tile in enumerate(_lane_tiles(jnp.dot(xb, wu_ref[...], preferred_element_type=F32))):
        u_st[c] = tile
    z_scr[...] = jnp.dot(xb, wz_ref[...], preferred_element_type=F32)

    def chunk_lhs(j):
        return jnp.concatenate([u_st[j, pl.ds(sg, crows, stride=t), :] for sg in range(t)], axis=1).astype(BF16)

    n_slab_tiles = 2 * tiles_per_slab

    def state_increments(j):
        st = jnp.dot(chunk_lhs(j), wst_ref[j], preferred_element_type=F32)
        for c, tile in enumerate(_lane_tiles(st)):
            for b in range(bsz):
                s_scrs[j][c, b * pitch:b * pitch + ch, :] = tile[b * ch:(b + 1) * ch]

    def recurrence(j):
        s_scr, xp_scr = s_scrs[j], xp_scrs[j]
        for c_r in range(tiles_per_slab):
            c_i = tiles_per_slab + c_r
            g_r, g_i = j * tiles_per_slab + c_r, n_tiles + j * tiles_per_slab + c_r
            a_re, a_im = at_ref[g_r:g_r + 1, :], at_ref[g_i:g_i + 1, :]
            x_re, x_im = carry_scr[g_r], carry_scr[g_i]
            for k in range(ch):
                sel = pl.ds(k, bsz, stride=pitch)
                xp_scr[c_r, sel, :] = x_re
                xp_scr[c_i, sel, :] = x_im
                x_re, x_im = (a_re * x_re - a_im * x_im + s_scr[c_r, sel, :],
                              a_re * x_im + a_im * x_re + s_scr[c_i, sel, :])
            carry_scr[g_r] = x_re
            carry_scr[g_i] = x_im

    def response(j):
        xin = [jnp.concatenate([xp_scrs[j][c, b * pitch:b * pitch + ch, :] for b in range(bsz)],
                               axis=0).astype(BF16) for c in range(n_slab_tiles)]
        yj = jnp.dot(jnp.concatenate([chunk_lhs(j)] + xin, axis=1), wy_ref[j], preferred_element_type=F32)
        for tau, tile in enumerate(_lane_tiles(yj)):
            y_st[j, pl.ds(tau, crows, stride=t), :] = tile

    state_increments(0)
    for j in range(SSM_SLABS):
        if j + 1 < SSM_SLABS:
            state_increments(j + 1)
        recurrence(j)
        if j:
            response(j - 1)
    response(SSM_SLABS - 1)

    for b in range(bsz):
        rs = slice(b * tl, (b + 1) * tl)
        y = jnp.concatenate([y_st[c, rs, :] for c in range(SSM_SLABS)], axis=1)
        u = jnp.concatenate([u_st[c, rs, :] for c in range(SSM_SLABS)], axis=1)
        y = jax.nn.gelu(y + dskip_ref[...] * u)
        gate = _sigmoid(jnp.dot(y.astype(BF16), wglu_ref[...], preferred_element_type=F32) + bglu_ref[...])
        out_ref[b] = (y * gate * _silu(z_scr[rs, :])).astype(BF16)


def _ssm_branch(x, w_in, ssm_w, d_skip, w_glu, b_glu, tl=128):
    bsz, seq, d = x.shape
    u_col = 3 * QKV_WIDTH + ATTN_WIDTH
    t = SSM_CHUNK
    rows, ch = bsz * tl, tl // t
    pitch = -(-ch // SUBLANES) * SUBLANES
    pitch += SUBLANES * (1 - (pitch // SUBLANES) % 2)
    w_y, w_st, a_t = ssm_w
    n_tiles2 = 2 * SSM_SLABS * SLAB_STATE // LANES
    a_t = a_t.reshape(n_tiles2, LANES)
    out = pl.pallas_call(
        _ssm_kernel,
        grid=(seq // tl,),
        in_specs=[pl.BlockSpec((bsz, tl, d), lambda i: (0, i, 0)),
                  _col_block_spec(d, SSM_WIDTH, u_col), _col_block_spec(d, SSM_WIDTH, u_col + SSM_WIDTH),
                  _const_spec(w_y.shape),
                  _const_spec(w_st.shape), _const_spec(a_t.shape), _const_spec((1, SSM_WIDTH)),
                  _const_spec(w_glu.shape), _const_spec((1, SSM_WIDTH))],
        out_specs=pl.BlockSpec((bsz, tl, SSM_WIDTH), lambda i: (0, i, 0)),
        out_shape=jax.ShapeDtypeStruct((bsz, seq, SSM_WIDTH), BF16),
        scratch_shapes=[pltpu.VMEM((SSM_SLABS, rows, LANES), F32),
                        pltpu.VMEM((rows, SSM_WIDTH), F32),
                        pltpu.VMEM((SSM_SLABS, rows, LANES), F32),
                        pltpu.VMEM((n_tiles2, bsz, LANES), F32)]
        + [pltpu.VMEM((n_tiles2 // SSM_SLABS, bsz * pitch, LANES), F32)] * (2 * SSM_SLABS),
        compiler_params=pltpu.CompilerParams(dimension_semantics=("arbitrary",),
                                             vmem_limit_bytes=VMEM_LIMIT),
        name="s5_branch",
    )(x, w_in, w_in, w_y, w_st, a_t, d_skip.reshape(1, SSM_WIDTH).astype(F32),
      w_glu, b_glu.reshape(1, SSM_WIDTH).astype(F32))
    return out.reshape(bsz * seq, SSM_WIDTH)


def _merge_kernel(alpha, x_ref, attn_ref, ys_ref, wz_ref, wg_ref, wa_ref, ws_ref, wo_ref, g_ref, b_ref, out_ref):
    x = x_ref[...]
    d_model = x.shape[1]
    xb = x.astype(BF16)
    z_a = jnp.dot(xb, wz_ref[...], preferred_element_type=F32)
    gates = jnp.dot(xb, wg_ref[...], preferred_element_type=F32)
    gate_a = gates[:, :d_model]
    gate_s = gates[:, d_model:]
    attn = attn_ref[...].astype(F32) * _silu(z_a)
    y_a = jnp.dot(attn.astype(BF16), wa_ref[...], preferred_element_type=F32)
    y_s = jnp.dot(ys_ref[...], ws_ref[...], preferred_element_type=F32)
    merged = _sigmoid(gate_a) * y_a + _sigmoid(gate_s) * y_s
    out = jnp.dot(merged.astype(BF16), wo_ref[...], preferred_element_type=F32)
    h = alpha * x + out
    mu = jnp.mean(h, axis=-1, keepdims=True)
    hc = h - mu
    var = jnp.mean(hc * hc, axis=-1, keepdims=True)
    out_ref[...] = hc * lax.rsqrt(var + LN_EPS) * g_ref[...] + b_ref[...]


def _merge(x2, attn, ys, w_in, w_attn_up, w_ssm_up, w_o, ln_g, ln_b, alpha, tm=1024):
    n, d = x2.shape
    row = lambda width: pl.BlockSpec((tm, width), lambda i: (i, 0))
    z_col = 3 * QKV_WIDTH
    gate_col = z_col + ATTN_WIDTH + 2 * SSM_WIDTH
    return pl.pallas_call(
        functools.partial(_merge_kernel, alpha),
        grid=(n // tm,),
        in_specs=[row(d), row(ATTN_WIDTH), row(SSM_WIDTH),
                  _col_block_spec(d, ATTN_WIDTH, z_col), _col_block_spec(d, 2 * d, gate_col),
                  _const_spec(w_attn_up.shape), _const_spec(w_ssm_up.shape),
                  _const_spec(w_o.shape), _const_spec((1, d)), _const_spec((1, d))],
        out_specs=row(d),
        out_shape=jax.ShapeDtypeStruct((n, d), F32),
        compiler_params=pltpu.CompilerParams(dimension_semantics=("arbitrary",),
                                             vmem_limit_bytes=VMEM_LIMIT),
        name="merge_out_ln",
    )(x2, attn, ys, w_in, w_in, w_attn_up, w_ssm_up, w_o,
      ln_g.reshape(1, d).astype(F32), ln_b.reshape(1, d).astype(F32))


def _layer(h, w_in, lam_re, lam_im, log_dt, b_re, b_im, c_re, c_im, d_skip,
           w_glu, b_glu, w_attn_up, w_ssm_up, w_o, ln_g, ln_b, alpha):
    bsz, seq, d = h.shape
    assert seq % SPAN == 0 and all(w // r == BLOCK for w, r in GROUP_PATTERNS)
    assert tuple(r for _, r in GROUP_PATTERNS) == (1, PHASE_SPLIT, PHASE_SPLIT * PHASE_SPLIT)
    x2 = h.reshape(bsz * seq, d)
    w_in = w_in.astype(BF16)

    qkv = _qkv_proj(x2, w_in, bsz, seq)
    attn = _attention(qkv, bsz, seq)
    ssm_w = _ssm_weights(lam_re, lam_im, log_dt, b_re, b_im, c_re, c_im)
    ys = _ssm_branch(h, w_in, ssm_w, d_skip, w_glu.astype(BF16), b_glu)
    out = _merge(x2, attn, ys, w_in, w_attn_up.astype(BF16), w_ssm_up.astype(BF16),
                 w_o.astype(BF16), ln_g, ln_b, alpha)
    return out.reshape(bsz, seq, d)


def kernel(x, w_in, lam_re, lam_im, log_dt, b_re, b_im, c_re, c_im, d_skip,
           w_glu, b_glu, w_attn_up, w_ssm_up, w_o, ln_g, ln_b):
    depth = w_in.shape[0]
    alpha = (2.0 * depth) ** 0.25
    h = x
    for layer in range(depth):
        h = _layer(h, w_in[layer], lam_re[layer], lam_im[layer], log_dt[layer], b_re[layer], b_im[layer],
                   c_re[layer], c_im[layer], d_skip[layer], w_glu[layer], b_glu[layer], w_attn_up[layer],
                   w_ssm_up[layer], w_o[layer], ln_g[layer], ln_b[layer], alpha)
    return h
```

```python
import functools
import math

import jax
import jax.numpy as jnp
import numpy as np
from jax import lax
from jax.experimental import pallas as pl
from jax.experimental.pallas import tpu as pltpu

F32 = jnp.float32
BF16 = jnp.bfloat16

HEAD_DIM = 64
N_SLOTS = 8
GROUP_PATTERNS = ((128, 1), (512, 4), (2048, 16))
N_GROUPS = len(GROUP_PATTERNS)
ATTN_WIDTH = N_SLOTS * HEAD_DIM
QKV_WIDTH = N_GROUPS * ATTN_WIDTH
BLOCK = 128
SPAN = GROUP_PATTERNS[-1][1] * BLOCK
UNITS = SPAN // BLOCK
SSM_WIDTH = 512
SSM_GROUP = 16
SSM_GROUPS = SSM_WIDTH // SSM_GROUP
SSM_STATE = 64
LN_EPS = 1e-5
MASK_VALUE = -1e30
LOG2E = math.log2(math.e)
Q_SCALE = LOG2E / math.sqrt(HEAD_DIM)

LANES = 128
SUBLANES = 8
HEAD_PAIRS = ATTN_WIDTH // LANES
PHASE_SPLIT = 4
QUARTER = SPAN // PHASE_SPLIT
BLOCK_ROW_TOKEN = [PHASE_SPLIT * k + ph for ph in range(PHASE_SPLIT) for k in range(BLOCK // PHASE_SPLIT)]
SSM_CHUNK = PHASE_SPLIT
SSM_SLABS = SSM_WIDTH // LANES
GROUPS_PER_SLAB = LANES // SSM_GROUP
SLAB_STATE = GROUPS_PER_SLAB * SSM_STATE
VMEM_LIMIT = 56 * 1024 * 1024


def _const_spec(shape):
    nd = len(shape)
    return pl.BlockSpec(shape, lambda *_: (0,) * nd, pipeline_mode=pl.Buffered(1))


def _col_block_spec(rows, width, start):
    assert start % width == 0
    return pl.BlockSpec((rows, width), lambda *_: (0, start // width), pipeline_mode=pl.Buffered(1))


def _sigmoid(x):
    return 0.5 * jnp.tanh(0.5 * x) + 0.5


def _silu(x):
    return x * _sigmoid(x)


def _lane_tiles(x):
    return [x[:, c * LANES:(c + 1) * LANES] for c in range(x.shape[1] // LANES)]


def _qkv_kernel(x_ref, w_ref, *refs):
    out_refs, x_stage, stage = refs[:-2], refs[-2], refs[-1]
    tm = x_ref.shape[0]
    sub = tm // PHASE_SPLIT
    x = x_ref[...]
    for c, tile in enumerate(_lane_tiles(x)):
        x_stage[c] = tile
    xb = {1: x.astype(BF16),
          PHASE_SPLIT: jnp.concatenate(
              [jnp.concatenate([x_stage[c, pl.ds(q, sub, stride=PHASE_SPLIT), :] for c in range(x_stage.shape[0])],
                               axis=1) for q in range(PHASE_SPLIT)], axis=0).astype(BF16)}
    piece = BLOCK // PHASE_SPLIT
    xb_block_split = jnp.concatenate(
        [jnp.concatenate([x_stage[c, pl.ds(blk * BLOCK + ph, piece, stride=PHASE_SPLIT), :]
                          for c in range(x_stage.shape[0])], axis=1)
         for blk in range(tm // BLOCK) for ph in range(PHASE_SPLIT)], axis=0).astype(BF16)
    for kind in range(3):
        for g, (_, r) in enumerate(GROUP_PATTERNS):
            col = kind * QKV_WIDTH + g * ATTN_WIDTH
            lhs = xb_block_split if (kind == 0 and r == 1) else xb[min(r, PHASE_SPLIT)]
            res = jnp.dot(lhs, w_ref[:, col:col + ATTN_WIDTH], preferred_element_type=F32)
            if kind == 0:
                res = res * Q_SCALE
            o_ref = out_refs[kind * N_GROUPS + g]
            if r == 1:
                o_ref[0] = res.astype(BF16)
            elif r == PHASE_SPLIT:
                for q in range(r):
                    o_ref[q] = res[q * sub:(q + 1) * sub].astype(BF16)
            else:
                for c, tile in enumerate(_lane_tiles(res)):
                    stage[c] = tile
                for q in range(PHASE_SPLIT):
                    for s in range(PHASE_SPLIT):
                        o_ref[q + PHASE_SPLIT * s] = jnp.concatenate(
                            [stage[c, pl.ds(q * sub + s, sub // PHASE_SPLIT, stride=PHASE_SPLIT), :]
                             for c in range(HEAD_PAIRS)], axis=1).astype(BF16)


def _qkv_proj(x2, w_in, bsz, seq, tm=1024):
    n, d = x2.shape
    tiles = seq // tm
    out_specs, out_shape = [], []
    for _ in range(3):
        for _, r in GROUP_PATTERNS:
            out_specs.append(pl.BlockSpec((None, r, tm // r, ATTN_WIDTH),
                                          lambda i: (i // tiles, 0, i % tiles, 0)))
            out_shape.append(jax.ShapeDtypeStruct((bsz, r, seq // r, ATTN_WIDTH), BF16))
    return pl.pallas_call(
        _qkv_kernel,
        grid=(n // tm,),
        in_specs=[pl.BlockSpec((tm, d), lambda i: (i, 0)), _col_block_spec(d, 3 * QKV_WIDTH, 0)],
        out_specs=out_specs,
        out_shape=out_shape,
        scratch_shapes=[pltpu.VMEM((d // LANES, tm, LANES), F32), pltpu.VMEM((HEAD_PAIRS, tm, LANES), F32)],
        compiler_params=pltpu.CompilerParams(dimension_semantics=("arbitrary",),
                                             vmem_limit_bytes=VMEM_LIMIT),
        name="qkv_proj",
    )(x2, w_in)


def _attn_bias():
    slopes = 2.0 ** (-8.0 * np.arange(1, N_SLOTS + 1) / N_SLOTS)
    qi = np.arange(BLOCK)[:, None]
    kj = np.arange(2 * BLOCK)[None, :]
    dist = BLOCK + qi - kj
    valid = (dist >= 0) & (dist <= BLOCK)
    valid = np.stack([valid, valid & (kj >= BLOCK)])
    out = []
    for _, r in GROUP_PATTERNS:
        alibi = (-LOG2E * slopes)[:, None, None, None] * (dist * r)[None, None]
        bias = np.where(valid[None], alibi, MASK_VALUE)
        out.append(bias[:, :, BLOCK_ROW_TOKEN, :] if r == 1 else bias)
    return jnp.asarray(np.stack(out).astype(np.float32))


def _attn_kernel(*refs):
    ins, bias_ref, out_ref = refs[:5 * N_GROUPS], refs[5 * N_GROUPS], refs[5 * N_GROUPS + 1]
    scr = refs[5 * N_GROUPS + 2:]
    kv_scr, (o_stage, l_stage, m_stage, nat_stage) = scr[:2 * N_GROUPS], scr[2 * N_GROUPS:]
    first_span = pl.program_id(2) == 0
    lane = lax.broadcasted_iota(jnp.int32, (BLOCK, LANES), 1)
    low = lane < HEAD_DIM

    for g, (_, r) in enumerate(GROUP_PATTERNS):
        q_ref, kc_ref, kp_ref, vc_ref, vp_ref = ins[5 * g:5 * g + 5]
        k_scr, v_scr = kv_scr[2 * g:2 * g + 2]
        nb = UNITS // r
        k_scr[:, :BLOCK, :] = kp_ref[...]
        k_scr[:, BLOCK:, :] = kc_ref[...]
        v_scr[:, :BLOCK, :LANES] = vp_ref[...]
        v_scr[:, BLOCK:, :LANES] = vc_ref[...]
        v_scr[:, :, LANES:] = jnp.ones(v_scr.shape[:2] + (LANES,), BF16)

        def unit(u, carry, g=g, r=r, nb=nb, q_ref=q_ref, k_scr=k_scr, v_scr=v_scr):
            p, n = u // nb, u % nb
            row0 = pl.multiple_of(n * BLOCK, BLOCK)
            q = q_ref[p, pl.ds(row0, BLOCK), :]
            kcat = k_scr[p, pl.ds(row0, 2 * BLOCK), :]
            variant = jnp.where(jnp.logical_and(first_span, n == 0), 1, 0)
            qm = jnp.concatenate([jnp.where(low, q, jnp.zeros_like(q)),
                                  jnp.where(low, jnp.zeros_like(q), q)], axis=0)
            s_both = lax.dot_general(qm, kcat, (((1,), (1,)), ((), ())), preferred_element_type=F32)
            es, tops = [], []
            for hh in range(2):
                s = s_both[hh * BLOCK:(hh + 1) * BLOCK] + bias_ref[g, hh, variant]
                m = jnp.max(s, axis=1, keepdims=True)
                es.append(jnp.exp2(s - m).astype(BF16))
                tops.append(m)
            pv = jnp.dot(jnp.concatenate(es, axis=0), v_scr[p, pl.ds(row0, 2 * BLOCK), :],
                         preferred_element_type=F32)
            staged = ((o_stage, jnp.where(low, pv[:BLOCK, :LANES], pv[BLOCK:, :LANES])),
                      (l_stage, jnp.where(low, pv[:BLOCK, LANES:], pv[BLOCK:, LANES:])),
                      (m_stage, jnp.where(low, tops[0], tops[1])))
            for stage, val in staged:
                if r == 1:
                    piece = BLOCK // PHASE_SPLIT
                    for ph in range(PHASE_SPLIT):
                        dst = pl.multiple_of(ph * QUARTER + n * piece, piece)
                        stage[g, pl.ds(dst, piece), :] = val[ph * piece:(ph + 1) * piece]
                elif r == PHASE_SPLIT:
                    stage[g, pl.ds(pl.multiple_of(p * QUARTER + n * BLOCK, BLOCK), BLOCK), :] = val
                else:
                    start = (p % PHASE_SPLIT) * QUARTER + p // PHASE_SPLIT
                    stage[g, pl.ds(start, BLOCK, stride=PHASE_SPLIT), :] = val
            return carry

        lax.fori_loop(0, UNITS, unit, 0, unroll=UNITS)

    def mix(i, carry):
        rows = pl.ds(pl.multiple_of(i * BLOCK, BLOCK), BLOCK)

        tops = [m_stage[g, rows, :] for g in range(N_GROUPS)]
        top = functools.reduce(jnp.maximum, tops)
        weights = [jnp.exp2(tops[g] - top) for g in range(N_GROUPS)]
        num = sum(weights[g] * o_stage[g, rows, :] for g in range(N_GROUPS))
        den = sum(weights[g] * l_stage[g, rows, :] for g in range(N_GROUPS))
        start = (i % PHASE_SPLIT) * (PHASE_SPLIT * BLOCK) + i // PHASE_SPLIT
        nat_stage[pl.ds(start, BLOCK, stride=PHASE_SPLIT), :] = num / den
        return carry

    lax.fori_loop(0, UNITS, mix, 0, unroll=4)
    out_ref[...] = nat_stage[...].astype(BF16)


def _attention(qkv, bsz, seq):
    spans = seq // SPAN
    in_specs, args, scratch = [], [], []
    for g, (_, r) in enumerate(GROUP_PATTERNS):
        nb = UNITS // r
        cur = pl.BlockSpec((None, r, SPAN // r, LANES), lambda hp, b, s: (b, 0, s, hp))
        prev = pl.BlockSpec((None, r, BLOCK, LANES),
                            lambda hp, b, s, nb=nb: (b, 0, jnp.maximum(nb * s - 1, 0), hp))
        q, k, v = qkv[g], qkv[N_GROUPS + g], qkv[2 * N_GROUPS + g]
        in_specs += [cur, cur, prev, cur, prev]
        args += [q, k, k, v, v]
        scratch += [pltpu.VMEM((r, SPAN // r + BLOCK, LANES), BF16),
                    pltpu.VMEM((r, SPAN // r + BLOCK, 2 * LANES), BF16)]
    in_specs.append(pl.BlockSpec((N_GROUPS, 2, 2, BLOCK, 2 * BLOCK), lambda hp, b, s: (0, hp, 0, 0, 0)))
    scratch += [pltpu.VMEM((N_GROUPS, SPAN, LANES), F32)] * 3
    scratch += [pltpu.VMEM((SPAN, LANES), F32)]
    return pl.pallas_call(
        _attn_kernel,
        grid=(HEAD_PAIRS, bsz, spans),
        in_specs=in_specs,
        out_specs=pl.BlockSpec((SPAN, LANES), lambda hp, b, s: (b * spans + s, hp)),
        out_shape=jax.ShapeDtypeStruct((bsz * seq, ATTN_WIDTH), BF16),
        scratch_shapes=scratch,
        compiler_params=pltpu.CompilerParams(dimension_semantics=("arbitrary",) * 3,
                                             vmem_limit_bytes=VMEM_LIMIT),
        name="band_attn",
    )(*args, _attn_bias())


def _ssm_weights(lam_re, lam_im, log_dt, b_re, b_im, c_re, c_im):
    t = SSM_CHUNK
    lr, li = lam_re.astype(F32), lam_im.astype(F32)
    dt = jnp.exp(log_dt.astype(F32))[:, None]
    mag = jnp.exp(lr * dt)
    ab_re, ab_im = mag * jnp.cos(li * dt), mag * jnp.sin(li * dt)
    nr, ni = ab_re - 1.0, ab_im
    den = lr * lr + li * li
    coef_re = (nr * lr + ni * li) / den
    coef_im = (ni * lr - nr * li) / den
    br, bi = b_re.astype(F32), b_im.astype(F32)
    bb_re = coef_re[..., None] * br - coef_im[..., None] * bi
    bb_im = coef_re[..., None] * bi + coef_im[..., None] * br
    pw_re, pw_im = [jnp.ones_like(ab_re)], [jnp.zeros_like(ab_im)]
    for _ in range(t):
        pr, pi = pw_re[-1], pw_im[-1]
        pw_re.append(pr * ab_re - pi * ab_im)
        pw_im.append(pr * ab_im + pi * ab_re)
    pw_re, pw_im = jnp.stack(pw_re), jnp.stack(pw_im)
    cr, ci = c_re.astype(F32), c_im.astype(F32)
    n_ch = SSM_GROUPS * SSM_GROUP

    xb_re = pw_re[:t, :, :, None] * bb_re[None] - pw_im[:t, :, :, None] * bb_im[None]
    xb_im = pw_re[:t, :, :, None] * bb_im[None] + pw_im[:t, :, :, None] * bb_re[None]
    xb = jnp.stack([xb_re, xb_im]).transpose(0, 1, 2, 4, 3).reshape(2, t, n_ch, SSM_STATE)
    xb = jnp.concatenate([xb, xb], axis=-1)
    ca_re = cr[None] * pw_re[:, :, None, :] - ci[None] * pw_im[:, :, None, :]
    ca_im = cr[None] * pw_im[:, :, None, :] + ci[None] * pw_re[:, :, None, :]
    ca = jnp.stack([ca_re, -ca_im]).transpose(0, 1, 4, 2, 3).reshape(2, t + 1, SSM_STATE, n_ch)

    w_y, w_st = pl.pallas_call(
        _ssm_expand_kernel,
        grid=(SSM_SLABS,),
        in_specs=[pl.BlockSpec((2, t, LANES, LANES), lambda j: (0, 0, j, 0)),
                  pl.BlockSpec((2, t + 1, SSM_STATE, LANES), lambda j: (0, 0, 0, j))],
        out_specs=[pl.BlockSpec((None, t * LANES + 2 * SLAB_STATE, t * LANES), lambda j: (j, 0, 0)),
                   pl.BlockSpec((None, t * LANES, 2 * SLAB_STATE), lambda j: (j, 0, 0))],
        out_shape=[jax.ShapeDtypeStruct((SSM_SLABS, t * LANES + 2 * SLAB_STATE, t * LANES), BF16),
                   jax.ShapeDtypeStruct((SSM_SLABS, t * LANES, 2 * SLAB_STATE), BF16)],
        compiler_params=pltpu.CompilerParams(dimension_semantics=("arbitrary",),
                                             vmem_limit_bytes=VMEM_LIMIT),
        name="s5_expand",
    )(xb, ca)
    a_t = jnp.stack([pw_re[t].reshape(-1), pw_im[t].reshape(-1)])
    return w_y, w_st, a_t


def _ssm_expand_kernel(xb_ref, ca_ref, wy_ref, wst_ref):
    t = SSM_CHUNK
    keep_st = (lax.broadcasted_iota(jnp.int32, (LANES, SLAB_STATE), 0) // SSM_GROUP
               == lax.broadcasted_iota(jnp.int32, (LANES, SLAB_STATE), 1) // SSM_STATE)
    keep_out = (lax.broadcasted_iota(jnp.int32, (SLAB_STATE, LANES), 0) // SSM_STATE
                == lax.broadcasted_iota(jnp.int32, (SLAB_STATE, LANES), 1) // SSM_GROUP)

    def to_state(d):
        parts = [jnp.where(keep_st, jnp.concatenate([xb_ref[ri, d]] * (SLAB_STATE // LANES), axis=1), 0.0)
                 for ri in range(2)]
        return jnp.concatenate(parts, axis=1)

    def from_state(d):
        parts = [jnp.where(keep_out, jnp.concatenate([ca_ref[ri, d]] * GROUPS_PER_SLAB, axis=0), 0.0)
                 for ri in range(2)]
        return jnp.concatenate(parts, axis=0)

    c0 = from_state(0)
    zero = jnp.zeros((LANES, LANES), BF16)
    for d in range(t):
        st = to_state(d)
        sg = t - 1 - d
        wst_ref[sg * LANES:(sg + 1) * LANES, :] = st.astype(BF16)
        taps = jnp.dot(st, c0, preferred_element_type=F32, precision=lax.Precision.HIGHEST).astype(BF16)
        for sg in range(t - d):
            wy_ref[sg * LANES:(sg + 1) * LANES, (sg + d) * LANES:(sg + d + 1) * LANES] = taps
        if d:
            for tau in range(t - d):
                wy_ref[(tau + d) * LANES:(tau + d + 1) * LANES, tau * LANES:(tau + 1) * LANES] = zero
    for tau in range(t):
        wy_ref[t * LANES:, tau * LANES:(tau + 1) * LANES] = from_state(tau + 1).astype(BF16)


def _ssm_kernel(x_ref, wu_ref, wz_ref, wy_ref, wst_ref, at_ref, dskip_ref, wglu_ref, bglu_ref,
                out_ref, u_st, z_scr, y_st, carry_scr, *state_scrs):
    s_scrs, xp_scrs = state_scrs[:SSM_SLABS], state_scrs[SSM_SLABS:]
    bsz, tl, d_model = x_ref.shape
    t = SSM_CHUNK
    ch = tl // t
    rows = bsz * tl
    crows = bsz * ch
    tiles_per_slab = SLAB_STATE // LANES
    n_tiles = SSM_SLABS * tiles_per_slab
    pitch = s_scrs[0].shape[1] // bsz

    @pl.when(pl.program_id(0) == 0)
    def _():
        carry_scr[...] = jnp.zeros_like(carry_scr)

    xb = x_ref[...].reshape(rows, d_model).astype(BF16)
    for c, tile in enumerate(_lane_tiles(jnp.dot(xb, wu_ref[...], preferred_element_type=F32))):
        u_st[c] = tile
    z_scr[...] = jnp.dot(xb, wz_ref[...], preferred_element_type=F32)

    def chunk_lhs(j):
        return jnp.concatenate([u_st[j, pl.ds(sg, crows, stride=t), :] for sg in range(t)], axis=1).astype(BF16)

    n_slab_tiles = 2 * tiles_per_slab

    def state_increments(j):
        st = jnp.dot(chunk_lhs(j), wst_ref[j], preferred_element_type=F32)
        for c, tile in enumerate(_lane_tiles(st)):
            for b in range(bsz):
                s_scrs[j][c, b * pitch:b * pitch + ch, :] = tile[b * ch:(b + 1) * ch]

    def recurrence(j):
        s_scr, xp_scr = s_scrs[j], xp_scrs[j]
        for c_r in range(tiles_per_slab):
            c_i = tiles_per_slab + c_r
            g_r, g_i = j * tiles_per_slab + c_r, n_tiles + j * tiles_per_slab + c_r
            a_re, a_im = at_ref[g_r:g_r + 1, :], at_ref[g_i:g_i + 1, :]
            x_re, x_im = carry_scr[g_r], carry_scr[g_i]
            for k in range(ch):
                sel = pl.ds(k, bsz, stride=pitch)
                xp_scr[c_r, sel, :] = x_re
                xp_scr[c_i, sel, :] = x_im
                x_re, x_im = (a_re * x_re - a_im * x_im + s_scr[c_r, sel, :],
                              a_re * x_im + a_im * x_re + s_scr[c_i, sel, :])
            carry_scr[g_r] = x_re
            carry_scr[g_i] = x_im

    def response(j):
        xin = [jnp.concatenate([xp_scrs[j][c, b * pitch:b * pitch + ch, :] for b in range(bsz)],
                               axis=0).astype(BF16) for c in range(n_slab_tiles)]
        yj = jnp.dot(jnp.concatenate([chunk_lhs(j)] + xin, axis=1), wy_ref[j], preferred_element_type=F32)
        for tau, tile in enumerate(_lane_tiles(yj)):
            y_st[j, pl.ds(tau, crows, stride=t), :] = tile

    state_increments(0)
    for j in range(SSM_SLABS):
        if j + 1 < SSM_SLABS:
            state_increments(j + 1)
        recurrence(j)
        if j:
            response(j - 1)
    response(SSM_SLABS - 1)

    for b in range(bsz):
        rs = slice(b * tl, (b + 1) * tl)
        y = jnp.concatenate([y_st[c, rs, :] for c in range(SSM_SLABS)], axis=1)
        u = jnp.concatenate([u_st[c, rs, :] for c in range(SSM_SLABS)], axis=1)
        y = jax.nn.gelu(y + dskip_ref[...] * u)
        gate = _sigmoid(jnp.dot(y.astype(BF16), wglu_ref[...], preferred_element_type=F32) + bglu_ref[...])
        out_ref[b] = (y * gate * _silu(z_scr[rs, :])).astype(BF16)


def _ssm_branch(x, w_in, ssm_w, d_skip, w_glu, b_glu, tl=128):
    bsz, seq, d = x.shape
    u_col = 3 * QKV_WIDTH + ATTN_WIDTH
    t = SSM_CHUNK
    rows, ch = bsz * tl, tl // t
    pitch = -(-ch // SUBLANES) * SUBLANES
    pitch += SUBLANES * (1 - (pitch // SUBLANES) % 2)
    w_y, w_st, a_t = ssm_w
    n_tiles2 = 2 * SSM_SLABS * SLAB_STATE // LANES
    a_t = a_t.reshape(n_tiles2, LANES)
    out = pl.pallas_call(
        _ssm_kernel,
        grid=(seq // tl,),
        in_specs=[pl.BlockSpec((bsz, tl, d), lambda i: (0, i, 0)),
                  _col_block_spec(d, SSM_WIDTH, u_col), _col_block_spec(d, SSM_WIDTH, u_col + SSM_WIDTH),
                  _const_spec(w_y.shape),
                  _const_spec(w_st.shape), _const_spec(a_t.shape), _const_spec((1, SSM_WIDTH)),
                  _const_spec(w_glu.shape), _const_spec((1, SSM_WIDTH))],
        out_specs=pl.BlockSpec((bsz, tl, SSM_WIDTH), lambda i: (0, i, 0)),
        out_shape=jax.ShapeDtypeStruct((bsz, seq, SSM_WIDTH), BF16),
        scratch_shapes=[pltpu.VMEM((SSM_SLABS, rows, LANES), F32),
                        pltpu.VMEM((rows, SSM_WIDTH), F32),
                        pltpu.VMEM((SSM_SLABS, rows, LANES), F32),
                        pltpu.VMEM((n_tiles2, bsz, LANES), F32)]
        + [pltpu.VMEM((n_tiles2 // SSM_SLABS, bsz * pitch, LANES), F32)] * (2 * SSM_SLABS),
        compiler_params=pltpu.CompilerParams(dimension_semantics=("arbitrary",),
                                             vmem_limit_bytes=VMEM_LIMIT),
        name="s5_branch",
    )(x, w_in, w_in, w_y, w_st, a_t, d_skip.reshape(1, SSM_WIDTH).astype(F32),
      w_glu, b_glu.reshape(1, SSM_WIDTH).astype(F32))
    return out.reshape(bsz * seq, SSM_WIDTH)


def _merge_kernel(alpha, x_ref, attn_ref, ys_ref, wz_ref, wg_ref, wa_ref, ws_ref, wo_ref, g_ref, b_ref, out_ref):
    x = x_ref[...]
    d_model = x.shape[1]
    xb = x.astype(BF16)
    z_a = jnp.dot(xb, wz_ref[...], preferred_element_type=F32)
    gates = jnp.dot(xb, wg_ref[...], preferred_element_type=F32)
    gate_a = gates[:, :d_model]
    gate_s = gates[:, d_model:]
    attn = attn_ref[...].astype(F32) * _silu(z_a)
    y_a = jnp.dot(attn.astype(BF16), wa_ref[...], preferred_element_type=F32)
    y_s = jnp.dot(ys_ref[...], ws_ref[...], preferred_element_type=F32)
    merged = _sigmoid(gate_a) * y_a + _sigmoid(gate_s) * y_s
    out = jnp.dot(merged.astype(BF16), wo_ref[...], preferred_element_type=F32)
    h = alpha * x + out
    mu = jnp.mean(h, axis=-1, keepdims=True)
    hc = h - mu
    var = jnp.mean(hc * hc, axis=-1, keepdims=True)
    out_ref[...] = hc * lax.rsqrt(var + LN_EPS) * g_ref[...] + b_ref[...]


def _merge(x2, attn, ys, w_in, w_attn_up, w_ssm_up, w_o, ln_g, ln_b, alpha, tm=1024):
    n, d = x2.shape
    row = lambda width: pl.BlockSpec((tm, width), lambda i: (i, 0))
    z_col = 3 * QKV_WIDTH
    gate_col = z_col + ATTN_WIDTH + 2 * SSM_WIDTH
    return pl.pallas_call(
        functools.partial(_merge_kernel, alpha),
        grid=(n // tm,),
        in_specs=[row(d), row(ATTN_WIDTH), row(SSM_WIDTH),
                  _col_block_spec(d, ATTN_WIDTH, z_col), _col_block_spec(d, 2 * d, gate_col),
                  _const_spec(w_attn_up.shape), _const_spec(w_ssm_up.shape),
                  _const_spec(w_o.shape), _const_spec((1, d)), _const_spec((1, d))],
        out_specs=row(d),
        out_shape=jax.ShapeDtypeStruct((n, d), F32),
        compiler_params=pltpu.CompilerParams(dimension_semantics=("arbitrary",),
                                             vmem_limit_bytes=VMEM_LIMIT),
        name="merge_out_ln",
    )(x2, attn, ys, w_in, w_in, w_attn_up, w_ssm_up, w_o,
      ln_g.reshape(1, d).astype(F32), ln_b.reshape(1, d).astype(F32))


def _layer(h, w_in, lam_re, lam_im, log_dt, b_re, b_im, c_re, c_im, d_skip,
           w_glu, b_glu, w_attn_up, w_ssm_up, w_o, ln_g, ln_b, alpha):
    bsz, seq, d = h.shape
    assert seq % SPAN == 0 and all(w // r == BLOCK for w, r in GROUP_PATTERNS)
    assert tuple(r for _, r in GROUP_PATTERNS) == (1, PHASE_SPLIT, PHASE_SPLIT * PHASE_SPLIT)
    x2 = h.reshape(bsz * seq, d)
    w_in = w_in.astype(BF16)

    qkv = _qkv_proj(x2, w_in, bsz, seq)
    attn = _attention(qkv, bsz, seq)
    ssm_w = _ssm_weights(lam_re, lam_im, log_dt, b_re, b_im, c_re, c_im)
    ys = _ssm_branch(h, w_in, ssm_w, d_skip, w_glu.astype(BF16), b_glu)
    out = _merge(x2, attn, ys, w_in, w_attn_up.astype(BF16), w_ssm_up.astype(BF16),
                 w_o.astype(BF16), ln_g, ln_b, alpha)
    return out.reshape(bsz, seq, d)


def kernel(x, w_in, lam_re, lam_im, log_dt, b_re, b_im, c_re, c_im, d_skip,
           w_glu, b_glu, w_attn_up, w_ssm_up, w_o, ln_g, ln_b):
    depth = w_in.shape[0]
    alpha = (2.0 * depth) ** 0.25
    h = x
    for layer in range(depth):
        h = _layer(h, w_in[layer], lam_re[layer], lam_im[layer], log_dt[layer], b_re[layer], b_im[layer],
                   c_re[layer], c_im[layer], d_skip[layer], w_glu[layer], b_glu[layer], w_attn_up[layer],
                   w_ssm_up[layer], w_o[layer], ln_g[layer], ln_b[layer], alpha)
    return h
```

```python
import functools
import math

import jax
import jax.numpy as jnp
import numpy as np
from jax import lax
from jax.experimental import pallas as pl
from jax.experimental.pallas import tpu as pltpu

F32 = jnp.float32
BF16 = jnp.bfloat16

HEAD_DIM = 64
N_SLOTS = 8
GROUP_PATTERNS = ((128, 1), (512, 4), (2048, 16))
N_GROUPS = len(GROUP_PATTERNS)
ATTN_WIDTH = N_SLOTS * HEAD_DIM
QKV_WIDTH = N_GROUPS * ATTN_WIDTH
BLOCK = 128
SPAN = GROUP_PATTERNS[-1][1] * BLOCK
UNITS = SPAN // BLOCK
SSM_WIDTH = 512
SSM_GROUP = 16
SSM_GROUPS = SSM_WIDTH // SSM_GROUP
SSM_STATE = 64
LN_EPS = 1e-5
MASK_VALUE = -1e30
LOG2E = math.log2(math.e)
Q_SCALE = LOG2E / math.sqrt(HEAD_DIM)

LANES = 128
SUBLANES = 8
HEAD_PAIRS = ATTN_WIDTH // LANES
PHASE_SPLIT = 4
QUARTER = SPAN // PHASE_SPLIT
BLOCK_ROW_TOKEN = [PHASE_SPLIT * k + ph for ph in range(PHASE_SPLIT) for k in range(BLOCK // PHASE_SPLIT)]
SSM_CHUNK = PHASE_SPLIT
SSM_SLABS = SSM_WIDTH // LANES
GROUPS_PER_SLAB = LANES // SSM_GROUP
SLAB_STATE = GROUPS_PER_SLAB * SSM_STATE
VMEM_LIMIT = 56 * 1024 * 1024


def _const_spec(shape):
    nd = len(shape)
    return pl.BlockSpec(shape, lambda *_: (0,) * nd, pipeline_mode=pl.Buffered(1))


def _col_block_spec(rows, width, start):
    assert start % width == 0
    return pl.BlockSpec((rows, width), lambda *_: (0, start // width), pipeline_mode=pl.Buffered(1))


def _sigmoid(x):
    return 0.5 * jnp.tanh(0.5 * x) + 0.5


def _silu(x):
    return x * _sigmoid(x)


def _lane_tiles(x):
    return [x[:, c * LANES:(c + 1) * LANES] for c in range(x.shape[1] // LANES)]


def _qkv_kernel(x_ref, w_ref, *refs):
    out_refs, x_stage, stage = refs[:-2], refs[-2], refs[-1]
    tm = x_ref.shape[0]
    sub = tm // PHASE_SPLIT
    x = x_ref[...]
    for c, tile in enumerate(_lane_tiles(x)):
        x_stage[c] = tile
    xb = {1: x.astype(BF16),
          PHASE_SPLIT: jnp.concatenate(
              [jnp.concatenate([x_stage[c, pl.ds(q, sub, stride=PHASE_SPLIT), :] for c in range(x_stage.shape[0])],
                               axis=1) for q in range(PHASE_SPLIT)], axis=0).astype(BF16)}
    piece = BLOCK // PHASE_SPLIT
    xb_block_split = jnp.concatenate(
        [jnp.concatenate([x_stage[c, pl.ds(blk * BLOCK + ph, piece, stride=PHASE_SPLIT), :]
                          for c in range(x_stage.shape[0])], axis=1)
         for blk in range(tm // BLOCK) for ph in range(PHASE_SPLIT)], axis=0).astype(BF16)
    for kind in range(3):
        col = kind * QKV_WIDTH
        first = jnp.dot(xb_block_split if kind == 0 else xb[1], w_ref[:, col:col + ATTN_WIDTH],
                        preferred_element_type=F32)
        rest = jnp.dot(xb[PHASE_SPLIT], w_ref[:, col + ATTN_WIDTH:col + QKV_WIDTH], preferred_element_type=F32)
        by_group = [first] + [rest[:, (g - 1) * ATTN_WIDTH:g * ATTN_WIDTH] for g in range(1, N_GROUPS)]
        for g, (_, r) in enumerate(GROUP_PATTERNS):
            res = by_group[g]
            if kind == 0:
                res = res * Q_SCALE
            o_ref = out_refs[kind * N_GROUPS + g]
            if r == 1:
                o_ref[0] = res.astype(BF16)
            elif r == PHASE_SPLIT:
                for q in range(r):
                    o_ref[q] = res[q * sub:(q + 1) * sub].astype(BF16)
            else:
                for c, tile in enumerate(_lane_tiles(res)):
                    stage[c] = tile
                for q in range(PHASE_SPLIT):
                    for s in range(PHASE_SPLIT):
                        o_ref[q + PHASE_SPLIT * s] = jnp.concatenate(
                            [stage[c, pl.ds(q * sub + s, sub // PHASE_SPLIT, stride=PHASE_SPLIT), :]
                             for c in range(HEAD_PAIRS)], axis=1).astype(BF16)


def _qkv_proj(x2, w_in, bsz, seq, tm=1024):
    n, d = x2.shape
    tiles = seq // tm
    out_specs, out_shape = [], []
    for _ in range(3):
        for _, r in GROUP_PATTERNS:
            out_specs.append(pl.BlockSpec((None, r, tm // r, ATTN_WIDTH),
                                          lambda i: (i // tiles, 0, i % tiles, 0)))
            out_shape.append(jax.ShapeDtypeStruct((bsz, r, seq // r, ATTN_WIDTH), BF16))
    return pl.pallas_call(
        _qkv_kernel,
        grid=(n // tm,),
        in_specs=[pl.BlockSpec((tm, d), lambda i: (i, 0)), _col_block_spec(d, 3 * QKV_WIDTH, 0)],
        out_specs=out_specs,
        out_shape=out_shape,
        scratch_shapes=[pltpu.VMEM((d // LANES, tm, LANES), F32), pltpu.VMEM((HEAD_PAIRS, tm, LANES), F32)],
        compiler_params=pltpu.CompilerParams(dimension_semantics=("arbitrary",),
                                             vmem_limit_bytes=VMEM_LIMIT),
        name="qkv_proj",
    )(x2, w_in)


def _attn_bias():
    slopes = 2.0 ** (-8.0 * np.arange(1, N_SLOTS + 1) / N_SLOTS)
    qi = np.arange(BLOCK)[:, None]
    kj = np.arange(2 * BLOCK)[None, :]
    dist = BLOCK + qi - kj
    valid = (dist >= 0) & (dist <= BLOCK)
    valid = np.stack([valid, valid & (kj >= BLOCK)])
    out = []
    for _, r in GROUP_PATTERNS:
        alibi = (-LOG2E * slopes)[:, None, None, None] * (dist * r)[None, None]
        bias = np.where(valid[None], alibi, MASK_VALUE)
        out.append(bias[:, :, BLOCK_ROW_TOKEN, :] if r == 1 else bias)
    return jnp.asarray(np.stack(out).astype(np.float32))


def _attn_kernel(*refs):
    ins, bias_ref, out_ref = refs[:5 * N_GROUPS], refs[5 * N_GROUPS], refs[5 * N_GROUPS + 1]
    o_stage, l_stage, m_stage, nat_stage = refs[5 * N_GROUPS + 2:]
    first_span = pl.program_id(2) == 0
    lane = lax.broadcasted_iota(jnp.int32, (BLOCK, LANES), 1)
    low = lane < HEAD_DIM

    for g, (_, r) in enumerate(GROUP_PATTERNS):
        q_ref, kc_ref, kp_ref, vc_ref, vp_ref = ins[5 * g:5 * g + 5]
        nb = UNITS // r

        for p, n in ((p, n) for p in range(r) for n in range(nb)):
            q = q_ref[p, n * BLOCK:(n + 1) * BLOCK, :]
            if n == 0:
                kcat = jnp.concatenate([kp_ref[p], kc_ref[p, :BLOCK, :]], axis=0)
                vcat = jnp.concatenate([vp_ref[p], vc_ref[p, :BLOCK, :]], axis=0)
                variant = jnp.where(first_span, 1, 0)
            else:
                kcat = kc_ref[p, (n - 1) * BLOCK:(n + 1) * BLOCK, :]
                vcat = vc_ref[p, (n - 1) * BLOCK:(n + 1) * BLOCK, :]
                variant = 0
            v_ext = jnp.concatenate([vcat, jnp.ones_like(vcat)], axis=1)
            qm = jnp.concatenate([jnp.where(low, q, jnp.zeros_like(q)),
                                  jnp.where(low, jnp.zeros_like(q), q)], axis=0)
            s_both = lax.dot_general(qm, kcat, (((1,), (1,)), ((), ())), preferred_element_type=F32)
            es, tops = [], []
            for hh in range(2):
                s = s_both[hh * BLOCK:(hh + 1) * BLOCK] + bias_ref[g, hh, variant]
                m = jnp.max(s, axis=1, keepdims=True)
                es.append(jnp.exp2(s - m).astype(BF16))
                tops.append(m)
            pv = jnp.dot(jnp.concatenate(es, axis=0), v_ext, preferred_element_type=F32)
            staged = ((o_stage, jnp.where(low, pv[:BLOCK, :LANES], pv[BLOCK:, :LANES])),
                      (l_stage, jnp.where(low, pv[:BLOCK, LANES:], pv[BLOCK:, LANES:])),
                      (m_stage, jnp.where(low, tops[0], tops[1])))
            for stage, val in staged:
                if r == 1:
                    piece = BLOCK // PHASE_SPLIT
                    for ph in range(PHASE_SPLIT):
                        dst = ph * QUARTER + n * piece
                        stage[g, dst:dst + piece, :] = val[ph * piece:(ph + 1) * piece]
                elif r == PHASE_SPLIT:
                    dst = p * QUARTER + n * BLOCK
                    stage[g, dst:dst + BLOCK, :] = val
                else:
                    start = (p % PHASE_SPLIT) * QUARTER + p // PHASE_SPLIT
                    stage[g, pl.ds(start, BLOCK, stride=PHASE_SPLIT), :] = val

    def mix(i, carry):
        rows = pl.ds(pl.multiple_of(i * BLOCK, BLOCK), BLOCK)

        tops = [m_stage[g, rows, :] for g in range(N_GROUPS)]
        top = functools.reduce(jnp.maximum, tops)
        weights = [jnp.exp2(tops[g] - top) for g in range(N_GROUPS)]
        num = sum(weights[g] * o_stage[g, rows, :] for g in range(N_GROUPS))
        den = sum(weights[g] * l_stage[g, rows, :] for g in range(N_GROUPS))
        start = (i % PHASE_SPLIT) * (PHASE_SPLIT * BLOCK) + i // PHASE_SPLIT
        nat_stage[pl.ds(start, BLOCK, stride=PHASE_SPLIT), :] = num / den
        return carry

    lax.fori_loop(0, UNITS, mix, 0, unroll=4)
    out_ref[...] = nat_stage[...].astype(BF16)


def _attention(qkv, bsz, seq):
    spans = seq // SPAN
    in_specs, args = [], []
    for g, (_, r) in enumerate(GROUP_PATTERNS):
        nb = UNITS // r
        cur = pl.BlockSpec((None, r, SPAN // r, LANES), lambda hp, b, s: (b, 0, s, hp))
        prev = pl.BlockSpec((None, r, BLOCK, LANES),
                            lambda hp, b, s, nb=nb: (b, 0, jnp.maximum(nb * s - 1, 0), hp))
        q, k, v = qkv[g], qkv[N_GROUPS + g], qkv[2 * N_GROUPS + g]
        in_specs += [cur, cur, prev, cur, prev]
        args += [q, k, k, v, v]
    in_specs.append(pl.BlockSpec((N_GROUPS, 2, 2, BLOCK, 2 * BLOCK), lambda hp, b, s: (0, hp, 0, 0, 0)))
    scratch = [pltpu.VMEM((N_GROUPS, SPAN, LANES), F32)] * 3
    scratch += [pltpu.VMEM((SPAN, LANES), F32)]
    return pl.pallas_call(
        _attn_kernel,
        grid=(HEAD_PAIRS, bsz, spans),
        in_specs=in_specs,
        out_specs=pl.BlockSpec((SPAN, LANES), lambda hp, b, s: (b * spans + s, hp)),
        out_shape=jax.ShapeDtypeStruct((bsz * seq, ATTN_WIDTH), BF16),
        scratch_shapes=scratch,
        compiler_params=pltpu.CompilerParams(dimension_semantics=("arbitrary",) * 3,
                                             vmem_limit_bytes=VMEM_LIMIT),
        name="band_attn",
    )(*args, _attn_bias())


def _ssm_weights(lam_re, lam_im, log_dt, b_re, b_im, c_re, c_im):
    t = SSM_CHUNK
    lr, li = lam_re.astype(F32), lam_im.astype(F32)
    dt = jnp.exp(log_dt.astype(F32))[:, None]
    mag = jnp.exp(lr * dt)
    ab_re, ab_im = mag * jnp.cos(li * dt), mag * jnp.sin(li * dt)
    nr, ni = ab_re - 1.0, ab_im
    den = lr * lr + li * li
    coef_re = (nr * lr + ni * li) / den
    coef_im = (ni * lr - nr * li) / den
    br, bi = b_re.astype(F32), b_im.astype(F32)
    bb_re = coef_re[..., None] * br - coef_im[..., None] * bi
    bb_im = coef_re[..., None] * bi + coef_im[..., None] * br
    pw_re, pw_im = [jnp.ones_like(ab_re)], [jnp.zeros_like(ab_im)]
    for _ in range(t):
        pr, pi = pw_re[-1], pw_im[-1]
        pw_re.append(pr * ab_re - pi * ab_im)
        pw_im.append(pr * ab_im + pi * ab_re)
    pw_re, pw_im = jnp.stack(pw_re), jnp.stack(pw_im)
    cr, ci = c_re.astype(F32), c_im.astype(F32)
    n_ch = SSM_GROUPS * SSM_GROUP

    xb_re = pw_re[:t, :, :, None] * bb_re[None] - pw_im[:t, :, :, None] * bb_im[None]
    xb_im = pw_re[:t, :, :, None] * bb_im[None] + pw_im[:t, :, :, None] * bb_re[None]
    xb = jnp.stack([xb_re, xb_im]).transpose(0, 1, 2, 4, 3).reshape(2, t, n_ch, SSM_STATE)
    xb = jnp.concatenate([xb, xb], axis=-1)
    ca_re = cr[None] * pw_re[:, :, None, :] - ci[None] * pw_im[:, :, None, :]
    ca_im = cr[None] * pw_im[:, :, None, :] + ci[None] * pw_re[:, :, None, :]
    ca = jnp.stack([ca_re, -ca_im]).transpose(0, 1, 4, 2, 3).reshape(2, t + 1, SSM_STATE, n_ch)

    w_y, w_st = pl.pallas_call(
        _ssm_expand_kernel,
        grid=(SSM_SLABS,),
        in_specs=[pl.BlockSpec((2, t, LANES, LANES), lambda j: (0, 0, j, 0)),
                  pl.BlockSpec((2, t + 1, SSM_STATE, LANES), lambda j: (0, 0, 0, j))],
        out_specs=[pl.BlockSpec((None, t * LANES + 2 * SLAB_STATE, t * LANES), lambda j: (j, 0, 0)),
                   pl.BlockSpec((None, t * LANES, 2 * SLAB_STATE), lambda j: (j, 0, 0))],
        out_shape=[jax.ShapeDtypeStruct((SSM_SLABS, t * LANES + 2 * SLAB_STATE, t * LANES), BF16),
                   jax.ShapeDtypeStruct((SSM_SLABS, t * LANES, 2 * SLAB_STATE), BF16)],
        compiler_params=pltpu.CompilerParams(dimension_semantics=("arbitrary",),
                                             vmem_limit_bytes=VMEM_LIMIT),
        name="s5_expand",
    )(xb, ca)
    a_t = jnp.stack([pw_re[t].reshape(-1), pw_im[t].reshape(-1)])
    return w_y, w_st, a_t


def _ssm_expand_kernel(xb_ref, ca_ref, wy_ref, wst_ref):
    t = SSM_CHUNK
    keep_st = (lax.broadcasted_iota(jnp.int32, (LANES, SLAB_STATE), 0) // SSM_GROUP
               == lax.broadcasted_iota(jnp.int32, (LANES, SLAB_STATE), 1) // SSM_STATE)
    keep_out = (lax.broadcasted_iota(jnp.int32, (SLAB_STATE, LANES), 0) // SSM_STATE
                == lax.broadcasted_iota(jnp.int32, (SLAB_STATE, LANES), 1) // SSM_GROUP)

    def to_state(d):
        parts = [jnp.where(keep_st, jnp.concatenate([xb_ref[ri, d]] * (SLAB_STATE // LANES), axis=1), 0.0)
                 for ri in range(2)]
        return jnp.concatenate(parts, axis=1)

    def from_state(d):
        parts = [jnp.where(keep_out, jnp.concatenate([ca_ref[ri, d]] * GROUPS_PER_SLAB, axis=0), 0.0)
                 for ri in range(2)]
        return jnp.concatenate(parts, axis=0)

    c0 = from_state(0)
    zero = jnp.zeros((LANES, LANES), BF16)
    for d in range(t):
        st = to_state(d)
        sg = t - 1 - d
        wst_ref[sg * LANES:(sg + 1) * LANES, :] = st.astype(BF16)
        taps = jnp.dot(st, c0, preferred_element_type=F32, precision=lax.Precision.HIGHEST).astype(BF16)
        for sg in range(t - d):
            wy_ref[sg * LANES:(sg + 1) * LANES, (sg + d) * LANES:(sg + d + 1) * LANES] = taps
        if d:
            for tau in range(t - d):
                wy_ref[(tau + d) * LANES:(tau + d + 1) * LANES, tau * LANES:(tau + 1) * LANES] = zero
    for tau in range(t):
        wy_ref[t * LANES:, tau * LANES:(tau + 1) * LANES] = from_state(tau + 1).astype(BF16)


def _ssm_kernel(x_ref, wuz_ref, wy_ref, wst_ref, at_ref, dskip_ref, wglu_ref, bglu_ref,
                out_ref, u_st, z_scr, y_st, carry_scr, *state_scrs):
    s_scrs, xp_scrs = state_scrs[:SSM_SLABS], state_scrs[SSM_SLABS:]
    bsz, tl, d_model = x_ref.shape
    t = SSM_CHUNK
    ch = tl // t
    rows = bsz * tl
    crows = bsz * ch
    tiles_per_slab = SLAB_STATE // LANES
    n_tiles = SSM_SLABS * tiles_per_slab
    pitch = s_scrs[0].shape[1] // bsz

    @pl.when(pl.program_id(0) == 0)
    def _():
        carry_scr[...] = jnp.zeros_like(carry_scr)

    xb = x_ref[...].reshape(rows, d_model).astype(BF16)
    uz = jnp.dot(xb, wuz_ref[...], preferred_element_type=F32)
    for c, tile in enumerate(_lane_tiles(uz[:, :SSM_WIDTH])):
        u_st[c] = tile
    z_scr[...] = uz[:, SSM_WIDTH:]

    def chunk_lhs(j):
        return jnp.concatenate([u_st[j, pl.ds(sg, crows, stride=t), :] for sg in range(t)], axis=1).astype(BF16)

    n_slab_tiles = 2 * tiles_per_slab

    def state_increments(j):
        st = jnp.dot(chunk_lhs(j), wst_ref[j], preferred_element_type=F32)
        for c, tile in enumerate(_lane_tiles(st)):
            for b in range(bsz):
                s_scrs[j][c, b * pitch:b * pitch + ch, :] = tile[b * ch:(b + 1) * ch]

    def recurrence(j):
        s_scr, xp_scr = s_scrs[j], xp_scrs[j]
        for c_r in range(tiles_per_slab):
            c_i = tiles_per_slab + c_r
            g_r, g_i = j * tiles_per_slab + c_r, n_tiles + j * tiles_per_slab + c_r
            a_re, a_im = at_ref[g_r:g_r + 1, :], at_ref[g_i:g_i + 1, :]
            x_re, x_im = carry_scr[g_r], carry_scr[g_i]
            for k in range(ch):
                sel = pl.ds(k, bsz, stride=pitch)
                xp_scr[c_r, sel, :] = x_re
                xp_scr[c_i, sel, :] = x_im
                x_re, x_im = (a_re * x_re - a_im * x_im + s_scr[c_r, sel, :],
                              a_re * x_im + a_im * x_re + s_scr[c_i, sel, :])
            carry_scr[g_r] = x_re
            carry_scr[g_i] = x_im

    def response(j):
        xin = [jnp.concatenate([xp_scrs[j][c, b * pitch:b * pitch + ch, :] for b in range(bsz)],
                               axis=0).astype(BF16) for c in range(n_slab_tiles)]
        yj = jnp.dot(jnp.concatenate([chunk_lhs(j)] + xin, axis=1), wy_ref[j], preferred_element_type=F32)
        for tau, tile in enumerate(_lane_tiles(yj)):
            y_st[j, pl.ds(tau, crows, stride=t), :] = tile

    state_increments(0)
    for j in range(SSM_SLABS):
        if j + 1 < SSM_SLABS:
            state_increments(j + 1)
        recurrence(j)
        if j:
            response(j - 1)
    response(SSM_SLABS - 1)

    for b in range(bsz):
        rs = slice(b * tl, (b + 1) * tl)
        y = jnp.concatenate([y_st[c, rs, :] for c in range(SSM_SLABS)], axis=1)
        u = jnp.concatenate([u_st[c, rs, :] for c in range(SSM_SLABS)], axis=1)
        y = jax.nn.gelu(y + dskip_ref[...] * u)
        gate = _sigmoid(jnp.dot(y.astype(BF16), wglu_ref[...], preferred_element_type=F32) + bglu_ref[...])
        out_ref[b] = (y * gate * _silu(z_scr[rs, :])).astype(BF16)


def _ssm_branch(x, w_in, ssm_w, d_skip, w_glu, b_glu, tl=128):
    bsz, seq, d = x.shape
    u_col = 3 * QKV_WIDTH + ATTN_WIDTH
    t = SSM_CHUNK
    rows, ch = bsz * tl, tl // t
    pitch = -(-ch // SUBLANES) * SUBLANES
    pitch += SUBLANES * (1 - (pitch // SUBLANES) % 2)
    w_y, w_st, a_t = ssm_w
    n_tiles2 = 2 * SSM_SLABS * SLAB_STATE // LANES
    a_t = a_t.reshape(n_tiles2, LANES)
    out = pl.pallas_call(
        _ssm_kernel,
        grid=(seq // tl,),
        in_specs=[pl.BlockSpec((bsz, tl, d), lambda i: (0, i, 0)),
                  _col_block_spec(d, 2 * SSM_WIDTH, u_col),
                  _const_spec(w_y.shape),
                  _const_spec(w_st.shape), _const_spec(a_t.shape), _const_spec((1, SSM_WIDTH)),
                  _const_spec(w_glu.shape), _const_spec((1, SSM_WIDTH))],
        out_specs=pl.BlockSpec((bsz, tl, SSM_WIDTH), lambda i: (0, i, 0)),
        out_shape=jax.ShapeDtypeStruct((bsz, seq, SSM_WIDTH), BF16),
        scratch_shapes=[pltpu.VMEM((SSM_SLABS, rows, LANES), F32),
                        pltpu.VMEM((rows, SSM_WIDTH), F32),
                        pltpu.VMEM((SSM_SLABS, rows, LANES), F32),
                        pltpu.VMEM((n_tiles2, bsz, LANES), F32)]
        + [pltpu.VMEM((n_tiles2 // SSM_SLABS, bsz * pitch, LANES), F32)] * (2 * SSM_SLABS),
        compiler_params=pltpu.CompilerParams(dimension_semantics=("arbitrary",),
                                             vmem_limit_bytes=VMEM_LIMIT),
        name="s5_branch",
    )(x, w_in, w_y, w_st, a_t, d_skip.reshape(1, SSM_WIDTH).astype(F32),
      w_glu, b_glu.reshape(1, SSM_WIDTH).astype(F32))
    return out.reshape(bsz * seq, SSM_WIDTH)


def _merge_kernel(alpha, x_ref, attn_ref, ys_ref, wz_ref, wg_ref, wa_ref, ws_ref, wo_ref, g_ref, b_ref, out_ref):
    x = x_ref[...]
    d_model = x.shape[1]
    xb = x.astype(BF16)
    z_a = jnp.dot(xb, wz_ref[...], preferred_element_type=F32)
    gates = jnp.dot(xb, wg_ref[...], preferred_element_type=F32)
    gate_a = gates[:, :d_model]
    gate_s = gates[:, d_model:]
    attn = attn_ref[...].astype(F32) * _silu(z_a)
    y_a = jnp.dot(attn.astype(BF16), wa_ref[...], preferred_element_type=F32)
    y_s = jnp.dot(ys_ref[...], ws_ref[...], preferred_element_type=F32)
    merged = _sigmoid(gate_a) * y_a + _sigmoid(gate_s) * y_s
    out = jnp.dot(merged.astype(BF16), wo_ref[...], preferred_element_type=F32)
    h = alpha * x + out
    mu = jnp.mean(h, axis=-1, keepdims=True)
    hc = h - mu
    var = jnp.mean(hc * hc, axis=-1, keepdims=True)
    out_ref[...] = hc * lax.rsqrt(var + LN_EPS) * g_ref[...] + b_ref[...]


def _merge(x2, attn, ys, w_in, w_attn_up, w_ssm_up, w_o, ln_g, ln_b, alpha, tm=1024):
    n, d = x2.shape
    row = lambda width: pl.BlockSpec((tm, width), lambda i: (i, 0))
    z_col = 3 * QKV_WIDTH
    gate_col = z_col + ATTN_WIDTH + 2 * SSM_WIDTH
    return pl.pallas_call(
        functools.partial(_merge_kernel, alpha),
        grid=(n // tm,),
        in_specs=[row(d), row(ATTN_WIDTH), row(SSM_WIDTH),
                  _col_block_spec(d, ATTN_WIDTH, z_col), _col_block_spec(d, 2 * d, gate_col),
                  _const_spec(w_attn_up.shape), _const_spec(w_ssm_up.shape),
                  _const_spec(w_o.shape), _const_spec((1, d)), _const_spec((1, d))],
        out_specs=row(d),
        out_shape=jax.ShapeDtypeStruct((n, d), F32),
        compiler_params=pltpu.CompilerParams(dimension_semantics=("arbitrary",),
                                             vmem_limit_bytes=VMEM_LIMIT),
        name="merge_out_ln",
    )(x2, attn, ys, w_in, w_in, w_attn_up, w_ssm_up, w_o,
      ln_g.reshape(1, d).astype(F32), ln_b.reshape(1, d).astype(F32))


def _layer(h, w_in, lam_re, lam_im, log_dt, b_re, b_im, c_re, c_im, d_skip,
           w_glu, b_glu, w_attn_up, w_ssm_up, w_o, ln_g, ln_b, alpha):
    bsz, seq, d = h.shape
    assert seq % SPAN == 0 and all(w // r == BLOCK for w, r in GROUP_PATTERNS)
    assert tuple(r for _, r in GROUP_PATTERNS) == (1, PHASE_SPLIT, PHASE_SPLIT * PHASE_SPLIT)
    x2 = h.reshape(bsz * seq, d)
    w_in = w_in.astype(BF16)

    qkv = _qkv_proj(x2, w_in, bsz, seq)
    attn = _attention(qkv, bsz, seq)
    ssm_w = _ssm_weights(lam_re, lam_im, log_dt, b_re, b_im, c_re, c_im)
    ys = _ssm_branch(h, w_in, ssm_w, d_skip, w_glu.astype(BF16), b_glu)
    out = _merge(x2, attn, ys, w_in, w_attn_up.astype(BF16), w_ssm_up.astype(BF16),
                 w_o.astype(BF16), ln_g, ln_b, alpha)
    return out.reshape(bsz, seq, d)


def kernel(x, w_in, lam_re, lam_im, log_dt, b_re, b_im, c_re, c_im, d_skip,
           w_glu, b_glu, w_attn_up, w_ssm_up, w_o, ln_g, ln_b):
    depth = w_in.shape[0]
    alpha = (2.0 * depth) ** 0.25
    h = x
    for layer in range(depth):
        h = _layer(h, w_in[layer], lam_re[layer], lam_im[layer], log_dt[layer], b_re[layer], b_im[layer],
                   c_re[layer], c_im[layer], d_skip[layer], w_glu[layer], b_glu[layer], w_attn_up[layer],
                   w_ssm_up[layer], w_o[layer], ln_g[layer], ln_b[layer], alpha)
    return h
```

```python
import functools
import math

import jax
import jax.numpy as jnp
import numpy as np
from jax import lax
from jax.experimental import pallas as pl
from jax.experimental.pallas import tpu as pltpu

F32 = jnp.float32
BF16 = jnp.bfloat16

HEAD_DIM = 64
N_SLOTS = 8
GROUP_PATTERNS = ((128, 1), (512, 4), (2048, 16))
N_GROUPS = len(GROUP_PATTERNS)
ATTN_WIDTH = N_SLOTS * HEAD_DIM
QKV_WIDTH = N_GROUPS * ATTN_WIDTH
BLOCK = 128
SPAN = GROUP_PATTERNS[-1][1] * BLOCK
UNITS = SPAN // BLOCK
SSM_WIDTH = 512
SSM_GROUP = 16
SSM_GROUPS = SSM_WIDTH // SSM_GROUP
SSM_STATE = 64
LN_EPS = 1e-5
MASK_VALUE = -1e30
LOG2E = math.log2(math.e)
Q_SCALE = LOG2E / math.sqrt(HEAD_DIM)

LANES = 128
SUBLANES = 8
HEAD_PAIRS = ATTN_WIDTH // LANES
PHASE_SPLIT = 4
QUARTER = SPAN // PHASE_SPLIT
BLOCK_ROW_TOKEN = [PHASE_SPLIT * k + ph for ph in range(PHASE_SPLIT) for k in range(BLOCK // PHASE_SPLIT)]
SSM_CHUNK = PHASE_SPLIT
SSM_SLABS = SSM_WIDTH // LANES
GROUPS_PER_SLAB = LANES // SSM_GROUP
SLAB_STATE = GROUPS_PER_SLAB * SSM_STATE
VMEM_LIMIT = 56 * 1024 * 1024


def _const_spec(shape):
    nd = len(shape)
    return pl.BlockSpec(shape, lambda *_: (0,) * nd, pipeline_mode=pl.Buffered(1))


def _col_block_spec(rows, width, start):
    assert start % width == 0
    return pl.BlockSpec((rows, width), lambda *_: (0, start // width), pipeline_mode=pl.Buffered(1))


def _sigmoid(x):
    return 0.5 * jnp.tanh(0.5 * x) + 0.5


def _silu(x):
    return x * _sigmoid(x)


def _lane_tiles(x):
    return [x[:, c * LANES:(c + 1) * LANES] for c in range(x.shape[1] // LANES)]


def _qkv_kernel(x_ref, w_ref, *refs):
    out_refs, x_stage, stage = refs[:-2], refs[-2], refs[-1]
    tm = x_ref.shape[0]
    sub = tm // PHASE_SPLIT
    x = x_ref[...]
    for c, tile in enumerate(_lane_tiles(x)):
        x_stage[c] = tile
    xb = {1: x.astype(BF16),
          PHASE_SPLIT: jnp.concatenate(
              [jnp.concatenate([x_stage[c, pl.ds(q, sub, stride=PHASE_SPLIT), :] for c in range(x_stage.shape[0])],
                               axis=1) for q in range(PHASE_SPLIT)], axis=0).astype(BF16)}
    piece = BLOCK // PHASE_SPLIT
    xb_block_split = jnp.concatenate(
        [jnp.concatenate([x_stage[c, pl.ds(blk * BLOCK + ph, piece, stride=PHASE_SPLIT), :]
                          for c in range(x_stage.shape[0])], axis=1)
         for blk in range(tm // BLOCK) for ph in range(PHASE_SPLIT)], axis=0).astype(BF16)
    for kind in range(3):
        col = kind * QKV_WIDTH
        first = jnp.dot(xb_block_split if kind == 0 else xb[1], w_ref[:, col:col + ATTN_WIDTH],
                        preferred_element_type=F32)
        rest = jnp.dot(xb[PHASE_SPLIT], w_ref[:, col + ATTN_WIDTH:col + QKV_WIDTH], preferred_element_type=F32)
        by_group = [first] + [rest[:, (g - 1) * ATTN_WIDTH:g * ATTN_WIDTH] for g in range(1, N_GROUPS)]
        for g, (_, r) in enumerate(GROUP_PATTERNS):
            res = by_group[g]
            if kind == 0:
                res = res * Q_SCALE
            o_ref = out_refs[kind * N_GROUPS + g]
            if r == 1:
                o_ref[0] = res.astype(BF16)
            elif r == PHASE_SPLIT:
                for q in range(r):
                    o_ref[q] = res[q * sub:(q + 1) * sub].astype(BF16)
            else:
                for c, tile in enumerate(_lane_tiles(res)):
                    stage[c] = tile
                for q in range(PHASE_SPLIT):
                    for s in range(PHASE_SPLIT):
                        o_ref[q + PHASE_SPLIT * s] = jnp.concatenate(
                            [stage[c, pl.ds(q * sub + s, sub // PHASE_SPLIT, stride=PHASE_SPLIT), :]
                             for c in range(HEAD_PAIRS)], axis=1).astype(BF16)


def _qkv_proj(x2, w_in, bsz, seq, tm=1024):
    n, d = x2.shape
    tiles = seq // tm
    out_specs, out_shape = [], []
    for _ in range(3):
        for _, r in GROUP_PATTERNS:
            out_specs.append(pl.BlockSpec((None, r, tm // r, ATTN_WIDTH),
                                          lambda i: (i // tiles, 0, i % tiles, 0)))
            out_shape.append(jax.ShapeDtypeStruct((bsz, r, seq // r, ATTN_WIDTH), BF16))
    return pl.pallas_call(
        _qkv_kernel,
        grid=(n // tm,),
        in_specs=[pl.BlockSpec((tm, d), lambda i: (i, 0)), _col_block_spec(d, 3 * QKV_WIDTH, 0)],
        out_specs=out_specs,
        out_shape=out_shape,
        scratch_shapes=[pltpu.VMEM((d // LANES, tm, LANES), F32), pltpu.VMEM((HEAD_PAIRS, tm, LANES), F32)],
        compiler_params=pltpu.CompilerParams(dimension_semantics=("arbitrary",),
                                             vmem_limit_bytes=VMEM_LIMIT),
        name="qkv_proj",
    )(x2, w_in)


def _attn_bias():
    slopes = 2.0 ** (-8.0 * np.arange(1, N_SLOTS + 1) / N_SLOTS)
    qi = np.arange(BLOCK)[:, None]
    kj = np.arange(2 * BLOCK)[None, :]
    dist = BLOCK + qi - kj
    valid = (dist >= 0) & (dist <= BLOCK)
    valid = np.stack([valid, valid & (kj >= BLOCK)])
    out = []
    for _, r in GROUP_PATTERNS:
        alibi = (-LOG2E * slopes)[:, None, None, None] * (dist * r)[None, None]
        bias = np.where(valid[None], alibi, MASK_VALUE)
        out.append(bias[:, :, BLOCK_ROW_TOKEN, :] if r == 1 else bias)
    return jnp.asarray(np.stack(out).astype(np.float32))


def _attn_kernel(*refs):
    ins, bias_ref, out_ref = refs[:5 * N_GROUPS], refs[5 * N_GROUPS], refs[5 * N_GROUPS + 1]
    o_stage, l_stage, m_stage, nat_stage = refs[5 * N_GROUPS + 2:]
    first_span = pl.program_id(2) == 0
    lane = lax.broadcasted_iota(jnp.int32, (BLOCK, LANES), 1)
    low = lane < HEAD_DIM

    for g, (_, r) in enumerate(GROUP_PATTERNS):
        q_ref, kc_ref, kp_ref, vc_ref, vp_ref = ins[5 * g:5 * g + 5]
        nb = UNITS // r

        for p, n in ((p, n) for p in range(r) for n in range(nb)):
            q = q_ref[p, n * BLOCK:(n + 1) * BLOCK, :]
            if n == 0:
                kcat = jnp.concatenate([kp_ref[p], kc_ref[p, :BLOCK, :]], axis=0)
                vcat = jnp.concatenate([vp_ref[p], vc_ref[p, :BLOCK, :]], axis=0)
                variant = jnp.where(first_span, 1, 0)
            else:
                kcat = kc_ref[p, (n - 1) * BLOCK:(n + 1) * BLOCK, :]
                vcat = vc_ref[p, (n - 1) * BLOCK:(n + 1) * BLOCK, :]
                variant = 0
            v_ext = jnp.concatenate([vcat, jnp.ones_like(vcat)], axis=1)
            qm = jnp.concatenate([jnp.where(low, q, jnp.zeros_like(q)),
                                  jnp.where(low, jnp.zeros_like(q), q)], axis=0)
            s_both = lax.dot_general(qm, kcat, (((1,), (1,)), ((), ())), preferred_element_type=F32)
            es, tops = [], []
            for hh in range(2):
                s = s_both[hh * BLOCK:(hh + 1) * BLOCK] + bias_ref[g, hh, variant]
                m = jnp.max(s, axis=1, keepdims=True)
                es.append(jnp.exp2(s - m).astype(BF16))
                tops.append(m)
            pv = jnp.dot(jnp.concatenate(es, axis=0), v_ext, preferred_element_type=F32)
            staged = ((o_stage, jnp.where(low, pv[:BLOCK, :LANES], pv[BLOCK:, :LANES])),
                      (l_stage, jnp.where(low, pv[:BLOCK, LANES:], pv[BLOCK:, LANES:])),
                      (m_stage, jnp.where(low, tops[0], tops[1])))
            for stage, val in staged:
                if r == 1:
                    piece = BLOCK // PHASE_SPLIT
                    for ph in range(PHASE_SPLIT):
                        dst = ph * QUARTER + n * piece
                        stage[g, dst:dst + piece, :] = val[ph * piece:(ph + 1) * piece]
                elif r == PHASE_SPLIT:
                    dst = p * QUARTER + n * BLOCK
                    stage[g, dst:dst + BLOCK, :] = val
                else:
                    start = (p % PHASE_SPLIT) * QUARTER + p // PHASE_SPLIT
                    stage[g, pl.ds(start, BLOCK, stride=PHASE_SPLIT), :] = val

    def mix(i, carry):
        rows = pl.ds(pl.multiple_of(i * BLOCK, BLOCK), BLOCK)

        tops = [m_stage[g, rows, :] for g in range(N_GROUPS)]
        top = functools.reduce(jnp.maximum, tops)
        weights = [jnp.exp2(tops[g] - top) for g in range(N_GROUPS)]
        num = sum(weights[g] * o_stage[g, rows, :] for g in range(N_GROUPS))
        den = sum(weights[g] * l_stage[g, rows, :] for g in range(N_GROUPS))
        start = (i % PHASE_SPLIT) * (PHASE_SPLIT * BLOCK) + i // PHASE_SPLIT
        nat_stage[pl.ds(start, BLOCK, stride=PHASE_SPLIT), :] = num / den
        return carry

    lax.fori_loop(0, UNITS, mix, 0, unroll=4)
    out_ref[...] = nat_stage[...].astype(BF16)


def _attention(qkv, bsz, seq):
    spans = seq // SPAN
    in_specs, args = [], []
    for g, (_, r) in enumerate(GROUP_PATTERNS):
        nb = UNITS // r
        cur = pl.BlockSpec((None, r, SPAN // r, LANES), lambda hp, b, s: (b, 0, s, hp))
        prev = pl.BlockSpec((None, r, BLOCK, LANES),
                            lambda hp, b, s, nb=nb: (b, 0, jnp.maximum(nb * s - 1, 0), hp))
        q, k, v = qkv[g], qkv[N_GROUPS + g], qkv[2 * N_GROUPS + g]
        in_specs += [cur, cur, prev, cur, prev]
        args += [q, k, k, v, v]
    in_specs.append(pl.BlockSpec((N_GROUPS, 2, 2, BLOCK, 2 * BLOCK), lambda hp, b, s: (0, hp, 0, 0, 0)))
    scratch = [pltpu.VMEM((N_GROUPS, SPAN, LANES), F32)] * 3
    scratch += [pltpu.VMEM((SPAN, LANES), F32)]
    return pl.pallas_call(
        _attn_kernel,
        grid=(HEAD_PAIRS, bsz, spans),
        in_specs=in_specs,
        out_specs=pl.BlockSpec((SPAN, LANES), lambda hp, b, s: (b * spans + s, hp)),
        out_shape=jax.ShapeDtypeStruct((bsz * seq, ATTN_WIDTH), BF16),
        scratch_shapes=scratch,
        compiler_params=pltpu.CompilerParams(dimension_semantics=("arbitrary",) * 3,
                                             vmem_limit_bytes=VMEM_LIMIT),
        name="band_attn",
    )(*args, _attn_bias())


def _ssm_weights(lam_re, lam_im, log_dt, b_re, b_im, c_re, c_im):
    t = SSM_CHUNK
    lr, li = lam_re.astype(F32), lam_im.astype(F32)
    dt = jnp.exp(log_dt.astype(F32))[:, None]
    mag = jnp.exp(lr * dt)
    ab_re, ab_im = mag * jnp.cos(li * dt), mag * jnp.sin(li * dt)
    nr, ni = ab_re - 1.0, ab_im
    den = lr * lr + li * li
    coef_re = (nr * lr + ni * li) / den
    coef_im = (ni * lr - nr * li) / den
    br, bi = b_re.astype(F32), b_im.astype(F32)
    bb_re = coef_re[..., None] * br - coef_im[..., None] * bi
    bb_im = coef_re[..., None] * bi + coef_im[..., None] * br
    pw_re, pw_im = [jnp.ones_like(ab_re)], [jnp.zeros_like(ab_im)]
    for _ in range(t):
        pr, pi = pw_re[-1], pw_im[-1]
        pw_re.append(pr * ab_re - pi * ab_im)
        pw_im.append(pr * ab_im + pi * ab_re)
    pw_re, pw_im = jnp.stack(pw_re), jnp.stack(pw_im)
    cr, ci = c_re.astype(F32), c_im.astype(F32)
    n_ch = SSM_GROUPS * SSM_GROUP

    xb_re = pw_re[:t, :, :, None] * bb_re[None] - pw_im[:t, :, :, None] * bb_im[None]
    xb_im = pw_re[:t, :, :, None] * bb_im[None] + pw_im[:t, :, :, None] * bb_re[None]
    xb = jnp.stack([xb_re, xb_im]).transpose(0, 1, 2, 4, 3).reshape(2, t, n_ch, SSM_STATE)
    xb = jnp.concatenate([xb, xb], axis=-1)
    ca_re = cr[None] * pw_re[:, :, None, :] - ci[None] * pw_im[:, :, None, :]
    ca_im = cr[None] * pw_im[:, :, None, :] + ci[None] * pw_re[:, :, None, :]
    ca = jnp.stack([ca_re, -ca_im]).transpose(0, 1, 4, 2, 3).reshape(2, t + 1, SSM_STATE, n_ch)

    w_y, w_st = pl.pallas_call(
        _ssm_expand_kernel,
        grid=(SSM_SLABS,),
        in_specs=[pl.BlockSpec((2, t, LANES, LANES), lambda j: (0, 0, j, 0)),
                  pl.BlockSpec((2, t + 1, SSM_STATE, LANES), lambda j: (0, 0, 0, j))],
        out_specs=[pl.BlockSpec((None, t * LANES + 2 * SLAB_STATE, t * LANES), lambda j: (j, 0, 0)),
                   pl.BlockSpec((None, t * LANES, 2 * SLAB_STATE), lambda j: (j, 0, 0))],
        out_shape=[jax.ShapeDtypeStruct((SSM_SLABS, t * LANES + 2 * SLAB_STATE, t * LANES), BF16),
                   jax.ShapeDtypeStruct((SSM_SLABS, t * LANES, 2 * SLAB_STATE), BF16)],
        compiler_params=pltpu.CompilerParams(dimension_semantics=("arbitrary",),
                                             vmem_limit_bytes=VMEM_LIMIT),
        name="s5_expand",
    )(xb, ca)
    a_t = jnp.stack([pw_re[t].reshape(-1), pw_im[t].reshape(-1)])
    return w_y, w_st, a_t


def _ssm_expand_kernel(xb_ref, ca_ref, wy_ref, wst_ref):
    t = SSM_CHUNK
    keep_st = (lax.broadcasted_iota(jnp.int32, (LANES, SLAB_STATE), 0) // SSM_GROUP
               == lax.broadcasted_iota(jnp.int32, (LANES, SLAB_STATE), 1) // SSM_STATE)
    keep_out = (lax.broadcasted_iota(jnp.int32, (SLAB_STATE, LANES), 0) // SSM_STATE
                == lax.broadcasted_iota(jnp.int32, (SLAB_STATE, LANES), 1) // SSM_GROUP)

    def to_state(d):
        parts = [jnp.where(keep_st, jnp.concatenate([xb_ref[ri, d]] * (SLAB_STATE // LANES), axis=1), 0.0)
                 for ri in range(2)]
        return jnp.concatenate(parts, axis=1)

    def from_state(d):
        parts = [jnp.where(keep_out, jnp.concatenate([ca_ref[ri, d]] * GROUPS_PER_SLAB, axis=0), 0.0)
                 for ri in range(2)]
        return jnp.concatenate(parts, axis=0)

    c0 = from_state(0)
    zero = jnp.zeros((LANES, LANES), BF16)
    for d in range(t):
        st = to_state(d)
        sg = t - 1 - d
        wst_ref[sg * LANES:(sg + 1) * LANES, :] = st.astype(BF16)
        taps = jnp.dot(st, c0, preferred_element_type=F32, precision=lax.Precision.HIGHEST).astype(BF16)
        for sg in range(t - d):
            wy_ref[sg * LANES:(sg + 1) * LANES, (sg + d) * LANES:(sg + d + 1) * LANES] = taps
        if d:
            for tau in range(t - d):
                wy_ref[(tau + d) * LANES:(tau + d + 1) * LANES, tau * LANES:(tau + 1) * LANES] = zero
    for tau in range(t):
        wy_ref[t * LANES:, tau * LANES:(tau + 1) * LANES] = from_state(tau + 1).astype(BF16)


def _ssm_kernel(x_ref, wuz_ref, wy_ref, wst_ref, at_ref, dskip_ref, wglu_ref, bglu_ref,
                out_ref, u_st, z_scr, y_st, carry_scr, *state_scrs):
    s_scrs, xp_scrs = state_scrs[:SSM_SLABS], state_scrs[SSM_SLABS:]
    bsz, tl, d_model = x_ref.shape
    t = SSM_CHUNK
    ch = tl // t
    rows = bsz * tl
    crows = bsz * ch
    tiles_per_slab = SLAB_STATE // LANES
    n_tiles = SSM_SLABS * tiles_per_slab
    pitch = s_scrs[0].shape[1] // bsz

    @pl.when(pl.program_id(0) == 0)
    def _():
        carry_scr[...] = jnp.zeros_like(carry_scr)

    xb = x_ref[...].reshape(rows, d_model).astype(BF16)
    uz = jnp.dot(xb, wuz_ref[...], preferred_element_type=F32)
    for c, tile in enumerate(_lane_tiles(uz[:, :SSM_WIDTH])):
        u_st[c] = tile
    z_scr[...] = uz[:, SSM_WIDTH:]

    def chunk_lhs(j):
        return jnp.concatenate([u_st[j, pl.ds(sg, crows, stride=t), :] for sg in range(t)], axis=1).astype(BF16)

    n_slab_tiles = 2 * tiles_per_slab

    def state_increments(j):
        st = jnp.dot(chunk_lhs(j), wst_ref[j], preferred_element_type=F32)
        for c, tile in enumerate(_lane_tiles(st)):
            for b in range(bsz):
                s_scrs[j][c, b * pitch:b * pitch + ch, :] = tile[b * ch:(b + 1) * ch]

    def recurrence(j):
        s_scr, xp_scr = s_scrs[j], xp_scrs[j]
        for c_r in range(tiles_per_slab):
            c_i = tiles_per_slab + c_r
            g_r, g_i = j * tiles_per_slab + c_r, n_tiles + j * tiles_per_slab + c_r
            a_re, a_im = at_ref[g_r:g_r + 1, :], at_ref[g_i:g_i + 1, :]
            x_re, x_im = carry_scr[g_r], carry_scr[g_i]
            for k in range(ch):
                sel = pl.ds(k, bsz, stride=pitch)
                xp_scr[c_r, sel, :] = x_re
                xp_scr[c_i, sel, :] = x_im
                x_re, x_im = (a_re * x_re - a_im * x_im + s_scr[c_r, sel, :],
                              a_re * x_im + a_im * x_re + s_scr[c_i, sel, :])
            carry_scr[g_r] = x_re
            carry_scr[g_i] = x_im

    def response(j):
        xin = [jnp.concatenate([xp_scrs[j][c, b * pitch:b * pitch + ch, :] for b in range(bsz)],
                               axis=0).astype(BF16) for c in range(n_slab_tiles)]
        yj = jnp.dot(jnp.concatenate([chunk_lhs(j)] + xin, axis=1), wy_ref[j], preferred_element_type=F32)
        for tau, tile in enumerate(_lane_tiles(yj)):
            y_st[j, pl.ds(tau, crows, stride=t), :] = tile

    state_increments(0)
    for j in range(SSM_SLABS):
        if j + 1 < SSM_SLABS:
            state_increments(j + 1)
        recurrence(j)
        if j:
            response(j - 1)
    response(SSM_SLABS - 1)

    for b in range(bsz):
        rs = slice(b * tl, (b + 1) * tl)
        y = jnp.concatenate([y_st[c, rs, :] for c in range(SSM_SLABS)], axis=1)
        u = jnp.concatenate([u_st[c, rs, :] for c in range(SSM_SLABS)], axis=1)
        y = jax.nn.gelu(y + dskip_ref[...] * u)
        gate = _sigmoid(jnp.dot(y.astype(BF16), wglu_ref[...], preferred_element_type=F32) + bglu_ref[...])
        out_ref[b] = (y * gate * _silu(z_scr[rs, :])).astype(BF16)


def _ssm_branch(x, w_in, ssm_w, d_skip, w_glu, b_glu, tl=128):
    bsz, seq, d = x.shape
    u_col = 3 * QKV_WIDTH + ATTN_WIDTH
    t = SSM_CHUNK
    rows, ch = bsz * tl, tl // t
    pitch = -(-ch // PHASE_SPLIT) * PHASE_SPLIT
    pitch += PHASE_SPLIT * (1 - (pitch // PHASE_SPLIT) % 2)
    w_y, w_st, a_t = ssm_w
    n_tiles2 = 2 * SSM_SLABS * SLAB_STATE // LANES
    a_t = a_t.reshape(n_tiles2, LANES)
    out = pl.pallas_call(
        _ssm_kernel,
        grid=(seq // tl,),
        in_specs=[pl.BlockSpec((bsz, tl, d), lambda i: (0, i, 0)),
                  _col_block_spec(d, 2 * SSM_WIDTH, u_col),
                  _const_spec(w_y.shape),
                  _const_spec(w_st.shape), _const_spec(a_t.shape), _const_spec((1, SSM_WIDTH)),
                  _const_spec(w_glu.shape), _const_spec((1, SSM_WIDTH))],
        out_specs=pl.BlockSpec((bsz, tl, SSM_WIDTH), lambda i: (0, i, 0)),
        out_shape=jax.ShapeDtypeStruct((bsz, seq, SSM_WIDTH), BF16),
        scratch_shapes=[pltpu.VMEM((SSM_SLABS, rows, LANES), F32),
                        pltpu.VMEM((rows, SSM_WIDTH), F32),
                        pltpu.VMEM((SSM_SLABS, rows, LANES), F32),
                        pltpu.VMEM((n_tiles2, bsz, LANES), F32)]
        + [pltpu.VMEM((n_tiles2 // SSM_SLABS, bsz * pitch, LANES), F32)] * (2 * SSM_SLABS),
        compiler_params=pltpu.CompilerParams(dimension_semantics=("arbitrary",),
                                             vmem_limit_bytes=VMEM_LIMIT),
        name="s5_branch",
    )(x, w_in, w_y, w_st, a_t, d_skip.reshape(1, SSM_WIDTH).astype(F32),
      w_glu, b_glu.reshape(1, SSM_WIDTH).astype(F32))
    return out.reshape(bsz * seq, SSM_WIDTH)


def _merge_kernel(alpha, x_ref, attn_ref, ys_ref, wz_ref, wg_ref, wa_ref, ws_ref, wo_ref, g_ref, b_ref, out_ref):
    x = x_ref[...]
    d_model = x.shape[1]
    xb = x.astype(BF16)
    z_a = jnp.dot(xb, wz_ref[...], preferred_element_type=F32)
    gates = jnp.dot(xb, wg_ref[...], preferred_element_type=F32)
    gate_a = gates[:, :d_model]
    gate_s = gates[:, d_model:]
    attn = attn_ref[...].astype(F32) * _silu(z_a)
    y_a = jnp.dot(attn.astype(BF16), wa_ref[...], preferred_element_type=F32)
    y_s = jnp.dot(ys_ref[...], ws_ref[...], preferred_element_type=F32)
    merged = _sigmoid(gate_a) * y_a + _sigmoid(gate_s) * y_s
    out = jnp.dot(merged.astype(BF16), wo_ref[...], preferred_element_type=F32)
    h = alpha * x + out
    mu = jnp.mean(h, axis=-1, keepdims=True)
    hc = h - mu
    var = jnp.mean(hc * hc, axis=-1, keepdims=True)
    out_ref[...] = hc * lax.rsqrt(var + LN_EPS) * g_ref[...] + b_ref[...]


def _merge(x2, attn, ys, w_in, w_attn_up, w_ssm_up, w_o, ln_g, ln_b, alpha, tm=1024):
    n, d = x2.shape
    row = lambda width: pl.BlockSpec((tm, width), lambda i: (i, 0))
    z_col = 3 * QKV_WIDTH
    gate_col = z_col + ATTN_WIDTH + 2 * SSM_WIDTH
    return pl.pallas_call(
        functools.partial(_merge_kernel, alpha),
        grid=(n // tm,),
        in_specs=[row(d), row(ATTN_WIDTH), row(SSM_WIDTH),
                  _col_block_spec(d, ATTN_WIDTH, z_col), _col_block_spec(d, 2 * d, gate_col),
                  _const_spec(w_attn_up.shape), _const_spec(w_ssm_up.shape),
                  _const_spec(w_o.shape), _const_spec((1, d)), _const_spec((1, d))],
        out_specs=row(d),
        out_shape=jax.ShapeDtypeStruct((n, d), F32),
        compiler_params=pltpu.CompilerParams(dimension_semantics=("arbitrary",),
                                             vmem_limit_bytes=VMEM_LIMIT),
        name="merge_out_ln",
    )(x2, attn, ys, w_in, w_in, w_attn_up, w_ssm_up, w_o,
      ln_g.reshape(1, d).astype(F32), ln_b.reshape(1, d).astype(F32))


def _layer(h, w_in, lam_re, lam_im, log_dt, b_re, b_im, c_re, c_im, d_skip,
           w_glu, b_glu, w_attn_up, w_ssm_up, w_o, ln_g, ln_b, alpha):
    bsz, seq, d = h.shape
    assert seq % SPAN == 0 and all(w // r == BLOCK for w, r in GROUP_PATTERNS)
    assert tuple(r for _, r in GROUP_PATTERNS) == (1, PHASE_SPLIT, PHASE_SPLIT * PHASE_SPLIT)
    x2 = h.reshape(bsz * seq, d)
    w_in = w_in.astype(BF16)

    qkv = _qkv_proj(x2, w_in, bsz, seq)
    attn = _attention(qkv, bsz, seq)
    ssm_w = _ssm_weights(lam_re, lam_im, log_dt, b_re, b_im, c_re, c_im)
    ys = _ssm_branch(h, w_in, ssm_w, d_skip, w_glu.astype(BF16), b_glu)
    out = _merge(x2, attn, ys, w_in, w_attn_up.astype(BF16), w_ssm_up.astype(BF16),
                 w_o.astype(BF16), ln_g, ln_b, alpha)
    return out.reshape(bsz, seq, d)


def kernel(x, w_in, lam_re, lam_im, log_dt, b_re, b_im, c_re, c_im, d_skip,
           w_glu, b_glu, w_attn_up, w_ssm_up, w_o, ln_g, ln_b):
    depth = w_in.shape[0]
    alpha = (2.0 * depth) ** 0.25
    h = x
    for layer in range(depth):
        h = _layer(h, w_in[layer], lam_re[layer], lam_im[layer], log_dt[layer], b_re[layer], b_im[layer],
                   c_re[layer], c_im[layer], d_skip[layer], w_glu[layer], b_glu[layer], w_attn_up[layer],
                   w_ssm_up[layer], w_o[layer], ln_g[layer], ln_b[layer], alpha)
    return h
```

```python
import functools
import math

import jax
import jax.numpy as jnp
import numpy as np
from jax import lax
from jax.experimental import pallas as pl
from jax.experimental.pallas import tpu as pltpu

F32 = jnp.float32
BF16 = jnp.bfloat16

HEAD_DIM = 64
N_SLOTS = 8
GROUP_PATTERNS = ((128, 1), (512, 4), (2048, 16))
N_GROUPS = len(GROUP_PATTERNS)
ATTN_WIDTH = N_SLOTS * HEAD_DIM
QKV_WIDTH = N_GROUPS * ATTN_WIDTH
BLOCK = 128
SPAN = GROUP_PATTERNS[-1][1] * BLOCK
UNITS = SPAN // BLOCK
SSM_WIDTH = 512
SSM_GROUP = 16
SSM_GROUPS = SSM_WIDTH // SSM_GROUP
SSM_STATE = 64
LN_EPS = 1e-5
MASK_VALUE = -1e30
LOG2E = math.log2(math.e)
Q_SCALE = LOG2E / math.sqrt(HEAD_DIM)

LANES = 128
SUBLANES = 8
HEAD_PAIRS = ATTN_WIDTH // LANES
PHASE_SPLIT = 4
QUARTER = SPAN // PHASE_SPLIT
BLOCK_ROW_TOKEN = [PHASE_SPLIT * k + ph for ph in range(PHASE_SPLIT) for k in range(BLOCK // PHASE_SPLIT)]
SSM_CHUNK = PHASE_SPLIT
SSM_SLABS = SSM_WIDTH // LANES
GROUPS_PER_SLAB = LANES // SSM_GROUP
SLAB_STATE = GROUPS_PER_SLAB * SSM_STATE
LN_CHUNKS = 4
VMEM_LIMIT = 56 * 1024 * 1024


def _const_spec(shape):
    nd = len(shape)
    return pl.BlockSpec(shape, lambda *_: (0,) * nd, pipeline_mode=pl.Buffered(1))


def _col_block_spec(rows, width, start):
    assert start % width == 0
    return pl.BlockSpec((rows, width), lambda *_: (0, start // width), pipeline_mode=pl.Buffered(1))


def _sigmoid(x):
    return 0.5 * jnp.tanh(0.5 * x) + 0.5


def _silu(x):
    return x * _sigmoid(x)


def _lane_tiles(x):
    return [x[:, c * LANES:(c + 1) * LANES] for c in range(x.shape[1] // LANES)]


def _qkv_kernel(x_ref, w_ref, *refs):
    out_refs, x_stage, stage = refs[:-2], refs[-2], refs[-1]
    tm = x_ref.shape[0]
    sub = tm // PHASE_SPLIT
    x = x_ref[...]
    for c, tile in enumerate(_lane_tiles(x)):
        x_stage[c] = tile
    xb = {1: x.astype(BF16),
          PHASE_SPLIT: jnp.concatenate(
              [jnp.concatenate([x_stage[c, pl.ds(q, sub, stride=PHASE_SPLIT), :] for c in range(x_stage.shape[0])],
                               axis=1) for q in range(PHASE_SPLIT)], axis=0).astype(BF16)}
    piece = BLOCK // PHASE_SPLIT
    xb_block_split = jnp.concatenate(
        [jnp.concatenate([x_stage[c, pl.ds(blk * BLOCK + ph, piece, stride=PHASE_SPLIT), :]
                          for c in range(x_stage.shape[0])], axis=1)
         for blk in range(tm // BLOCK) for ph in range(PHASE_SPLIT)], axis=0).astype(BF16)
    for kind in range(3):
        col = kind * QKV_WIDTH
        first = jnp.dot(xb_block_split if kind == 0 else xb[1], w_ref[:, col:col + ATTN_WIDTH],
                        preferred_element_type=F32)
        rest = jnp.dot(xb[PHASE_SPLIT], w_ref[:, col + ATTN_WIDTH:col + QKV_WIDTH], preferred_element_type=F32)
        by_group = [first] + [rest[:, (g - 1) * ATTN_WIDTH:g * ATTN_WIDTH] for g in range(1, N_GROUPS)]
        for g, (_, r) in enumerate(GROUP_PATTERNS):
            res = by_group[g]
            if kind == 0:
                res = res * Q_SCALE
            o_ref = out_refs[kind * N_GROUPS + g]
            if r == 1:
                o_ref[0] = res.astype(BF16)
            elif r == PHASE_SPLIT:
                for q in range(r):
                    o_ref[q] = res[q * sub:(q + 1) * sub].astype(BF16)
            else:
                for c, tile in enumerate(_lane_tiles(res)):
                    stage[c] = tile
                for q in range(PHASE_SPLIT):
                    for s in range(PHASE_SPLIT):
                        o_ref[q + PHASE_SPLIT * s] = jnp.concatenate(
                            [stage[c, pl.ds(q * sub + s, sub // PHASE_SPLIT, stride=PHASE_SPLIT), :]
                             for c in range(HEAD_PAIRS)], axis=1).astype(BF16)


def _qkv_proj(x2, w_in, bsz, seq, tm=1024):
    n, d = x2.shape
    tiles = seq // tm
    out_specs, out_shape = [], []
    for _ in range(3):
        for _, r in GROUP_PATTERNS:
            out_specs.append(pl.BlockSpec((None, r, tm // r, ATTN_WIDTH),
                                          lambda i: (i // tiles, 0, i % tiles, 0)))
            out_shape.append(jax.ShapeDtypeStruct((bsz, r, seq // r, ATTN_WIDTH), BF16))
    return pl.pallas_call(
        _qkv_kernel,
        grid=(n // tm,),
        in_specs=[pl.BlockSpec((tm, d), lambda i: (i, 0)), _col_block_spec(d, 3 * QKV_WIDTH, 0)],
        out_specs=out_specs,
        out_shape=out_shape,
        scratch_shapes=[pltpu.VMEM((d // LANES, tm, LANES), F32), pltpu.VMEM((HEAD_PAIRS, tm, LANES), F32)],
        compiler_params=pltpu.CompilerParams(dimension_semantics=("arbitrary",),
                                             vmem_limit_bytes=VMEM_LIMIT),
        name="qkv_proj",
    )(x2, w_in)


def _attn_bias():
    slopes = 2.0 ** (-8.0 * np.arange(1, N_SLOTS + 1) / N_SLOTS)
    qi = np.arange(BLOCK)[:, None]
    kj = np.arange(2 * BLOCK)[None, :]
    dist = BLOCK + qi - kj
    valid = (dist >= 0) & (dist <= BLOCK)
    valid = np.stack([valid, valid & (kj >= BLOCK)])
    out = []
    for _, r in GROUP_PATTERNS:
        alibi = (-LOG2E * slopes)[:, None, None, None] * (dist * r)[None, None]
        bias = np.where(valid[None], alibi, MASK_VALUE)
        out.append(bias[:, :, BLOCK_ROW_TOKEN, :] if r == 1 else bias)
    return jnp.asarray(np.stack(out).astype(np.float32))


def _attn_kernel(*refs):
    ins, bias_ref, out_ref = refs[:5 * N_GROUPS], refs[5 * N_GROUPS], refs[5 * N_GROUPS + 1]
    o_stage, l_stage, m_stage, nat_stage = refs[5 * N_GROUPS + 2:]
    first_span = pl.program_id(2) == 0
    lane = lax.broadcasted_iota(jnp.int32, (BLOCK, LANES), 1)
    low = lane < HEAD_DIM

    for g, (_, r) in enumerate(GROUP_PATTERNS):
        q_ref, kc_ref, kp_ref, vc_ref, vp_ref = ins[5 * g:5 * g + 5]
        nb = UNITS // r

        for p, n in ((p, n) for p in range(r) for n in range(nb)):
            q = q_ref[p, n * BLOCK:(n + 1) * BLOCK, :]
            if n == 0:
                kcat = jnp.concatenate([kp_ref[p], kc_ref[p, :BLOCK, :]], axis=0)
                vcat = jnp.concatenate([vp_ref[p], vc_ref[p, :BLOCK, :]], axis=0)
                variant = jnp.where(first_span, 1, 0)
            else:
                kcat = kc_ref[p, (n - 1) * BLOCK:(n + 1) * BLOCK, :]
                vcat = vc_ref[p, (n - 1) * BLOCK:(n + 1) * BLOCK, :]
                variant = 0
            v_ext = jnp.concatenate([vcat, jnp.ones_like(vcat)], axis=1)
            qm = jnp.concatenate([jnp.where(low, q, jnp.zeros_like(q)),
                                  jnp.where(low, jnp.zeros_like(q), q)], axis=0)
            s_both = lax.dot_general(qm, kcat, (((1,), (1,)), ((), ())), preferred_element_type=F32)
            es, tops = [], []
            for hh in range(2):
                s = s_both[hh * BLOCK:(hh + 1) * BLOCK] + bias_ref[g, hh, variant]
                m = jnp.max(s, axis=1, keepdims=True)
                es.append(jnp.exp2(s - m).astype(BF16))
                tops.append(m)
            pv = jnp.dot(jnp.concatenate(es, axis=0), v_ext, preferred_element_type=F32)
            staged = ((o_stage, jnp.where(low, pv[:BLOCK, :LANES], pv[BLOCK:, :LANES])),
                      (l_stage, jnp.where(low, pv[:BLOCK, LANES:], pv[BLOCK:, LANES:])),
                      (m_stage, jnp.where(low, tops[0], tops[1])))
            for stage, val in staged:
                if r == 1:
                    piece = BLOCK // PHASE_SPLIT
                    for ph in range(PHASE_SPLIT):
                        dst = ph * QUARTER + n * piece
                        stage[g, dst:dst + piece, :] = val[ph * piece:(ph + 1) * piece]
                elif r == PHASE_SPLIT:
                    dst = p * QUARTER + n * BLOCK
                    stage[g, dst:dst + BLOCK, :] = val
                else:
                    start = (p % PHASE_SPLIT) * QUARTER + p // PHASE_SPLIT
                    stage[g, pl.ds(start, BLOCK, stride=PHASE_SPLIT), :] = val

    def mix(i, carry):
        rows = pl.ds(pl.multiple_of(i * BLOCK, BLOCK), BLOCK)

        tops = [m_stage[g, rows, :] for g in range(N_GROUPS)]
        top = functools.reduce(jnp.maximum, tops)
        weights = [jnp.exp2(tops[g] - top) for g in range(N_GROUPS)]
        num = sum(weights[g] * o_stage[g, rows, :] for g in range(N_GROUPS))
        den = sum(weights[g] * l_stage[g, rows, :] for g in range(N_GROUPS))
        start = (i % PHASE_SPLIT) * (PHASE_SPLIT * BLOCK) + i // PHASE_SPLIT
        nat_stage[pl.ds(start, BLOCK, stride=PHASE_SPLIT), :] = num / den
        return carry

    lax.fori_loop(0, UNITS, mix, 0, unroll=4)
    out_ref[...] = nat_stage[...].astype(BF16)


def _attention(qkv, bsz, seq):
    spans = seq // SPAN
    in_specs, args = [], []
    for g, (_, r) in enumerate(GROUP_PATTERNS):
        nb = UNITS // r
        cur = pl.BlockSpec((None, r, SPAN // r, LANES), lambda hp, b, s: (b, 0, s, hp))
        prev = pl.BlockSpec((None, r, BLOCK, LANES),
                            lambda hp, b, s, nb=nb: (b, 0, jnp.maximum(nb * s - 1, 0), hp))
        q, k, v = qkv[g], qkv[N_GROUPS + g], qkv[2 * N_GROUPS + g]
        in_specs += [cur, cur, prev, cur, prev]
        args += [q, k, k, v, v]
    in_specs.append(pl.BlockSpec((N_GROUPS, 2, 2, BLOCK, 2 * BLOCK), lambda hp, b, s: (0, hp, 0, 0, 0)))
    scratch = [pltpu.VMEM((N_GROUPS, SPAN, LANES), F32)] * 3
    scratch += [pltpu.VMEM((SPAN, LANES), F32)]
    return pl.pallas_call(
        _attn_kernel,
        grid=(HEAD_PAIRS, bsz, spans),
        in_specs=in_specs,
        out_specs=pl.BlockSpec((SPAN, LANES), lambda hp, b, s: (b * spans + s, hp)),
        out_shape=jax.ShapeDtypeStruct((bsz * seq, ATTN_WIDTH), BF16),
        scratch_shapes=scratch,
        compiler_params=pltpu.CompilerParams(dimension_semantics=("arbitrary",) * 3,
                                             vmem_limit_bytes=VMEM_LIMIT),
        name="band_attn",
    )(*args, _attn_bias())


def _ssm_weights(lam_re, lam_im, log_dt, b_re, b_im, c_re, c_im):
    t = SSM_CHUNK
    lr, li = lam_re.astype(F32), lam_im.astype(F32)
    dt = jnp.exp(log_dt.astype(F32))[:, None]
    mag = jnp.exp(lr * dt)
    ab_re, ab_im = mag * jnp.cos(li * dt), mag * jnp.sin(li * dt)
    nr, ni = ab_re - 1.0, ab_im
    den = lr * lr + li * li
    coef_re = (nr * lr + ni * li) / den
    coef_im = (ni * lr - nr * li) / den
    br, bi = b_re.astype(F32), b_im.astype(F32)
    bb_re = coef_re[..., None] * br - coef_im[..., None] * bi
    bb_im = coef_re[..., None] * bi + coef_im[..., None] * br
    pw_re, pw_im = [jnp.ones_like(ab_re)], [jnp.zeros_like(ab_im)]
    for _ in range(t):
        pr, pi = pw_re[-1], pw_im[-1]
        pw_re.append(pr * ab_re - pi * ab_im)
        pw_im.append(pr * ab_im + pi * ab_re)
    pw_re, pw_im = jnp.stack(pw_re), jnp.stack(pw_im)
    cr, ci = c_re.astype(F32), c_im.astype(F32)
    n_ch = SSM_GROUPS * SSM_GROUP

    xb_re = pw_re[:t, :, :, None] * bb_re[None] - pw_im[:t, :, :, None] * bb_im[None]
    xb_im = pw_re[:t, :, :, None] * bb_im[None] + pw_im[:t, :, :, None] * bb_re[None]
    xb = jnp.stack([xb_re, xb_im]).transpose(0, 1, 2, 4, 3).reshape(2, t, n_ch, SSM_STATE)
    xb = jnp.concatenate([xb, xb], axis=-1)
    ca_re = cr[None] * pw_re[:, :, None, :] - ci[None] * pw_im[:, :, None, :]
    ca_im = cr[None] * pw_im[:, :, None, :] + ci[None] * pw_re[:, :, None, :]
    ca = jnp.stack([ca_re, -ca_im]).transpose(0, 1, 4, 2, 3).reshape(2, t + 1, SSM_STATE, n_ch)

    w_y, w_st = pl.pallas_call(
        _ssm_expand_kernel,
        grid=(SSM_SLABS,),
        in_specs=[pl.BlockSpec((2, t, LANES, LANES), lambda j: (0, 0, j, 0)),
                  pl.BlockSpec((2, t + 1, SSM_STATE, LANES), lambda j: (0, 0, 0, j))],
        out_specs=[pl.BlockSpec((None, t * LANES + 2 * SLAB_STATE, t * LANES), lambda j: (j, 0, 0)),
                   pl.BlockSpec((None, t * LANES, 2 * SLAB_STATE), lambda j: (j, 0, 0))],
        out_shape=[jax.ShapeDtypeStruct((SSM_SLABS, t * LANES + 2 * SLAB_STATE, t * LANES), BF16),
                   jax.ShapeDtypeStruct((SSM_SLABS, t * LANES, 2 * SLAB_STATE), BF16)],
        compiler_params=pltpu.CompilerParams(dimension_semantics=("arbitrary",),
                                             vmem_limit_bytes=VMEM_LIMIT),
        name="s5_expand",
    )(xb, ca)
    a_t = jnp.stack([pw_re[t].reshape(-1), pw_im[t].reshape(-1)])
    return w_y, w_st, a_t


def _ssm_expand_kernel(xb_ref, ca_ref, wy_ref, wst_ref):
    t = SSM_CHUNK
    keep_st = (lax.broadcasted_iota(jnp.int32, (LANES, SLAB_STATE), 0) // SSM_GROUP
               == lax.broadcasted_iota(jnp.int32, (LANES, SLAB_STATE), 1) // SSM_STATE)
    keep_out = (lax.broadcasted_iota(jnp.int32, (SLAB_STATE, LANES), 0) // SSM_STATE
                == lax.broadcasted_iota(jnp.int32, (SLAB_STATE, LANES), 1) // SSM_GROUP)

    def to_state(d):
        parts = [jnp.where(keep_st, jnp.concatenate([xb_ref[ri, d]] * (SLAB_STATE // LANES), axis=1), 0.0)
                 for ri in range(2)]
        return jnp.concatenate(parts, axis=1)

    def from_state(d):
        parts = [jnp.where(keep_out, jnp.concatenate([ca_ref[ri, d]] * GROUPS_PER_SLAB, axis=0), 0.0)
                 for ri in range(2)]
        return jnp.concatenate(parts, axis=0)

    c0 = from_state(0)
    zero = jnp.zeros((LANES, LANES), BF16)
    for d in range(t):
        st = to_state(d)
        sg = t - 1 - d
        wst_ref[sg * LANES:(sg + 1) * LANES, :] = st.astype(BF16)
        taps = jnp.dot(st, c0, preferred_element_type=F32, precision=lax.Precision.HIGHEST).astype(BF16)
        for sg in range(t - d):
            wy_ref[sg * LANES:(sg + 1) * LANES, (sg + d) * LANES:(sg + d + 1) * LANES] = taps
        if d:
            for tau in range(t - d):
                wy_ref[(tau + d) * LANES:(tau + d + 1) * LANES, tau * LANES:(tau + 1) * LANES] = zero
    for tau in range(t):
        wy_ref[t * LANES:, tau * LANES:(tau + 1) * LANES] = from_state(tau + 1).astype(BF16)


def _ssm_kernel(x_ref, wuz_ref, wy_ref, wst_ref, at_ref, dskip_ref, wglu_ref, bglu_ref,
                out_ref, u_st, z_scr, y_st, carry_scr, *state_scrs):
    s_scrs, xp_scrs = state_scrs[:SSM_SLABS], state_scrs[SSM_SLABS:]
    bsz, tl, d_model = x_ref.shape
    t = SSM_CHUNK
    ch = tl // t
    rows = bsz * tl
    crows = bsz * ch
    tiles_per_slab = SLAB_STATE // LANES
    n_tiles = SSM_SLABS * tiles_per_slab
    pitch = s_scrs[0].shape[1] // bsz

    @pl.when(pl.program_id(0) == 0)
    def _():
        carry_scr[...] = jnp.zeros_like(carry_scr)

    xb = x_ref[...].reshape(rows, d_model).astype(BF16)
    uz = jnp.dot(xb, wuz_ref[...], preferred_element_type=F32)
    for c, tile in enumerate(_lane_tiles(uz[:, :SSM_WIDTH])):
        u_st[c] = tile
    z_scr[...] = uz[:, SSM_WIDTH:]

    def chunk_lhs(j):
        return jnp.concatenate([u_st[j, pl.ds(sg, crows, stride=t), :] for sg in range(t)], axis=1).astype(BF16)

    n_slab_tiles = 2 * tiles_per_slab

    def state_increments(j):
        st = jnp.dot(chunk_lhs(j), wst_ref[j], preferred_element_type=F32)
        for c, tile in enumerate(_lane_tiles(st)):
            for b in range(bsz):
                s_scrs[j][c, b * pitch:b * pitch + ch, :] = tile[b * ch:(b + 1) * ch]

    def recurrence(j):
        s_scr, xp_scr = s_scrs[j], xp_scrs[j]
        for c_r in range(tiles_per_slab):
            c_i = tiles_per_slab + c_r
            g_r, g_i = j * tiles_per_slab + c_r, n_tiles + j * tiles_per_slab + c_r
            a_re, a_im = at_ref[g_r:g_r + 1, :], at_ref[g_i:g_i + 1, :]
            x_re, x_im = carry_scr[g_r], carry_scr[g_i]
            for k in range(ch):
                sel = pl.ds(k, bsz, stride=pitch)
                xp_scr[c_r, sel, :] = x_re
                xp_scr[c_i, sel, :] = x_im
                x_re, x_im = (a_re * x_re - a_im * x_im + s_scr[c_r, sel, :],
                              a_re * x_im + a_im * x_re + s_scr[c_i, sel, :])
            carry_scr[g_r] = x_re
            carry_scr[g_i] = x_im

    def response(j):
        xin = [jnp.concatenate([xp_scrs[j][c, b * pitch:b * pitch + ch, :] for b in range(bsz)],
                               axis=0).astype(BF16) for c in range(n_slab_tiles)]
        yj = jnp.dot(jnp.concatenate([chunk_lhs(j)] + xin, axis=1), wy_ref[j], preferred_element_type=F32)
        for tau, tile in enumerate(_lane_tiles(yj)):
            y_st[j, pl.ds(tau, crows, stride=t), :] = tile

    state_increments(0)
    for j in range(SSM_SLABS):
        if j + 1 < SSM_SLABS:
            state_increments(j + 1)
        recurrence(j)
        if j:
            response(j - 1)
    response(SSM_SLABS - 1)

    for b in range(bsz):
        rs = slice(b * tl, (b + 1) * tl)
        y = jnp.concatenate([y_st[c, rs, :] for c in range(SSM_SLABS)], axis=1)
        u = jnp.concatenate([u_st[c, rs, :] for c in range(SSM_SLABS)], axis=1)
        y = jax.nn.gelu(y + dskip_ref[...] * u)
        gate = _sigmoid(jnp.dot(y.astype(BF16), wglu_ref[...], preferred_element_type=F32) + bglu_ref[...])
        out_ref[b] = (y * gate * _silu(z_scr[rs, :])).astype(BF16)


def _ssm_branch(x, w_in, ssm_w, d_skip, w_glu, b_glu, tl=128):
    bsz, seq, d = x.shape
    u_col = 3 * QKV_WIDTH + ATTN_WIDTH
    t = SSM_CHUNK
    rows, ch = bsz * tl, tl // t
    pitch = -(-ch // PHASE_SPLIT) * PHASE_SPLIT
    pitch += PHASE_SPLIT * (1 - (pitch // PHASE_SPLIT) % 2)
    w_y, w_st, a_t = ssm_w
    n_tiles2 = 2 * SSM_SLABS * SLAB_STATE // LANES
    a_t = a_t.reshape(n_tiles2, LANES)
    out = pl.pallas_call(
        _ssm_kernel,
        grid=(seq // tl,),
        in_specs=[pl.BlockSpec((bsz, tl, d), lambda i: (0, i, 0)),
                  _col_block_spec(d, 2 * SSM_WIDTH, u_col),
                  _const_spec(w_y.shape),
                  _const_spec(w_st.shape), _const_spec(a_t.shape), _const_spec((1, SSM_WIDTH)),
                  _const_spec(w_glu.shape), _const_spec((1, SSM_WIDTH))],
        out_specs=pl.BlockSpec((bsz, tl, SSM_WIDTH), lambda i: (0, i, 0)),
        out_shape=jax.ShapeDtypeStruct((bsz, seq, SSM_WIDTH), BF16),
        scratch_shapes=[pltpu.VMEM((SSM_SLABS, rows, LANES), F32),
                        pltpu.VMEM((rows, SSM_WIDTH), F32),
                        pltpu.VMEM((SSM_SLABS, rows, LANES), F32),
                        pltpu.VMEM((n_tiles2, bsz, LANES), F32)]
        + [pltpu.VMEM((n_tiles2 // SSM_SLABS, bsz * pitch, LANES), F32)] * (2 * SSM_SLABS),
        compiler_params=pltpu.CompilerParams(dimension_semantics=("arbitrary",),
                                             vmem_limit_bytes=VMEM_LIMIT),
        name="s5_branch",
    )(x, w_in, w_y, w_st, a_t, d_skip.reshape(1, SSM_WIDTH).astype(F32),
      w_glu, b_glu.reshape(1, SSM_WIDTH).astype(F32))
    return out.reshape(bsz * seq, SSM_WIDTH)


def _merge_kernel(alpha, x_ref, attn_ref, ys_ref, wz_ref, wg_ref, wa_ref, ws_ref, wo_ref, g_ref, b_ref, out_ref):
    tm, d_model = x_ref.shape
    y_s = jnp.dot(ys_ref[...], ws_ref[...], preferred_element_type=F32)
    xb = x_ref[...].astype(BF16)
    z_a = jnp.dot(xb, wz_ref[...], preferred_element_type=F32)
    gates = jnp.dot(xb, wg_ref[...], preferred_element_type=F32)
    gate_a = gates[:, :d_model]
    gate_s = gates[:, d_model:]
    attn = attn_ref[...].astype(F32) * _silu(z_a)
    y_a = jnp.dot(attn.astype(BF16), wa_ref[...], preferred_element_type=F32)
    merged = (_sigmoid(gate_a) * y_a + _sigmoid(gate_s) * y_s).astype(BF16)
    chunk = tm // LN_CHUNKS
    for i in range(LN_CHUNKS):
        rows = slice(i * chunk, (i + 1) * chunk)
        out = jnp.dot(merged[rows], wo_ref[...], preferred_element_type=F32)
        h = alpha * x_ref[rows, :] + out
        mu = jnp.mean(h, axis=-1, keepdims=True)
        hc = h - mu
        var = jnp.mean(hc * hc, axis=-1, keepdims=True)
        out_ref[rows, :] = hc * lax.rsqrt(var + LN_EPS) * g_ref[...] + b_ref[...]


def _merge(x2, attn, ys, w_in, w_attn_up, w_ssm_up, w_o, ln_g, ln_b, alpha, tm=1024):
    n, d = x2.shape
    row = lambda width: pl.BlockSpec((tm, width), lambda i: (i, 0))
    z_col = 3 * QKV_WIDTH
    gate_col = z_col + ATTN_WIDTH + 2 * SSM_WIDTH
    return pl.pallas_call(
        functools.partial(_merge_kernel, alpha),
        grid=(n // tm,),
        in_specs=[row(d), row(ATTN_WIDTH), row(SSM_WIDTH),
                  _col_block_spec(d, ATTN_WIDTH, z_col), _col_block_spec(d, 2 * d, gate_col),
                  _const_spec(w_attn_up.shape), _const_spec(w_ssm_up.shape),
                  _const_spec(w_o.shape), _const_spec((1, d)), _const_spec((1, d))],
        out_specs=row(d),
        out_shape=jax.ShapeDtypeStruct((n, d), F32),
        compiler_params=pltpu.CompilerParams(dimension_semantics=("arbitrary",),
                                             vmem_limit_bytes=VMEM_LIMIT),
        name="merge_out_ln",
    )(x2, attn, ys, w_in, w_in, w_attn_up, w_ssm_up, w_o,
      ln_g.reshape(1, d).astype(F32), ln_b.reshape(1, d).astype(F32))


def _layer(h, w_in, lam_re, lam_im, log_dt, b_re, b_im, c_re, c_im, d_skip,
           w_glu, b_glu, w_attn_up, w_ssm_up, w_o, ln_g, ln_b, alpha):
    bsz, seq, d = h.shape
    assert seq % SPAN == 0 and all(w // r == BLOCK for w, r in GROUP_PATTERNS)
    assert tuple(r for _, r in GROUP_PATTERNS) == (1, PHASE_SPLIT, PHASE_SPLIT * PHASE_SPLIT)
    x2 = h.reshape(bsz * seq, d)
    w_in = w_in.astype(BF16)

    qkv = _qkv_proj(x2, w_in, bsz, seq)
    attn = _attention(qkv, bsz, seq)
    ssm_w = _ssm_weights(lam_re, lam_im, log_dt, b_re, b_im, c_re, c_im)
    ys = _ssm_branch(h, w_in, ssm_w, d_skip, w_glu.astype(BF16), b_glu)
    out = _merge(x2, attn, ys, w_in, w_attn_up.astype(BF16), w_ssm_up.astype(BF16),
                 w_o.astype(BF16), ln_g, ln_b, alpha)
    return out.reshape(bsz, seq, d)


def kernel(x, w_in, lam_re, lam_im, log_dt, b_re, b_im, c_re, c_im, d_skip,
           w_glu, b_glu, w_attn_up, w_ssm_up, w_o, ln_g, ln_b):
    depth = w_in.shape[0]
    alpha = (2.0 * depth) ** 0.25
    h = x
    for layer in range(depth):
        h = _layer(h, w_in[layer], lam_re[layer], lam_im[layer], log_dt[layer], b_re[layer], b_im[layer],
                   c_re[layer], c_im[layer], d_skip[layer], w_glu[layer], b_glu[layer], w_attn_up[layer],
                   w_ssm_up[layer], w_o[layer], ln_g[layer], ln_b[layer], alpha)
    return h
```

```python
import functools
import math

import jax
import jax.numpy as jnp
import numpy as np
from jax import lax
from jax.experimental import pallas as pl
from jax.experimental.pallas import tpu as pltpu

F32 = jnp.float32
BF16 = jnp.bfloat16

HEAD_DIM = 64
N_SLOTS = 8
GROUP_PATTERNS = ((128, 1), (512, 4), (2048, 16))
N_GROUPS = len(GROUP_PATTERNS)
ATTN_WIDTH = N_SLOTS * HEAD_DIM
QKV_WIDTH = N_GROUPS * ATTN_WIDTH
BLOCK = 128
SPAN = GROUP_PATTERNS[-1][1] * BLOCK
UNITS = SPAN // BLOCK
SSM_WIDTH = 512
SSM_GROUP = 16
SSM_GROUPS = SSM_WIDTH // SSM_GROUP
SSM_STATE = 64
LN_EPS = 1e-5
MASK_VALUE = -1e30
LOG2E = math.log2(math.e)
Q_SCALE = LOG2E / math.sqrt(HEAD_DIM)

LANES = 128
HEAD_PAIRS = ATTN_WIDTH // LANES
PHASE_SPLIT = 4
QUARTER = SPAN // PHASE_SPLIT
BLOCK_ROW_TOKEN = [PHASE_SPLIT * k + ph for ph in range(PHASE_SPLIT) for k in range(BLOCK // PHASE_SPLIT)]
SSM_CHUNK = PHASE_SPLIT
SSM_SLABS = SSM_WIDTH // LANES
GROUPS_PER_SLAB = LANES // SSM_GROUP
SLAB_STATE = GROUPS_PER_SLAB * SSM_STATE
LN_CHUNKS = 4
V7X_VMEM_BYTES = 64 * 1024 * 1024
VMEM_LIMIT = V7X_VMEM_BYTES - 8 * 1024 * 1024


def _const_spec(shape):
    nd = len(shape)
    return pl.BlockSpec(shape, lambda *_: (0,) * nd, pipeline_mode=pl.Buffered(1))


def _col_block_spec(rows, width, start):
    assert start % width == 0
    return pl.BlockSpec((rows, width), lambda *_: (0, start // width), pipeline_mode=pl.Buffered(1))


def _sigmoid(x):
    return 0.5 * jnp.tanh(0.5 * x) + 0.5


def _silu(x):
    return x * _sigmoid(x)


def _lane_tiles(x):
    return [x[:, c * LANES:(c + 1) * LANES] for c in range(x.shape[1] // LANES)]


def _qkv_kernel(x_ref, w_ref, *refs):
    out_refs, x_stage, stage = refs[:-2], refs[-2], refs[-1]
    tm = x_ref.shape[0]
    sub = tm // PHASE_SPLIT
    x = x_ref[...]
    for c, tile in enumerate(_lane_tiles(x)):
        x_stage[c] = tile
    xb = {1: x.astype(BF16),
          PHASE_SPLIT: jnp.concatenate(
              [jnp.concatenate([x_stage[c, pl.ds(q, sub, stride=PHASE_SPLIT), :] for c in range(x_stage.shape[0])],
                               axis=1) for q in range(PHASE_SPLIT)], axis=0).astype(BF16)}
    piece = BLOCK // PHASE_SPLIT
    xb_block_split = jnp.concatenate(
        [jnp.concatenate([x_stage[c, pl.ds(blk * BLOCK + ph, piece, stride=PHASE_SPLIT), :]
                          for c in range(x_stage.shape[0])], axis=1)
         for blk in range(tm // BLOCK) for ph in range(PHASE_SPLIT)], axis=0).astype(BF16)
    for kind in range(3):
        col = kind * QKV_WIDTH
        first = jnp.dot(xb_block_split if kind == 0 else xb[1], w_ref[:, col:col + ATTN_WIDTH],
                        preferred_element_type=F32)
        rest = jnp.dot(xb[PHASE_SPLIT], w_ref[:, col + ATTN_WIDTH:col + QKV_WIDTH], preferred_element_type=F32)
        by_group = [first] + [rest[:, (g - 1) * ATTN_WIDTH:g * ATTN_WIDTH] for g in range(1, N_GROUPS)]
        for g, (_, r) in enumerate(GROUP_PATTERNS):
            res = by_group[g]
            if kind == 0:
                res = res * Q_SCALE
            o_ref = out_refs[kind * N_GROUPS + g]
            if r == 1:
                o_ref[0] = res.astype(BF16)
            elif r == PHASE_SPLIT:
                for q in range(r):
                    o_ref[q] = res[q * sub:(q + 1) * sub].astype(BF16)
            else:
                for c, tile in enumerate(_lane_tiles(res)):
                    stage[c] = tile
                for q in range(PHASE_SPLIT):
                    for s in range(PHASE_SPLIT):
                        o_ref[q + PHASE_SPLIT * s] = jnp.concatenate(
                            [stage[c, pl.ds(q * sub + s, sub // PHASE_SPLIT, stride=PHASE_SPLIT), :]
                             for c in range(HEAD_PAIRS)], axis=1).astype(BF16)


def _qkv_proj(x2, w_in, bsz, seq, tm=1024):
    n, d = x2.shape
    tiles = seq // tm
    out_specs, out_shape = [], []
    for _ in range(3):
        for _, r in GROUP_PATTERNS:
            out_specs.append(pl.BlockSpec((None, r, tm // r, ATTN_WIDTH),
                                          lambda i: (i // tiles, 0, i % tiles, 0)))
            out_shape.append(jax.ShapeDtypeStruct((bsz, r, seq // r, ATTN_WIDTH), BF16))
    return pl.pallas_call(
        _qkv_kernel,
        grid=(n // tm,),
        in_specs=[pl.BlockSpec((tm, d), lambda i: (i, 0)), _col_block_spec(d, 3 * QKV_WIDTH, 0)],
        out_specs=out_specs,
        out_shape=out_shape,
        scratch_shapes=[pltpu.VMEM((d // LANES, tm, LANES), F32), pltpu.VMEM((HEAD_PAIRS, tm, LANES), F32)],
        compiler_params=pltpu.CompilerParams(dimension_semantics=("arbitrary",),
                                             vmem_limit_bytes=VMEM_LIMIT),
        name="qkv_proj",
    )(x2, w_in)


def _attn_bias():
    slopes = 2.0 ** (-8.0 * np.arange(1, N_SLOTS + 1) / N_SLOTS)
    qi = np.arange(BLOCK)[:, None]
    kj = np.arange(2 * BLOCK)[None, :]
    dist = BLOCK + qi - kj
    valid = (dist >= 0) & (dist <= BLOCK)
    valid = np.stack([valid, valid & (kj >= BLOCK)])
    out = []
    for _, r in GROUP_PATTERNS:
        alibi = (-LOG2E * slopes)[:, None, None, None] * (dist * r)[None, None]
        bias = np.where(valid[None], alibi, MASK_VALUE)
        out.append(bias[:, :, BLOCK_ROW_TOKEN, :] if r == 1 else bias)
    return jnp.asarray(np.stack(out).astype(np.float32))


def _attn_kernel(*refs):
    ins, bias_ref, out_ref = refs[:5 * N_GROUPS], refs[5 * N_GROUPS], refs[5 * N_GROUPS + 1]
    o_stage, l_stage, m_stage, nat_stage = refs[5 * N_GROUPS + 2:]
    first_span = pl.program_id(2) == 0
    lane = lax.broadcasted_iota(jnp.int32, (BLOCK, LANES), 1)
    low = lane < HEAD_DIM

    for g, (_, r) in enumerate(GROUP_PATTERNS):
        q_ref, kc_ref, kp_ref, vc_ref, vp_ref = ins[5 * g:5 * g + 5]
        nb = UNITS // r

        for p, n in ((p, n) for p in range(r) for n in range(nb)):
            q = q_ref[p, n * BLOCK:(n + 1) * BLOCK, :]
            if n == 0:
                kcat = jnp.concatenate([kp_ref[p], kc_ref[p, :BLOCK, :]], axis=0)
                vcat = jnp.concatenate([vp_ref[p], vc_ref[p, :BLOCK, :]], axis=0)
                variant = jnp.where(first_span, 1, 0)
            else:
                kcat = kc_ref[p, (n - 1) * BLOCK:(n + 1) * BLOCK, :]
                vcat = vc_ref[p, (n - 1) * BLOCK:(n + 1) * BLOCK, :]
                variant = 0
            v_ext = jnp.concatenate([vcat, jnp.ones_like(vcat)], axis=1)
            qm = jnp.concatenate([jnp.where(low, q, jnp.zeros_like(q)),
                                  jnp.where(low, jnp.zeros_like(q), q)], axis=0)
            s_both = lax.dot_general(qm, kcat, (((1,), (1,)), ((), ())), preferred_element_type=F32)
            es, tops = [], []
            for hh in range(2):
                s = s_both[hh * BLOCK:(hh + 1) * BLOCK] + bias_ref[g, hh, variant]
                m = jnp.max(s, axis=1, keepdims=True)
                es.append(jnp.exp2(s - m).astype(BF16))
                tops.append(m)
            pv = jnp.dot(jnp.concatenate(es, axis=0), v_ext, preferred_element_type=F32)
            staged = ((o_stage, jnp.where(low, pv[:BLOCK, :LANES], pv[BLOCK:, :LANES])),
                      (l_stage, jnp.where(low, pv[:BLOCK, LANES:], pv[BLOCK:, LANES:])),
                      (m_stage, jnp.where(low, tops[0], tops[1])))
            for stage, val in staged:
                if r == 1:
                    piece = BLOCK // PHASE_SPLIT
                    for ph in range(PHASE_SPLIT):
                        dst = ph * QUARTER + n * piece
                        stage[g, dst:dst + piece, :] = val[ph * piece:(ph + 1) * piece]
                elif r == PHASE_SPLIT:
                    dst = p * QUARTER + n * BLOCK
                    stage[g, dst:dst + BLOCK, :] = val
                else:
                    start = (p % PHASE_SPLIT) * QUARTER + p // PHASE_SPLIT
                    stage[g, pl.ds(start, BLOCK, stride=PHASE_SPLIT), :] = val

    def mix(i, carry):
        rows = pl.ds(pl.multiple_of(i * BLOCK, BLOCK), BLOCK)

        tops = [m_stage[g, rows, :] for g in range(N_GROUPS)]
        top = functools.reduce(jnp.maximum, tops)
        weights = [jnp.exp2(tops[g] - top) for g in range(N_GROUPS)]
        num = sum(weights[g] * o_stage[g, rows, :] for g in range(N_GROUPS))
        den = sum(weights[g] * l_stage[g, rows, :] for g in range(N_GROUPS))
        start = (i % PHASE_SPLIT) * (PHASE_SPLIT * BLOCK) + i // PHASE_SPLIT
        nat_stage[pl.ds(start, BLOCK, stride=PHASE_SPLIT), :] = num / den
        return carry

    lax.fori_loop(0, UNITS, mix, 0, unroll=4)
    out_ref[...] = nat_stage[...].astype(BF16)


def _attention(qkv, bsz, seq):
    spans = seq // SPAN
    in_specs, args = [], []
    for g, (_, r) in enumerate(GROUP_PATTERNS):
        nb = UNITS // r
        cur = pl.BlockSpec((None, r, SPAN // r, LANES), lambda hp, b, s: (b, 0, s, hp))
        prev = pl.BlockSpec((None, r, BLOCK, LANES),
                            lambda hp, b, s, nb=nb: (b, 0, jnp.maximum(nb * s - 1, 0), hp))
        q, k, v = qkv[g], qkv[N_GROUPS + g], qkv[2 * N_GROUPS + g]
        in_specs += [cur, cur, prev, cur, prev]
        args += [q, k, k, v, v]
    in_specs.append(pl.BlockSpec((N_GROUPS, 2, 2, BLOCK, 2 * BLOCK), lambda hp, b, s: (0, hp, 0, 0, 0)))
    scratch = [pltpu.VMEM((N_GROUPS, SPAN, LANES), F32)] * 3
    scratch += [pltpu.VMEM((SPAN, LANES), F32)]
    return pl.pallas_call(
        _attn_kernel,
        grid=(HEAD_PAIRS, bsz, spans),
        in_specs=in_specs,
        out_specs=pl.BlockSpec((SPAN, LANES), lambda hp, b, s: (b * spans + s, hp)),
        out_shape=jax.ShapeDtypeStruct((bsz * seq, ATTN_WIDTH), BF16),
        scratch_shapes=scratch,
        compiler_params=pltpu.CompilerParams(dimension_semantics=("arbitrary",) * 3,
                                             vmem_limit_bytes=VMEM_LIMIT),
        name="band_attn",
    )(*args, _attn_bias())


def _ssm_weights(lam_re, lam_im, log_dt, b_re, b_im, c_re, c_im):
    t = SSM_CHUNK
    lr, li = lam_re.astype(F32), lam_im.astype(F32)
    dt = jnp.exp(log_dt.astype(F32))[:, None]
    mag = jnp.exp(lr * dt)
    ab_re, ab_im = mag * jnp.cos(li * dt), mag * jnp.sin(li * dt)
    nr, ni = ab_re - 1.0, ab_im
    den = lr * lr + li * li
    coef_re = (nr * lr + ni * li) / den
    coef_im = (ni * lr - nr * li) / den
    br, bi = b_re.astype(F32), b_im.astype(F32)
    bb_re = coef_re[..., None] * br - coef_im[..., None] * bi
    bb_im = coef_re[..., None] * bi + coef_im[..., None] * br
    pw_re, pw_im = [jnp.ones_like(ab_re)], [jnp.zeros_like(ab_im)]
    for _ in range(t):
        pr, pi = pw_re[-1], pw_im[-1]
        pw_re.append(pr * ab_re - pi * ab_im)
        pw_im.append(pr * ab_im + pi * ab_re)
    pw_re, pw_im = jnp.stack(pw_re), jnp.stack(pw_im)
    cr, ci = c_re.astype(F32), c_im.astype(F32)
    n_ch = SSM_GROUPS * SSM_GROUP

    xb_re = pw_re[:t, :, :, None] * bb_re[None] - pw_im[:t, :, :, None] * bb_im[None]
    xb_im = pw_re[:t, :, :, None] * bb_im[None] + pw_im[:t, :, :, None] * bb_re[None]
    xb = jnp.stack([xb_re, xb_im]).transpose(0, 1, 2, 4, 3).reshape(2, t, n_ch, SSM_STATE)
    xb = jnp.concatenate([xb, xb], axis=-1)
    ca_re = cr[None] * pw_re[:, :, None, :] - ci[None] * pw_im[:, :, None, :]
    ca_im = cr[None] * pw_im[:, :, None, :] + ci[None] * pw_re[:, :, None, :]
    ca = jnp.stack([ca_re, -ca_im]).transpose(0, 1, 4, 2, 3).reshape(2, t + 1, SSM_STATE, n_ch)

    w_y, w_st = pl.pallas_call(
        _ssm_expand_kernel,
        grid=(SSM_SLABS,),
        in_specs=[pl.BlockSpec((2, t, LANES, LANES), lambda j: (0, 0, j, 0)),
                  pl.BlockSpec((2, t + 1, SSM_STATE, LANES), lambda j: (0, 0, 0, j))],
        out_specs=[pl.BlockSpec((None, t * LANES + 2 * SLAB_STATE, t * LANES), lambda j: (j, 0, 0)),
                   pl.BlockSpec((None, t * LANES, 2 * SLAB_STATE), lambda j: (j, 0, 0))],
        out_shape=[jax.ShapeDtypeStruct((SSM_SLABS, t * LANES + 2 * SLAB_STATE, t * LANES), BF16),
                   jax.ShapeDtypeStruct((SSM_SLABS, t * LANES, 2 * SLAB_STATE), BF16)],
        compiler_params=pltpu.CompilerParams(dimension_semantics=("arbitrary",),
                                             vmem_limit_bytes=VMEM_LIMIT),
        name="s5_expand",
    )(xb, ca)
    a_t = jnp.stack([pw_re[t].reshape(-1), pw_im[t].reshape(-1)])
    return w_y, w_st, a_t


def _ssm_expand_kernel(xb_ref, ca_ref, wy_ref, wst_ref):
    t = SSM_CHUNK
    keep_st = (lax.broadcasted_iota(jnp.int32, (LANES, SLAB_STATE), 0) // SSM_GROUP
               == lax.broadcasted_iota(jnp.int32, (LANES, SLAB_STATE), 1) // SSM_STATE)
    keep_out = (lax.broadcasted_iota(jnp.int32, (SLAB_STATE, LANES), 0) // SSM_STATE
                == lax.broadcasted_iota(jnp.int32, (SLAB_STATE, LANES), 1) // SSM_GROUP)

    def to_state(d):
        parts = [jnp.where(keep_st, jnp.concatenate([xb_ref[ri, d]] * (SLAB_STATE // LANES), axis=1), 0.0)
                 for ri in range(2)]
        return jnp.concatenate(parts, axis=1)

    def from_state(d):
        parts = [jnp.where(keep_out, jnp.concatenate([ca_ref[ri, d]] * GROUPS_PER_SLAB, axis=0), 0.0)
                 for ri in range(2)]
        return jnp.concatenate(parts, axis=0)

    c0 = from_state(0)
    zero = jnp.zeros((LANES, LANES), BF16)
    for d in range(t):
        st = to_state(d)
        sg = t - 1 - d
        wst_ref[sg * LANES:(sg + 1) * LANES, :] = st.astype(BF16)
        taps = jnp.dot(st, c0, preferred_element_type=F32, precision=lax.Precision.HIGHEST).astype(BF16)
        for sg in range(t - d):
            wy_ref[sg * LANES:(sg + 1) * LANES, (sg + d) * LANES:(sg + d + 1) * LANES] = taps
        if d:
            for tau in range(t - d):
                wy_ref[(tau + d) * LANES:(tau + d + 1) * LANES, tau * LANES:(tau + 1) * LANES] = zero
    for tau in range(t):
        wy_ref[t * LANES:, tau * LANES:(tau + 1) * LANES] = from_state(tau + 1).astype(BF16)


def _ssm_kernel(x_ref, wuz_ref, wy_ref, wst_ref, at_ref, dskip_ref, wglu_ref, bglu_ref,
                out_ref, u_st, z_scr, y_st, carry_scr, *state_scrs):
    s_scrs, xp_scrs = state_scrs[:SSM_SLABS], state_scrs[SSM_SLABS:]
    bsz, tl, d_model = x_ref.shape
    t = SSM_CHUNK
    ch = tl // t
    rows = bsz * tl
    crows = bsz * ch
    tiles_per_slab = SLAB_STATE // LANES
    n_tiles = SSM_SLABS * tiles_per_slab
    pitch = s_scrs[0].shape[1] // bsz

    @pl.when(pl.program_id(0) == 0)
    def _():
        carry_scr[...] = jnp.zeros_like(carry_scr)

    xb = x_ref[...].reshape(rows, d_model).astype(BF16)
    uz = jnp.dot(xb, wuz_ref[...], preferred_element_type=F32)
    for c, tile in enumerate(_lane_tiles(uz[:, :SSM_WIDTH])):
        u_st[c] = tile
    z_scr[...] = uz[:, SSM_WIDTH:]

    def chunk_lhs(j):
        return jnp.concatenate([u_st[j, pl.ds(sg, crows, stride=t), :] for sg in range(t)], axis=1).astype(BF16)

    n_slab_tiles = 2 * tiles_per_slab

    def state_increments(j):
        st = jnp.dot(chunk_lhs(j), wst_ref[j], preferred_element_type=F32)
        for c, tile in enumerate(_lane_tiles(st)):
            for b in range(bsz):
                s_scrs[j][c, b * pitch:b * pitch + ch, :] = tile[b * ch:(b + 1) * ch]

    def recurrence(j):
        s_scr, xp_scr = s_scrs[j], xp_scrs[j]
        for c_r in range(tiles_per_slab):
            c_i = tiles_per_slab + c_r
            g_r, g_i = j * tiles_per_slab + c_r, n_tiles + j * tiles_per_slab + c_r
            a_re, a_im = at_ref[g_r:g_r + 1, :], at_ref[g_i:g_i + 1, :]
            x_re, x_im = carry_scr[g_r], carry_scr[g_i]
            for k in range(ch):
                sel = pl.ds(k, bsz, stride=pitch)
                xp_scr[c_r, sel, :] = x_re
                xp_scr[c_i, sel, :] = x_im
                x_re, x_im = (a_re * x_re - a_im * x_im + s_scr[c_r, sel, :],
                              a_re * x_im + a_im * x_re + s_scr[c_i, sel, :])
            carry_scr[g_r] = x_re
            carry_scr[g_i] = x_im

    def response(j):
        xin = [jnp.concatenate([xp_scrs[j][c, b * pitch:b * pitch + ch, :] for b in range(bsz)],
                               axis=0).astype(BF16) for c in range(n_slab_tiles)]
        yj = jnp.dot(jnp.concatenate([chunk_lhs(j)] + xin, axis=1), wy_ref[j], preferred_element_type=F32)
        for tau, tile in enumerate(_lane_tiles(yj)):
            y_st[j, pl.ds(tau, crows, stride=t), :] = tile

    state_increments(0)
    for j in range(SSM_SLABS):
        if j + 1 < SSM_SLABS:
            state_increments(j + 1)
        recurrence(j)
        if j:
            response(j - 1)
    response(SSM_SLABS - 1)

    for b in range(bsz):
        rs = slice(b * tl, (b + 1) * tl)
        y = jnp.concatenate([y_st[c, rs, :] for c in range(SSM_SLABS)], axis=1)
        u = jnp.concatenate([u_st[c, rs, :] for c in range(SSM_SLABS)], axis=1)
        y = jax.nn.gelu(y + dskip_ref[...] * u)
        gate = _sigmoid(jnp.dot(y.astype(BF16), wglu_ref[...], preferred_element_type=F32) + bglu_ref[...])
        out_ref[b] = (y * gate * _silu(z_scr[rs, :])).astype(BF16)


def _ssm_branch(x, w_in, ssm_w, d_skip, w_glu, b_glu, tl=128):
    bsz, seq, d = x.shape
    u_col = 3 * QKV_WIDTH + ATTN_WIDTH
    t = SSM_CHUNK
    rows, ch = bsz * tl, tl // t
    pitch = -(-ch // PHASE_SPLIT) * PHASE_SPLIT
    pitch += PHASE_SPLIT * (1 - (pitch // PHASE_SPLIT) % 2)
    w_y, w_st, a_t = ssm_w
    n_tiles2 = 2 * SSM_SLABS * SLAB_STATE // LANES
    a_t = a_t.reshape(n_tiles2, LANES)
    out = pl.pallas_call(
        _ssm_kernel,
        grid=(seq // tl,),
        in_specs=[pl.BlockSpec((bsz, tl, d), lambda i: (0, i, 0)),
                  _col_block_spec(d, 2 * SSM_WIDTH, u_col),
                  _const_spec(w_y.shape),
                  _const_spec(w_st.shape), _const_spec(a_t.shape), _const_spec((1, SSM_WIDTH)),
                  _const_spec(w_glu.shape), _const_spec((1, SSM_WIDTH))],
        out_specs=pl.BlockSpec((bsz, tl, SSM_WIDTH), lambda i: (0, i, 0)),
        out_shape=jax.ShapeDtypeStruct((bsz, seq, SSM_WIDTH), BF16),
        scratch_shapes=[pltpu.VMEM((SSM_SLABS, rows, LANES), F32),
                        pltpu.VMEM((rows, SSM_WIDTH), F32),
                        pltpu.VMEM((SSM_SLABS, rows, LANES), F32),
                        pltpu.VMEM((n_tiles2, bsz, LANES), F32)]
        + [pltpu.VMEM((n_tiles2 // SSM_SLABS, bsz * pitch, LANES), F32)] * (2 * SSM_SLABS),
        compiler_params=pltpu.CompilerParams(dimension_semantics=("arbitrary",),
                                             vmem_limit_bytes=VMEM_LIMIT),
        name="s5_branch",
    )(x, w_in, w_y, w_st, a_t, d_skip.reshape(1, SSM_WIDTH).astype(F32),
      w_glu, b_glu.reshape(1, SSM_WIDTH).astype(F32))
    return out.reshape(bsz * seq, SSM_WIDTH)


def _merge_kernel(alpha, x_ref, attn_ref, ys_ref, wz_ref, wg_ref, wa_ref, ws_ref, wo_ref, g_ref, b_ref, out_ref):
    tm, d_model = x_ref.shape
    y_s = jnp.dot(ys_ref[...], ws_ref[...], preferred_element_type=F32)
    xb = x_ref[...].astype(BF16)
    z_a = jnp.dot(xb, wz_ref[...], preferred_element_type=F32)
    gates = jnp.dot(xb, wg_ref[...], preferred_element_type=F32)
    gate_a = gates[:, :d_model]
    gate_s = gates[:, d_model:]
    attn = attn_ref[...].astype(F32) * _silu(z_a)
    y_a = jnp.dot(attn.astype(BF16), wa_ref[...], preferred_element_type=F32)
    merged = (_sigmoid(gate_a) * y_a + _sigmoid(gate_s) * y_s).astype(BF16)
    chunk = tm // LN_CHUNKS
    for i in range(LN_CHUNKS):
        rows = slice(i * chunk, (i + 1) * chunk)
        out = jnp.dot(merged[rows], wo_ref[...], preferred_element_type=F32)
        h = alpha * x_ref[rows, :] + out
        mu = jnp.mean(h, axis=-1, keepdims=True)
        hc = h - mu
        var = jnp.mean(hc * hc, axis=-1, keepdims=True)
        out_ref[rows, :] = hc * lax.rsqrt(var + LN_EPS) * g_ref[...] + b_ref[...]


def _merge(x2, attn, ys, w_in, w_attn_up, w_ssm_up, w_o, ln_g, ln_b, alpha, tm=1024):
    n, d = x2.shape
    row = lambda width: pl.BlockSpec((tm, width), lambda i: (i, 0))
    z_col = 3 * QKV_WIDTH
    gate_col = z_col + ATTN_WIDTH + 2 * SSM_WIDTH
    return pl.pallas_call(
        functools.partial(_merge_kernel, alpha),
        grid=(n // tm,),
        in_specs=[row(d), row(ATTN_WIDTH), row(SSM_WIDTH),
                  _col_block_spec(d, ATTN_WIDTH, z_col), _col_block_spec(d, 2 * d, gate_col),
                  _const_spec(w_attn_up.shape), _const_spec(w_ssm_up.shape),
                  _const_spec(w_o.shape), _const_spec((1, d)), _const_spec((1, d))],
        out_specs=row(d),
        out_shape=jax.ShapeDtypeStruct((n, d), F32),
        compiler_params=pltpu.CompilerParams(dimension_semantics=("arbitrary",),
                                             vmem_limit_bytes=VMEM_LIMIT),
        name="merge_out_ln",
    )(x2, attn, ys, w_in, w_in, w_attn_up, w_ssm_up, w_o,
      ln_g.reshape(1, d).astype(F32), ln_b.reshape(1, d).astype(F32))


def _layer(h, w_in, lam_re, lam_im, log_dt, b_re, b_im, c_re, c_im, d_skip,
           w_glu, b_glu, w_attn_up, w_ssm_up, w_o, ln_g, ln_b, alpha):
    bsz, seq, d = h.shape
    assert seq % SPAN == 0 and all(w // r == BLOCK for w, r in GROUP_PATTERNS)
    assert tuple(r for _, r in GROUP_PATTERNS) == (1, PHASE_SPLIT, PHASE_SPLIT * PHASE_SPLIT)
    x2 = h.reshape(bsz * seq, d)
    w_in = w_in.astype(BF16)

    qkv = _qkv_proj(x2, w_in, bsz, seq)
    attn = _attention(qkv, bsz, seq)
    ssm_w = _ssm_weights(lam_re, lam_im, log_dt, b_re, b_im, c_re, c_im)
    ys = _ssm_branch(h, w_in, ssm_w, d_skip, w_glu.astype(BF16), b_glu)
    out = _merge(x2, attn, ys, w_in, w_attn_up.astype(BF16), w_ssm_up.astype(BF16),
                 w_o.astype(BF16), ln_g, ln_b, alpha)
    return out.reshape(bsz, seq, d)


def kernel(x, w_in, lam_re, lam_im, log_dt, b_re, b_im, c_re, c_im, d_skip,
           w_glu, b_glu, w_attn_up, w_ssm_up, w_o, ln_g, ln_b):
    depth = w_in.shape[0]
    alpha = (2.0 * depth) ** 0.25
    h = x
    for layer in range(depth):
        h = _layer(h, w_in[layer], lam_re[layer], lam_im[layer], log_dt[layer], b_re[layer], b_im[layer],
                   c_re[layer], c_im[layer], d_skip[layer], w_glu[layer], b_glu[layer], w_attn_up[layer],
                   w_ssm_up[layer], w_o[layer], ln_g[layer], ln_b[layer], alpha)
    return h
```

```python
import functools
import math

import jax
import jax.numpy as jnp
import numpy as np
from jax import lax
from jax.experimental import pallas as pl
from jax.experimental.pallas import tpu as pltpu

F32 = jnp.float32
BF16 = jnp.bfloat16

HEAD_DIM = 64
N_SLOTS = 8
GROUP_PATTERNS = ((128, 1), (512, 4), (2048, 16))
N_GROUPS = len(GROUP_PATTERNS)
ATTN_WIDTH = N_SLOTS * HEAD_DIM
QKV_WIDTH = N_GROUPS * ATTN_WIDTH
BLOCK = 128
SPAN = GROUP_PATTERNS[-1][1] * BLOCK
UNITS = SPAN // BLOCK
SSM_WIDTH = 512
SSM_GROUP = 16
SSM_GROUPS = SSM_WIDTH // SSM_GROUP
SSM_STATE = 64
LN_EPS = 1e-5
MASK_VALUE = -1e30
LOG2E = math.log2(math.e)
Q_SCALE = LOG2E / math.sqrt(HEAD_DIM)

LANES = 128
HEAD_PAIRS = ATTN_WIDTH // LANES
PHASE_SPLIT = 4
QUARTER = SPAN // PHASE_SPLIT
BLOCK_ROW_TOKEN = [PHASE_SPLIT * k + ph for ph in range(PHASE_SPLIT) for k in range(BLOCK // PHASE_SPLIT)]
SSM_CHUNK = PHASE_SPLIT
SSM_SLABS = SSM_WIDTH // LANES
GROUPS_PER_SLAB = LANES // SSM_GROUP
SLAB_STATE = GROUPS_PER_SLAB * SSM_STATE
LN_CHUNKS = 4
V7X_VMEM_BYTES = 64 * 1024 * 1024
VMEM_LIMIT = V7X_VMEM_BYTES - 8 * 1024 * 1024


def _const_spec(shape):
    nd = len(shape)
    return pl.BlockSpec(shape, lambda *_: (0,) * nd, pipeline_mode=pl.Buffered(1))


def _col_block_spec(rows, width, start):
    assert start % width == 0
    return pl.BlockSpec((rows, width), lambda *_: (0, start // width), pipeline_mode=pl.Buffered(1))


def _sigmoid(x):
    return 0.5 * jnp.tanh(0.5 * x) + 0.5


def _silu(x):
    return x * _sigmoid(x)


def _lane_tiles(x):
    return [x[:, c * LANES:(c + 1) * LANES] for c in range(x.shape[1] // LANES)]


def _qkv_kernel(x_ref, w_ref, *refs):
    out_refs, x_stage, stage = refs[:-2], refs[-2], refs[-1]
    tm = x_ref.shape[0]
    sub = tm // PHASE_SPLIT
    x = x_ref[...]
    for c, tile in enumerate(_lane_tiles(x)):
        x_stage[c] = tile
    xb = {1: x.astype(BF16),
          PHASE_SPLIT: jnp.concatenate(
              [jnp.concatenate([x_stage[c, pl.ds(q, sub, stride=PHASE_SPLIT), :] for c in range(x_stage.shape[0])],
                               axis=1) for q in range(PHASE_SPLIT)], axis=0).astype(BF16)}
    piece = BLOCK // PHASE_SPLIT
    xb_block_split = jnp.concatenate(
        [jnp.concatenate([x_stage[c, pl.ds(blk * BLOCK + ph, piece, stride=PHASE_SPLIT), :]
                          for c in range(x_stage.shape[0])], axis=1)
         for blk in range(tm // BLOCK) for ph in range(PHASE_SPLIT)], axis=0).astype(BF16)
    for kind in range(3):
        col = kind * QKV_WIDTH
        first = jnp.dot(xb_block_split if kind == 0 else xb[1], w_ref[:, col:col + ATTN_WIDTH],
                        preferred_element_type=F32)
        rest = jnp.dot(xb[PHASE_SPLIT], w_ref[:, col + ATTN_WIDTH:col + QKV_WIDTH], preferred_element_type=F32)
        by_group = [first] + [rest[:, (g - 1) * ATTN_WIDTH:g * ATTN_WIDTH] for g in range(1, N_GROUPS)]
        for g, (_, r) in enumerate(GROUP_PATTERNS):
            res = by_group[g]
            if kind == 0:
                res = res * Q_SCALE
            o_ref = out_refs[kind * N_GROUPS + g]
            if r == 1:
                o_ref[0] = res.astype(BF16)
            elif r == PHASE_SPLIT:
                for q in range(r):
                    o_ref[q] = res[q * sub:(q + 1) * sub].astype(BF16)
            else:
                for c, tile in enumerate(_lane_tiles(res)):
                    stage[c] = tile
                for q in range(PHASE_SPLIT):
                    for s in range(PHASE_SPLIT):
                        o_ref[q + PHASE_SPLIT * s] = jnp.concatenate(
                            [stage[c, pl.ds(q * sub + s, sub // PHASE_SPLIT, stride=PHASE_SPLIT), :]
                             for c in range(HEAD_PAIRS)], axis=1).astype(BF16)


def _qkv_proj(x2, w_in, bsz, seq, tm=1024):
    n, d = x2.shape
    tiles = seq // tm
    out_specs, out_shape = [], []
    for _ in range(3):
        for _, r in GROUP_PATTERNS:
            out_specs.append(pl.BlockSpec((None, r, tm // r, ATTN_WIDTH),
                                          lambda i: (i // tiles, 0, i % tiles, 0)))
            out_shape.append(jax.ShapeDtypeStruct((bsz, r, seq // r, ATTN_WIDTH), BF16))
    return pl.pallas_call(
        _qkv_kernel,
        grid=(n // tm,),
        in_specs=[pl.BlockSpec((tm, d), lambda i: (i, 0)), _col_block_spec(d, 3 * QKV_WIDTH, 0)],
        out_specs=out_specs,
        out_shape=out_shape,
        scratch_shapes=[pltpu.VMEM((d // LANES, tm, LANES), F32), pltpu.VMEM((HEAD_PAIRS, tm, LANES), F32)],
        compiler_params=pltpu.CompilerParams(dimension_semantics=("arbitrary",),
                                             vmem_limit_bytes=VMEM_LIMIT),
        name="qkv_proj",
    )(x2, w_in)


def _attn_bias():
    slopes = 2.0 ** (-8.0 * np.arange(1, N_SLOTS + 1) / N_SLOTS)
    qi = np.arange(BLOCK)[:, None]
    kj = np.arange(2 * BLOCK)[None, :]
    dist = BLOCK + qi - kj
    valid = (dist >= 0) & (dist <= BLOCK)
    valid = np.stack([valid, valid & (kj >= BLOCK)])
    out = []
    for _, r in GROUP_PATTERNS:
        alibi = (-LOG2E * slopes)[:, None, None, None] * (dist * r)[None, None]
        bias = np.where(valid[None], alibi, MASK_VALUE)
        out.append(bias[:, :, BLOCK_ROW_TOKEN, :] if r == 1 else bias)
    return jnp.asarray(np.stack(out).astype(np.float32))


def _attn_kernel(*refs):
    ins, bias_ref, out_ref = refs[:5 * N_GROUPS], refs[5 * N_GROUPS], refs[5 * N_GROUPS + 1]
    o_stage, l_stage, m_stage = refs[5 * N_GROUPS + 2:]
    first_span = pl.program_id(2) == 0
    lane = lax.broadcasted_iota(jnp.int32, (BLOCK, LANES), 1)
    low = lane < HEAD_DIM

    for g, (_, r) in enumerate(GROUP_PATTERNS):
        q_ref, kc_ref, kp_ref, vc_ref, vp_ref = ins[5 * g:5 * g + 5]
        nb = UNITS // r

        for p, n in ((p, n) for p in range(r) for n in range(nb)):
            q = q_ref[p, n * BLOCK:(n + 1) * BLOCK, :]
            if n == 0:
                kcat = jnp.concatenate([kp_ref[p], kc_ref[p, :BLOCK, :]], axis=0)
                vcat = jnp.concatenate([vp_ref[p], vc_ref[p, :BLOCK, :]], axis=0)
                variant = jnp.where(first_span, 1, 0)
            else:
                kcat = kc_ref[p, (n - 1) * BLOCK:(n + 1) * BLOCK, :]
                vcat = vc_ref[p, (n - 1) * BLOCK:(n + 1) * BLOCK, :]
                variant = 0
            v_ext = jnp.concatenate([vcat, jnp.ones_like(vcat)], axis=1)
            qm = jnp.concatenate([jnp.where(low, q, jnp.zeros_like(q)),
                                  jnp.where(low, jnp.zeros_like(q), q)], axis=0)
            s_both = lax.dot_general(qm, kcat, (((1,), (1,)), ((), ())), preferred_element_type=F32)
            es, tops = [], []
            for hh in range(2):
                s = s_both[hh * BLOCK:(hh + 1) * BLOCK] + bias_ref[g, hh, variant]
                m = jnp.max(s, axis=1, keepdims=True)
                es.append(jnp.exp2(s - m).astype(BF16))
                tops.append(m)
            pv = jnp.dot(jnp.concatenate(es, axis=0), v_ext, preferred_element_type=F32)
            staged = ((o_stage, jnp.where(low, pv[:BLOCK, :LANES], pv[BLOCK:, :LANES])),
                      (l_stage, jnp.where(low, pv[:BLOCK, LANES:], pv[BLOCK:, LANES:])),
                      (m_stage, jnp.where(low, tops[0], tops[1])))
            for stage, val in staged:
                if r == 1:
                    piece = BLOCK // PHASE_SPLIT
                    for ph in range(PHASE_SPLIT):
                        dst = ph * QUARTER + n * piece
                        stage[g, dst:dst + piece, :] = val[ph * piece:(ph + 1) * piece]
                elif r == PHASE_SPLIT:
                    dst = p * QUARTER + n * BLOCK
                    stage[g, dst:dst + BLOCK, :] = val
                else:
                    start = (p % PHASE_SPLIT) * QUARTER + p // PHASE_SPLIT
                    stage[g, pl.ds(start, BLOCK, stride=PHASE_SPLIT), :] = val

    def mix(i, carry):
        rows = pl.ds(pl.multiple_of(i * BLOCK, BLOCK), BLOCK)

        tops = [m_stage[g, rows, :] for g in range(N_GROUPS)]
        top = functools.reduce(jnp.maximum, tops)
        weights = [jnp.exp2(tops[g] - top) for g in range(N_GROUPS)]
        num = sum(weights[g] * o_stage[g, rows, :] for g in range(N_GROUPS))
        den = sum(weights[g] * l_stage[g, rows, :] for g in range(N_GROUPS))
        start = (i % PHASE_SPLIT) * (PHASE_SPLIT * BLOCK) + i // PHASE_SPLIT
        out_ref[pl.ds(start, BLOCK, stride=PHASE_SPLIT), :] = num / den
        return carry

    lax.fori_loop(0, UNITS, mix, 0, unroll=4)


def _attention(qkv, bsz, seq):
    spans = seq // SPAN
    in_specs, args = [], []
    for g, (_, r) in enumerate(GROUP_PATTERNS):
        nb = UNITS // r
        cur = pl.BlockSpec((None, r, SPAN // r, LANES), lambda hp, b, s: (b, 0, s, hp))
        prev = pl.BlockSpec((None, r, BLOCK, LANES),
                            lambda hp, b, s, nb=nb: (b, 0, jnp.maximum(nb * s - 1, 0), hp))
        q, k, v = qkv[g], qkv[N_GROUPS + g], qkv[2 * N_GROUPS + g]
        in_specs += [cur, cur, prev, cur, prev]
        args += [q, k, k, v, v]
    in_specs.append(pl.BlockSpec((N_GROUPS, 2, 2, BLOCK, 2 * BLOCK), lambda hp, b, s: (0, hp, 0, 0, 0)))
    scratch = [pltpu.VMEM((N_GROUPS, SPAN, LANES), F32)] * 3
    return pl.pallas_call(
        _attn_kernel,
        grid=(HEAD_PAIRS, bsz, spans),
        in_specs=in_specs,
        out_specs=pl.BlockSpec((SPAN, LANES), lambda hp, b, s: (b * spans + s, hp)),
        out_shape=jax.ShapeDtypeStruct((bsz * seq, ATTN_WIDTH), F32),
        scratch_shapes=scratch,
        compiler_params=pltpu.CompilerParams(dimension_semantics=("arbitrary",) * 3,
                                             vmem_limit_bytes=VMEM_LIMIT),
        name="band_attn",
    )(*args, _attn_bias())


def _ssm_weights(lam_re, lam_im, log_dt, b_re, b_im, c_re, c_im):
    t = SSM_CHUNK
    lr, li = lam_re.astype(F32), lam_im.astype(F32)
    dt = jnp.exp(log_dt.astype(F32))[:, None]
    mag = jnp.exp(lr * dt)
    ab_re, ab_im = mag * jnp.cos(li * dt), mag * jnp.sin(li * dt)
    nr, ni = ab_re - 1.0, ab_im
    den = lr * lr + li * li
    coef_re = (nr * lr + ni * li) / den
    coef_im = (ni * lr - nr * li) / den
    br, bi = b_re.astype(F32), b_im.astype(F32)
    bb_re = coef_re[..., None] * br - coef_im[..., None] * bi
    bb_im = coef_re[..., None] * bi + coef_im[..., None] * br
    pw_re, pw_im = [jnp.ones_like(ab_re)], [jnp.zeros_like(ab_im)]
    for _ in range(t):
        pr, pi = pw_re[-1], pw_im[-1]
        pw_re.append(pr * ab_re - pi * ab_im)
        pw_im.append(pr * ab_im + pi * ab_re)
    pw_re, pw_im = jnp.stack(pw_re), jnp.stack(pw_im)
    cr, ci = c_re.astype(F32), c_im.astype(F32)
    n_ch = SSM_GROUPS * SSM_GROUP

    xb_re = pw_re[:t, :, :, None] * bb_re[None] - pw_im[:t, :, :, None] * bb_im[None]
    xb_im = pw_re[:t, :, :, None] * bb_im[None] + pw_im[:t, :, :, None] * bb_re[None]
    xb = jnp.stack([xb_re, xb_im]).transpose(0, 1, 2, 4, 3).reshape(2, t, n_ch, SSM_STATE)
    xb = jnp.concatenate([xb, xb], axis=-1)
    ca_re = cr[None] * pw_re[:, :, None, :] - ci[None] * pw_im[:, :, None, :]
    ca_im = cr[None] * pw_im[:, :, None, :] + ci[None] * pw_re[:, :, None, :]
    ca = jnp.stack([ca_re, -ca_im]).transpose(0, 1, 4, 2, 3).reshape(2, t + 1, SSM_STATE, n_ch)

    w_y, w_st = pl.pallas_call(
        _ssm_expand_kernel,
        grid=(SSM_SLABS,),
        in_specs=[pl.BlockSpec((2, t, LANES, LANES), lambda j: (0, 0, j, 0)),
                  pl.BlockSpec((2, t + 1, SSM_STATE, LANES), lambda j: (0, 0, 0, j))],
        out_specs=[pl.BlockSpec((None, t * LANES + 2 * SLAB_STATE, t * LANES), lambda j: (j, 0, 0)),
                   pl.BlockSpec((None, t * LANES, 2 * SLAB_STATE), lambda j: (j, 0, 0))],
        out_shape=[jax.ShapeDtypeStruct((SSM_SLABS, t * LANES + 2 * SLAB_STATE, t * LANES), BF16),
                   jax.ShapeDtypeStruct((SSM_SLABS, t * LANES, 2 * SLAB_STATE), BF16)],
        compiler_params=pltpu.CompilerParams(dimension_semantics=("arbitrary",),
                                             vmem_limit_bytes=VMEM_LIMIT),
        name="s5_expand",
    )(xb, ca)
    a_t = jnp.stack([pw_re[t].reshape(-1), pw_im[t].reshape(-1)])
    return w_y, w_st, a_t


def _ssm_expand_kernel(xb_ref, ca_ref, wy_ref, wst_ref):
    t = SSM_CHUNK
    keep_st = (lax.broadcasted_iota(jnp.int32, (LANES, SLAB_STATE), 0) // SSM_GROUP
               == lax.broadcasted_iota(jnp.int32, (LANES, SLAB_STATE), 1) // SSM_STATE)
    keep_out = (lax.broadcasted_iota(jnp.int32, (SLAB_STATE, LANES), 0) // SSM_STATE
                == lax.broadcasted_iota(jnp.int32, (SLAB_STATE, LANES), 1) // SSM_GROUP)

    def to_state(d):
        parts = [jnp.where(keep_st, jnp.concatenate([xb_ref[ri, d]] * (SLAB_STATE // LANES), axis=1), 0.0)
                 for ri in range(2)]
        return jnp.concatenate(parts, axis=1)

    def from_state(d):
        parts = [jnp.where(keep_out, jnp.concatenate([ca_ref[ri, d]] * GROUPS_PER_SLAB, axis=0), 0.0)
                 for ri in range(2)]
        return jnp.concatenate(parts, axis=0)

    c0 = from_state(0)
    zero = jnp.zeros((LANES, LANES), BF16)
    for d in range(t):
        st = to_state(d)
        sg = t - 1 - d
        wst_ref[sg * LANES:(sg + 1) * LANES, :] = st.astype(BF16)
        taps = jnp.dot(st, c0, preferred_element_type=F32, precision=lax.Precision.HIGHEST).astype(BF16)
        for sg in range(t - d):
            wy_ref[sg * LANES:(sg + 1) * LANES, (sg + d) * LANES:(sg + d + 1) * LANES] = taps
        if d:
            for tau in range(t - d):
                wy_ref[(tau + d) * LANES:(tau + d + 1) * LANES, tau * LANES:(tau + 1) * LANES] = zero
    for tau in range(t):
        wy_ref[t * LANES:, tau * LANES:(tau + 1) * LANES] = from_state(tau + 1).astype(BF16)


def _ssm_kernel(x_ref, wuz_ref, wy_ref, wst_ref, at_ref, dskip_ref, wglu_ref, bglu_ref,
                out_ref, u_st, z_scr, y_st, carry_scr, *state_scrs):
    s_scrs, xp_scrs = state_scrs[:SSM_SLABS], state_scrs[SSM_SLABS:]
    bsz, tl, d_model = x_ref.shape
    t = SSM_CHUNK
    ch = tl // t
    rows = bsz * tl
    crows = bsz * ch
    tiles_per_slab = SLAB_STATE // LANES
    n_tiles = SSM_SLABS * tiles_per_slab
    pitch = s_scrs[0].shape[1] // bsz

    @pl.when(pl.program_id(0) == 0)
    def _():
        carry_scr[...] = jnp.zeros_like(carry_scr)

    xb = x_ref[...].reshape(rows, d_model).astype(BF16)
    uz = jnp.dot(xb, wuz_ref[...], preferred_element_type=F32)
    for c, tile in enumerate(_lane_tiles(uz[:, :SSM_WIDTH])):
        u_st[c] = tile
    z_scr[...] = uz[:, SSM_WIDTH:]

    def chunk_lhs(j):
        return jnp.concatenate([u_st[j, pl.ds(sg, crows, stride=t), :] for sg in range(t)], axis=1).astype(BF16)

    n_slab_tiles = 2 * tiles_per_slab

    def state_increments(j):
        st = jnp.dot(chunk_lhs(j), wst_ref[j], preferred_element_type=F32)
        for c, tile in enumerate(_lane_tiles(st)):
            for b in range(bsz):
                s_scrs[j][c, b * pitch:b * pitch + ch, :] = tile[b * ch:(b + 1) * ch]

    def recurrence(j):
        s_scr, xp_scr = s_scrs[j], xp_scrs[j]
        for c_r in range(tiles_per_slab):
            c_i = tiles_per_slab + c_r
            g_r, g_i = j * tiles_per_slab + c_r, n_tiles + j * tiles_per_slab + c_r
            a_re, a_im = at_ref[g_r:g_r + 1, :], at_ref[g_i:g_i + 1, :]
            x_re, x_im = carry_scr[g_r], carry_scr[g_i]
            for k in range(ch):
                sel = pl.ds(k, bsz, stride=pitch)
                xp_scr[c_r, sel, :] = x_re
                xp_scr[c_i, sel, :] = x_im
                x_re, x_im = (a_re * x_re - a_im * x_im + s_scr[c_r, sel, :],
                              a_re * x_im + a_im * x_re + s_scr[c_i, sel, :])
            carry_scr[g_r] = x_re
            carry_scr[g_i] = x_im

    def response(j):
        xin = [jnp.concatenate([xp_scrs[j][c, b * pitch:b * pitch + ch, :] for b in range(bsz)],
                               axis=0).astype(BF16) for c in range(n_slab_tiles)]
        yj = jnp.dot(jnp.concatenate([chunk_lhs(j)] + xin, axis=1), wy_ref[j], preferred_element_type=F32)
        for tau, tile in enumerate(_lane_tiles(yj)):
            y_st[j, pl.ds(tau, crows, stride=t), :] = tile

    state_increments(0)
    for j in range(SSM_SLABS):
        if j + 1 < SSM_SLABS:
            state_increments(j + 1)
        recurrence(j)
        if j:
            response(j - 1)
    response(SSM_SLABS - 1)

    for b in range(bsz):
        rs = slice(b * tl, (b + 1) * tl)
        y = jnp.concatenate([y_st[c, rs, :] for c in range(SSM_SLABS)], axis=1)
        u = jnp.concatenate([u_st[c, rs, :] for c in range(SSM_SLABS)], axis=1)
        y = jax.nn.gelu(y + dskip_ref[...] * u)
        gate = _sigmoid(jnp.dot(y.astype(BF16), wglu_ref[...], preferred_element_type=F32) + bglu_ref[...])
        out_ref[b] = (y * gate * _silu(z_scr[rs, :])).astype(BF16)


def _ssm_branch(x, w_in, ssm_w, d_skip, w_glu, b_glu, tl=128):
    bsz, seq, d = x.shape
    u_col = 3 * QKV_WIDTH + ATTN_WIDTH
    t = SSM_CHUNK
    rows, ch = bsz * tl, tl // t
    pitch = -(-ch // PHASE_SPLIT) * PHASE_SPLIT
    pitch += PHASE_SPLIT * (1 - (pitch // PHASE_SPLIT) % 2)
    w_y, w_st, a_t = ssm_w
    n_tiles2 = 2 * SSM_SLABS * SLAB_STATE // LANES
    a_t = a_t.reshape(n_tiles2, LANES)
    out = pl.pallas_call(
        _ssm_kernel,
        grid=(seq // tl,),
        in_specs=[pl.BlockSpec((bsz, tl, d), lambda i: (0, i, 0)),
                  _col_block_spec(d, 2 * SSM_WIDTH, u_col),
                  _const_spec(w_y.shape),
                  _const_spec(w_st.shape), _const_spec(a_t.shape), _const_spec((1, SSM_WIDTH)),
                  _const_spec(w_glu.shape), _const_spec((1, SSM_WIDTH))],
        out_specs=pl.BlockSpec((bsz, tl, SSM_WIDTH), lambda i: (0, i, 0)),
        out_shape=jax.ShapeDtypeStruct((bsz, seq, SSM_WIDTH), BF16),
        scratch_shapes=[pltpu.VMEM((SSM_SLABS, rows, LANES), F32),
                        pltpu.VMEM((rows, SSM_WIDTH), F32),
                        pltpu.VMEM((SSM_SLABS, rows, LANES), F32),
                        pltpu.VMEM((n_tiles2, bsz, LANES), F32)]
        + [pltpu.VMEM((n_tiles2 // SSM_SLABS, bsz * pitch, LANES), F32)] * (2 * SSM_SLABS),
        compiler_params=pltpu.CompilerParams(dimension_semantics=("arbitrary",),
                                             vmem_limit_bytes=VMEM_LIMIT),
        name="s5_branch",
    )(x, w_in, w_y, w_st, a_t, d_skip.reshape(1, SSM_WIDTH).astype(F32),
      w_glu, b_glu.reshape(1, SSM_WIDTH).astype(F32))
    return out.reshape(bsz * seq, SSM_WIDTH)


def _merge_kernel(alpha, x_ref, attn_ref, ys_ref, wz_ref, wg_ref, wa_ref, ws_ref, wo_ref, g_ref, b_ref, out_ref):
    tm, d_model = x_ref.shape
    y_s = jnp.dot(ys_ref[...], ws_ref[...], preferred_element_type=F32)
    xb = x_ref[...].astype(BF16)
    z_a = jnp.dot(xb, wz_ref[...], preferred_element_type=F32)
    gates = jnp.dot(xb, wg_ref[...], preferred_element_type=F32)
    gate_a = gates[:, :d_model]
    gate_s = gates[:, d_model:]
    attn = attn_ref[...].astype(F32) * _silu(z_a)
    y_a = jnp.dot(attn.astype(BF16), wa_ref[...], preferred_element_type=F32)
    merged = (_sigmoid(gate_a) * y_a + _sigmoid(gate_s) * y_s).astype(BF16)
    chunk = tm // LN_CHUNKS
    for i in range(LN_CHUNKS):
        rows = slice(i * chunk, (i + 1) * chunk)
        out = jnp.dot(merged[rows], wo_ref[...], preferred_element_type=F32)
        h = alpha * x_ref[rows, :] + out
        mu = jnp.mean(h, axis=-1, keepdims=True)
        hc = h - mu
        var = jnp.mean(hc * hc, axis=-1, keepdims=True)
        out_ref[rows, :] = hc * lax.rsqrt(var + LN_EPS) * g_ref[...] + b_ref[...]


def _merge(x2, attn, ys, w_in, w_attn_up, w_ssm_up, w_o, ln_g, ln_b, alpha, tm=1024):
    n, d = x2.shape
    row = lambda width: pl.BlockSpec((tm, width), lambda i: (i, 0))
    z_col = 3 * QKV_WIDTH
    gate_col = z_col + ATTN_WIDTH + 2 * SSM_WIDTH
    return pl.pallas_call(
        functools.partial(_merge_kernel, alpha),
        grid=(n // tm,),
        in_specs=[row(d), row(ATTN_WIDTH), row(SSM_WIDTH),
                  _col_block_spec(d, ATTN_WIDTH, z_col), _col_block_spec(d, 2 * d, gate_col),
                  _const_spec(w_attn_up.shape), _const_spec(w_ssm_up.shape),
                  _const_spec(w_o.shape), _const_spec((1, d)), _const_spec((1, d))],
        out_specs=row(d),
        out_shape=jax.ShapeDtypeStruct((n, d), F32),
        compiler_params=pltpu.CompilerParams(dimension_semantics=("arbitrary",),
                                             vmem_limit_bytes=VMEM_LIMIT),
        name="merge_out_ln",
    )(x2, attn, ys, w_in, w_in, w_attn_up, w_ssm_up, w_o,
      ln_g.reshape(1, d).astype(F32), ln_b.reshape(1, d).astype(F32))


def _layer(h, w_in, lam_re, lam_im, log_dt, b_re, b_im, c_re, c_im, d_skip,
           w_glu, b_glu, w_attn_up, w_ssm_up, w_o, ln_g, ln_b, alpha):
    bsz, seq, d = h.shape
    assert seq % SPAN == 0 and all(w // r == BLOCK for w, r in GROUP_PATTERNS)
    assert tuple(r for _, r in GROUP_PATTERNS) == (1, PHASE_SPLIT, PHASE_SPLIT * PHASE_SPLIT)
    x2 = h.reshape(bsz * seq, d)
    w_in = w_in.astype(BF16)

    qkv = _qkv_proj(x2, w_in, bsz, seq)
    attn = _attention(qkv, bsz, seq)
    ssm_w = _ssm_weights(lam_re, lam_im, log_dt, b_re, b_im, c_re, c_im)
    ys = _ssm_branch(h, w_in, ssm_w, d_skip, w_glu.astype(BF16), b_glu)
    out = _merge(x2, attn, ys, w_in, w_attn_up.astype(BF16), w_ssm_up.astype(BF16),
                 w_o.astype(BF16), ln_g, ln_b, alpha)
    return out.reshape(bsz, seq, d)


def kernel(x, w_in, lam_re, lam_im, log_dt, b_re, b_im, c_re, c_im, d_skip,
           w_glu, b_glu, w_attn_up, w_ssm_up, w_o, ln_g, ln_b):
    depth = w_in.shape[0]
    alpha = (2.0 * depth) ** 0.25
    h = x
    for layer in range(depth):
        h = _layer(h, w_in[layer], lam_re[layer], lam_im[layer], log_dt[layer], b_re[layer], b_im[layer],
                   c_re[layer], c_im[layer], d_skip[layer], w_glu[layer], b_glu[layer], w_attn_up[layer],
                   w_ssm_up[layer], w_o[layer], ln_g[layer], ln_b[layer], alpha)
    return h
```

```python
import functools
import math

import jax
import jax.numpy as jnp
import numpy as np
from jax import lax
from jax.experimental import pallas as pl
from jax.experimental.pallas import tpu as pltpu

F32 = jnp.float32
BF16 = jnp.bfloat16

HEAD_DIM = 64
N_SLOTS = 8
GROUP_PATTERNS = ((128, 1), (512, 4), (2048, 16))
N_GROUPS = len(GROUP_PATTERNS)
ATTN_WIDTH = N_SLOTS * HEAD_DIM
QKV_WIDTH = N_GROUPS * ATTN_WIDTH
BLOCK = 128
SPAN = GROUP_PATTERNS[-1][1] * BLOCK
UNITS = SPAN // BLOCK
SSM_WIDTH = 512
SSM_GROUP = 16
SSM_GROUPS = SSM_WIDTH // SSM_GROUP
SSM_STATE = 64
LN_EPS = 1e-5
MASK_VALUE = -1e30
LOG2E = math.log2(math.e)
Q_SCALE = LOG2E / math.sqrt(HEAD_DIM)

LANES = 128
HEAD_PAIRS = ATTN_WIDTH // LANES
PHASE_SPLIT = 4
QUARTER = SPAN // PHASE_SPLIT
BLOCK_ROW_TOKEN = [PHASE_SPLIT * k + ph for ph in range(PHASE_SPLIT) for k in range(BLOCK // PHASE_SPLIT)]
SSM_CHUNK = PHASE_SPLIT
SSM_SLABS = SSM_WIDTH // LANES
GROUPS_PER_SLAB = LANES // SSM_GROUP
SLAB_STATE = GROUPS_PER_SLAB * SSM_STATE
LN_CHUNKS = 4
V7X_VMEM_BYTES = 64 * 1024 * 1024
VMEM_LIMIT = V7X_VMEM_BYTES - 8 * 1024 * 1024


def _const_spec(shape):
    nd = len(shape)
    return pl.BlockSpec(shape, lambda *_: (0,) * nd, pipeline_mode=pl.Buffered(1))


def _col_block_spec(rows, width, start):
    assert start % width == 0
    return pl.BlockSpec((rows, width), lambda *_: (0, start // width), pipeline_mode=pl.Buffered(1))


def _sigmoid(x):
    return 0.5 * jnp.tanh(0.5 * x) + 0.5


def _silu(x):
    return x * _sigmoid(x)


def _lane_tiles(x):
    return [x[:, c * LANES:(c + 1) * LANES] for c in range(x.shape[1] // LANES)]


def _qkv_kernel(x_ref, w_ref, *refs):
    out_refs, x_stage, stage = refs[:-2], refs[-2], refs[-1]
    tm = x_ref.shape[0]
    sub = tm // PHASE_SPLIT
    x = x_ref[...]
    for c, tile in enumerate(_lane_tiles(x)):
        x_stage[c] = tile
    xb = {1: x.astype(BF16),
          PHASE_SPLIT: jnp.concatenate(
              [jnp.concatenate([x_stage[c, pl.ds(q, sub, stride=PHASE_SPLIT), :] for c in range(x_stage.shape[0])],
                               axis=1) for q in range(PHASE_SPLIT)], axis=0).astype(BF16)}
    piece = BLOCK // PHASE_SPLIT
    xb_block_split = jnp.concatenate(
        [xb[PHASE_SPLIT][ph * sub + blk * piece:ph * sub + (blk + 1) * piece]
         for blk in range(tm // BLOCK) for ph in range(PHASE_SPLIT)], axis=0)
    for kind in range(3):
        col = kind * QKV_WIDTH
        first = jnp.dot(xb_block_split if kind == 0 else xb[1], w_ref[:, col:col + ATTN_WIDTH],
                        preferred_element_type=F32)
        rest = jnp.dot(xb[PHASE_SPLIT], w_ref[:, col + ATTN_WIDTH:col + QKV_WIDTH], preferred_element_type=F32)
        by_group = [first] + [rest[:, (g - 1) * ATTN_WIDTH:g * ATTN_WIDTH] for g in range(1, N_GROUPS)]
        for g, (_, r) in enumerate(GROUP_PATTERNS):
            res = by_group[g]
            if kind == 0:
                res = res * Q_SCALE
            o_ref = out_refs[kind * N_GROUPS + g]
            if r == 1:
                o_ref[0] = res.astype(BF16)
            elif r == PHASE_SPLIT:
                for q in range(r):
                    o_ref[q] = res[q * sub:(q + 1) * sub].astype(BF16)
            else:
                for c, tile in enumerate(_lane_tiles(res)):
                    stage[c] = tile
                for q in range(PHASE_SPLIT):
                    for s in range(PHASE_SPLIT):
                        o_ref[q + PHASE_SPLIT * s] = jnp.concatenate(
                            [stage[c, pl.ds(q * sub + s, sub // PHASE_SPLIT, stride=PHASE_SPLIT), :]
                             for c in range(HEAD_PAIRS)], axis=1).astype(BF16)


def _qkv_proj(x2, w_in, bsz, seq, tm=1024):
    n, d = x2.shape
    tiles = seq // tm
    out_specs, out_shape = [], []
    for _ in range(3):
        for _, r in GROUP_PATTERNS:
            out_specs.append(pl.BlockSpec((None, r, tm // r, ATTN_WIDTH),
                                          lambda i: (i // tiles, 0, i % tiles, 0)))
            out_shape.append(jax.ShapeDtypeStruct((bsz, r, seq // r, ATTN_WIDTH), BF16))
    return pl.pallas_call(
        _qkv_kernel,
        grid=(n // tm,),
        in_specs=[pl.BlockSpec((tm, d), lambda i: (i, 0)), _col_block_spec(d, 3 * QKV_WIDTH, 0)],
        out_specs=out_specs,
        out_shape=out_shape,
        scratch_shapes=[pltpu.VMEM((d // LANES, tm, LANES), F32), pltpu.VMEM((HEAD_PAIRS, tm, LANES), F32)],
        compiler_params=pltpu.CompilerParams(dimension_semantics=("arbitrary",),
                                             vmem_limit_bytes=VMEM_LIMIT),
        name="qkv_proj",
    )(x2, w_in)


def _attn_bias():
    slopes = 2.0 ** (-8.0 * np.arange(1, N_SLOTS + 1) / N_SLOTS)
    qi = np.arange(BLOCK)[:, None]
    kj = np.arange(2 * BLOCK)[None, :]
    dist = BLOCK + qi - kj
    valid = (dist >= 0) & (dist <= BLOCK)
    valid = np.stack([valid, valid & (kj >= BLOCK)])
    out = []
    for _, r in GROUP_PATTERNS:
        alibi = (-LOG2E * slopes)[:, None, None, None] * (dist * r)[None, None]
        bias = np.where(valid[None], alibi, MASK_VALUE)
        out.append(bias[:, :, BLOCK_ROW_TOKEN, :] if r == 1 else bias)
    return jnp.asarray(np.stack(out).astype(np.float32))


def _attn_kernel(*refs):
    ins, bias_ref, out_ref = refs[:5 * N_GROUPS], refs[5 * N_GROUPS], refs[5 * N_GROUPS + 1]
    o_stage, l_stage, m_stage = refs[5 * N_GROUPS + 2:]
    first_span = pl.program_id(2) == 0
    lane = lax.broadcasted_iota(jnp.int32, (BLOCK, LANES), 1)
    low = lane < HEAD_DIM

    for g, (_, r) in enumerate(GROUP_PATTERNS):
        q_ref, kc_ref, kp_ref, vc_ref, vp_ref = ins[5 * g:5 * g + 5]
        nb = UNITS // r

        for p, n in ((p, n) for p in range(r) for n in range(nb)):
            q = q_ref[p, n * BLOCK:(n + 1) * BLOCK, :]
            if n == 0:
                kcat = jnp.concatenate([kp_ref[p], kc_ref[p, :BLOCK, :]], axis=0)
                vcat = jnp.concatenate([vp_ref[p], vc_ref[p, :BLOCK, :]], axis=0)
                variant = jnp.where(first_span, 1, 0)
            else:
                kcat = kc_ref[p, (n - 1) * BLOCK:(n + 1) * BLOCK, :]
                vcat = vc_ref[p, (n - 1) * BLOCK:(n + 1) * BLOCK, :]
                variant = 0
            v_ext = jnp.concatenate([vcat, jnp.ones_like(vcat)], axis=1)
            qm = jnp.concatenate([jnp.where(low, q, jnp.zeros_like(q)),
                                  jnp.where(low, jnp.zeros_like(q), q)], axis=0)
            s_both = lax.dot_general(qm, kcat, (((1,), (1,)), ((), ())), preferred_element_type=F32)
            es, tops = [], []
            for hh in range(2):
                s = s_both[hh * BLOCK:(hh + 1) * BLOCK] + bias_ref[g, hh, variant]
                m = jnp.max(s, axis=1, keepdims=True)
                es.append(jnp.exp2(s - m).astype(BF16))
                tops.append(m)
            pv = jnp.dot(jnp.concatenate(es, axis=0), v_ext, preferred_element_type=F32)
            staged = ((o_stage, jnp.where(low, pv[:BLOCK, :LANES], pv[BLOCK:, :LANES])),
                      (l_stage, jnp.where(low, pv[:BLOCK, LANES:], pv[BLOCK:, LANES:])),
                      (m_stage, jnp.where(low, tops[0], tops[1])))
            for stage, val in staged:
                if r == 1:
                    piece = BLOCK // PHASE_SPLIT
                    for ph in range(PHASE_SPLIT):
                        dst = ph * QUARTER + n * piece
                        stage[g, dst:dst + piece, :] = val[ph * piece:(ph + 1) * piece]
                elif r == PHASE_SPLIT:
                    dst = p * QUARTER + n * BLOCK
                    stage[g, dst:dst + BLOCK, :] = val
                else:
                    start = (p % PHASE_SPLIT) * QUARTER + p // PHASE_SPLIT
                    stage[g, pl.ds(start, BLOCK, stride=PHASE_SPLIT), :] = val

    def mix(i, carry):
        rows = pl.ds(pl.multiple_of(i * BLOCK, BLOCK), BLOCK)

        tops = [m_stage[g, rows, :] for g in range(N_GROUPS)]
        top = functools.reduce(jnp.maximum, tops)
        weights = [jnp.exp2(tops[g] - top) for g in range(N_GROUPS)]
        num = sum(weights[g] * o_stage[g, rows, :] for g in range(N_GROUPS))
        den = sum(weights[g] * l_stage[g, rows, :] for g in range(N_GROUPS))
        start = (i % PHASE_SPLIT) * (PHASE_SPLIT * BLOCK) + i // PHASE_SPLIT
        out_ref[pl.ds(start, BLOCK, stride=PHASE_SPLIT), :] = num / den
        return carry

    lax.fori_loop(0, UNITS, mix, 0, unroll=4)


def _attention(qkv, bsz, seq):
    spans = seq // SPAN
    in_specs, args = [], []
    for g, (_, r) in enumerate(GROUP_PATTERNS):
        nb = UNITS // r
        cur = pl.BlockSpec((None, r, SPAN // r, LANES), lambda hp, b, s: (b, 0, s, hp))
        prev = pl.BlockSpec((None, r, BLOCK, LANES),
                            lambda hp, b, s, nb=nb: (b, 0, jnp.maximum(nb * s - 1, 0), hp))
        q, k, v = qkv[g], qkv[N_GROUPS + g], qkv[2 * N_GROUPS + g]
        in_specs += [cur, cur, prev, cur, prev]
        args += [q, k, k, v, v]
    in_specs.append(pl.BlockSpec((N_GROUPS, 2, 2, BLOCK, 2 * BLOCK), lambda hp, b, s: (0, hp, 0, 0, 0)))
    scratch = [pltpu.VMEM((N_GROUPS, SPAN, LANES), F32)] * 3
    return pl.pallas_call(
        _attn_kernel,
        grid=(HEAD_PAIRS, bsz, spans),
        in_specs=in_specs,
        out_specs=pl.BlockSpec((SPAN, LANES), lambda hp, b, s: (b * spans + s, hp)),
        out_shape=jax.ShapeDtypeStruct((bsz * seq, ATTN_WIDTH), F32),
        scratch_shapes=scratch,
        compiler_params=pltpu.CompilerParams(dimension_semantics=("arbitrary",) * 3,
                                             vmem_limit_bytes=VMEM_LIMIT),
        name="band_attn",
    )(*args, _attn_bias())


def _ssm_weights(lam_re, lam_im, log_dt, b_re, b_im, c_re, c_im):
    t = SSM_CHUNK
    lr, li = lam_re.astype(F32), lam_im.astype(F32)
    dt = jnp.exp(log_dt.astype(F32))[:, None]
    mag = jnp.exp(lr * dt)
    ab_re, ab_im = mag * jnp.cos(li * dt), mag * jnp.sin(li * dt)
    nr, ni = ab_re - 1.0, ab_im
    den = lr * lr + li * li
    coef_re = (nr * lr + ni * li) / den
    coef_im = (ni * lr - nr * li) / den
    br, bi = b_re.astype(F32), b_im.astype(F32)
    bb_re = coef_re[..., None] * br - coef_im[..., None] * bi
    bb_im = coef_re[..., None] * bi + coef_im[..., None] * br
    pw_re, pw_im = [jnp.ones_like(ab_re)], [jnp.zeros_like(ab_im)]
    for _ in range(t):
        pr, pi = pw_re[-1], pw_im[-1]
        pw_re.append(pr * ab_re - pi * ab_im)
        pw_im.append(pr * ab_im + pi * ab_re)
    pw_re, pw_im = jnp.stack(pw_re), jnp.stack(pw_im)
    cr, ci = c_re.astype(F32), c_im.astype(F32)
    n_ch = SSM_GROUPS * SSM_GROUP

    xb_re = pw_re[:t, :, :, None] * bb_re[None] - pw_im[:t, :, :, None] * bb_im[None]
    xb_im = pw_re[:t, :, :, None] * bb_im[None] + pw_im[:t, :, :, None] * bb_re[None]
    xb = jnp.stack([xb_re, xb_im]).transpose(0, 1, 2, 4, 3).reshape(2, t, n_ch, SSM_STATE)
    xb = jnp.concatenate([xb, xb], axis=-1)
    ca_re = cr[None] * pw_re[:, :, None, :] - ci[None] * pw_im[:, :, None, :]
    ca_im = cr[None] * pw_im[:, :, None, :] + ci[None] * pw_re[:, :, None, :]
    ca = jnp.stack([ca_re, -ca_im]).transpose(0, 1, 4, 2, 3).reshape(2, t + 1, SSM_STATE, n_ch)

    w_y, w_st = pl.pallas_call(
        _ssm_expand_kernel,
        grid=(SSM_SLABS,),
        in_specs=[pl.BlockSpec((2, t, LANES, LANES), lambda j: (0, 0, j, 0)),
                  pl.BlockSpec((2, t + 1, SSM_STATE, LANES), lambda j: (0, 0, 0, j))],
        out_specs=[pl.BlockSpec((None, t * LANES + 2 * SLAB_STATE, t * LANES), lambda j: (j, 0, 0)),
                   pl.BlockSpec((None, t * LANES, 2 * SLAB_STATE), lambda j: (j, 0, 0))],
        out_shape=[jax.ShapeDtypeStruct((SSM_SLABS, t * LANES + 2 * SLAB_STATE, t * LANES), BF16),
                   jax.ShapeDtypeStruct((SSM_SLABS, t * LANES, 2 * SLAB_STATE), BF16)],
        compiler_params=pltpu.CompilerParams(dimension_semantics=("arbitrary",),
                                             vmem_limit_bytes=VMEM_LIMIT),
        name="s5_expand",
    )(xb, ca)
    a_t = jnp.stack([pw_re[t].reshape(-1), pw_im[t].reshape(-1)])
    return w_y, w_st, a_t


def _ssm_expand_kernel(xb_ref, ca_ref, wy_ref, wst_ref):
    t = SSM_CHUNK
    keep_st = (lax.broadcasted_iota(jnp.int32, (LANES, SLAB_STATE), 0) // SSM_GROUP
               == lax.broadcasted_iota(jnp.int32, (LANES, SLAB_STATE), 1) // SSM_STATE)
    keep_out = (lax.broadcasted_iota(jnp.int32, (SLAB_STATE, LANES), 0) // SSM_STATE
                == lax.broadcasted_iota(jnp.int32, (SLAB_STATE, LANES), 1) // SSM_GROUP)

    def to_state(d):
        parts = [jnp.where(keep_st, jnp.concatenate([xb_ref[ri, d]] * (SLAB_STATE // LANES), axis=1), 0.0)
                 for ri in range(2)]
        return jnp.concatenate(parts, axis=1)

    def from_state(d):
        parts = [jnp.where(keep_out, jnp.concatenate([ca_ref[ri, d]] * GROUPS_PER_SLAB, axis=0), 0.0)
                 for ri in range(2)]
        return jnp.concatenate(parts, axis=0)

    c0 = from_state(0)
    zero = jnp.zeros((LANES, LANES), BF16)
    for d in range(t):
        st = to_state(d)
        sg = t - 1 - d
        wst_ref[sg * LANES:(sg + 1) * LANES, :] = st.astype(BF16)
        taps = jnp.dot(st, c0, preferred_element_type=F32, precision=lax.Precision.HIGHEST).astype(BF16)
        for sg in range(t - d):
            wy_ref[sg * LANES:(sg + 1) * LANES, (sg + d) * LANES:(sg + d + 1) * LANES] = taps
        if d:
            for tau in range(t - d):
                wy_ref[(tau + d) * LANES:(tau + d + 1) * LANES, tau * LANES:(tau + 1) * LANES] = zero
    for tau in range(t):
        wy_ref[t * LANES:, tau * LANES:(tau + 1) * LANES] = from_state(tau + 1).astype(BF16)


def _ssm_kernel(x_ref, wuz_ref, wy_ref, wst_ref, at_ref, dskip_ref, wglu_ref, bglu_ref,
                out_ref, u_st, z_scr, y_st, carry_scr, *state_scrs):
    s_scrs, xp_scrs = state_scrs[:SSM_SLABS], state_scrs[SSM_SLABS:]
    bsz, tl, d_model = x_ref.shape
    t = SSM_CHUNK
    ch = tl // t
    rows = bsz * tl
    crows = bsz * ch
    tiles_per_slab = SLAB_STATE // LANES
    n_tiles = SSM_SLABS * tiles_per_slab
    pitch = s_scrs[0].shape[1] // bsz

    @pl.when(pl.program_id(0) == 0)
    def _():
        carry_scr[...] = jnp.zeros_like(carry_scr)

    xb = x_ref[...].reshape(rows, d_model).astype(BF16)
    uz = jnp.dot(xb, wuz_ref[...], preferred_element_type=F32)
    for c, tile in enumerate(_lane_tiles(uz[:, :SSM_WIDTH])):
        u_st[c] = tile
    z_scr[...] = uz[:, SSM_WIDTH:]

    def chunk_lhs(j):
        return jnp.concatenate([u_st[j, pl.ds(sg, crows, stride=t), :] for sg in range(t)], axis=1).astype(BF16)

    n_slab_tiles = 2 * tiles_per_slab

    def state_increments(j):
        st = jnp.dot(chunk_lhs(j), wst_ref[j], preferred_element_type=F32)
        for c, tile in enumerate(_lane_tiles(st)):
            for b in range(bsz):
                s_scrs[j][c, b * pitch:b * pitch + ch, :] = tile[b * ch:(b + 1) * ch]

    def recurrence(j):
        s_scr, xp_scr = s_scrs[j], xp_scrs[j]
        for c_r in range(tiles_per_slab):
            c_i = tiles_per_slab + c_r
            g_r, g_i = j * tiles_per_slab + c_r, n_tiles + j * tiles_per_slab + c_r
            a_re, a_im = at_ref[g_r:g_r + 1, :], at_ref[g_i:g_i + 1, :]
            x_re, x_im = carry_scr[g_r], carry_scr[g_i]
            for k in range(ch):
                sel = pl.ds(k, bsz, stride=pitch)
                xp_scr[c_r, sel, :] = x_re
                xp_scr[c_i, sel, :] = x_im
                x_re, x_im = (a_re * x_re - a_im * x_im + s_scr[c_r, sel, :],
                              a_re * x_im + a_im * x_re + s_scr[c_i, sel, :])
            carry_scr[g_r] = x_re
            carry_scr[g_i] = x_im

    def response(j):
        xin = [jnp.concatenate([xp_scrs[j][c, b * pitch:b * pitch + ch, :] for b in range(bsz)],
                               axis=0).astype(BF16) for c in range(n_slab_tiles)]
        yj = jnp.dot(jnp.concatenate([chunk_lhs(j)] + xin, axis=1), wy_ref[j], preferred_element_type=F32)
        for tau, tile in enumerate(_lane_tiles(yj)):
            y_st[j, pl.ds(tau, crows, stride=t), :] = tile

    state_increments(0)
    for j in range(SSM_SLABS):
        if j + 1 < SSM_SLABS:
            state_increments(j + 1)
        recurrence(j)
        if j:
            response(j - 1)
    response(SSM_SLABS - 1)

    for b in range(bsz):
        rs = slice(b * tl, (b + 1) * tl)
        y = jnp.concatenate([y_st[c, rs, :] for c in range(SSM_SLABS)], axis=1)
        u = jnp.concatenate([u_st[c, rs, :] for c in range(SSM_SLABS)], axis=1)
        y = jax.nn.gelu(y + dskip_ref[...] * u)
        gate = _sigmoid(jnp.dot(y.astype(BF16), wglu_ref[...], preferred_element_type=F32) + bglu_ref[...])
        out_ref[b] = (y * gate * _silu(z_scr[rs, :])).astype(BF16)


def _ssm_branch(x, w_in, ssm_w, d_skip, w_glu, b_glu, tl=128):
    bsz, seq, d = x.shape
    u_col = 3 * QKV_WIDTH + ATTN_WIDTH
    t = SSM_CHUNK
    rows, ch = bsz * tl, tl // t
    pitch = -(-ch // PHASE_SPLIT) * PHASE_SPLIT
    pitch += PHASE_SPLIT * (1 - (pitch // PHASE_SPLIT) % 2)
    w_y, w_st, a_t = ssm_w
    n_tiles2 = 2 * SSM_SLABS * SLAB_STATE // LANES
    a_t = a_t.reshape(n_tiles2, LANES)
    out = pl.pallas_call(
        _ssm_kernel,
        grid=(seq // tl,),
        in_specs=[pl.BlockSpec((bsz, tl, d), lambda i: (0, i, 0)),
                  _col_block_spec(d, 2 * SSM_WIDTH, u_col),
                  _const_spec(w_y.shape),
                  _const_spec(w_st.shape), _const_spec(a_t.shape), _const_spec((1, SSM_WIDTH)),
                  _const_spec(w_glu.shape), _const_spec((1, SSM_WIDTH))],
        out_specs=pl.BlockSpec((bsz, tl, SSM_WIDTH), lambda i: (0, i, 0)),
        out_shape=jax.ShapeDtypeStruct((bsz, seq, SSM_WIDTH), BF16),
        scratch_shapes=[pltpu.VMEM((SSM_SLABS, rows, LANES), F32),
                        pltpu.VMEM((rows, SSM_WIDTH), F32),
                        pltpu.VMEM((SSM_SLABS, rows, LANES), F32),
                        pltpu.VMEM((n_tiles2, bsz, LANES), F32)]
        + [pltpu.VMEM((n_tiles2 // SSM_SLABS, bsz * pitch, LANES), F32)] * (2 * SSM_SLABS),
        compiler_params=pltpu.CompilerParams(dimension_semantics=("arbitrary",),
                                             vmem_limit_bytes=VMEM_LIMIT),
        name="s5_branch",
    )(x, w_in, w_y, w_st, a_t, d_skip.reshape(1, SSM_WIDTH).astype(F32),
      w_glu, b_glu.reshape(1, SSM_WIDTH).astype(F32))
    return out.reshape(bsz * seq, SSM_WIDTH)


def _merge_kernel(alpha, x_ref, attn_ref, ys_ref, wz_ref, wg_ref, wa_ref, ws_ref, wo_ref, g_ref, b_ref, out_ref):
    tm, d_model = x_ref.shape
    y_s = jnp.dot(ys_ref[...], ws_ref[...], preferred_element_type=F32)
    xb = x_ref[...].astype(BF16)
    z_a = jnp.dot(xb, wz_ref[...], preferred_element_type=F32)
    gates = jnp.dot(xb, wg_ref[...], preferred_element_type=F32)
    gate_a = gates[:, :d_model]
    gate_s = gates[:, d_model:]
    attn = attn_ref[...].astype(F32) * _silu(z_a)
    y_a = jnp.dot(attn.astype(BF16), wa_ref[...], preferred_element_type=F32)
    merged = (_sigmoid(gate_a) * y_a + _sigmoid(gate_s) * y_s).astype(BF16)
    chunk = tm // LN_CHUNKS
    for i in range(LN_CHUNKS):
        rows = slice(i * chunk, (i + 1) * chunk)
        out = jnp.dot(merged[rows], wo_ref[...], preferred_element_type=F32)
        h = alpha * x_ref[rows, :] + out
        mu = jnp.mean(h, axis=-1, keepdims=True)
        hc = h - mu
        var = jnp.mean(hc * hc, axis=-1, keepdims=True)
        out_ref[rows, :] = hc * lax.rsqrt(var + LN_EPS) * g_ref[...] + b_ref[...]


def _merge(x2, attn, ys, w_in, w_attn_up, w_ssm_up, w_o, ln_g, ln_b, alpha, tm=1024):
    n, d = x2.shape
    row = lambda width: pl.BlockSpec((tm, width), lambda i: (i, 0))
    z_col = 3 * QKV_WIDTH
    gate_col = z_col + ATTN_WIDTH + 2 * SSM_WIDTH
    return pl.pallas_call(
        functools.partial(_merge_kernel, alpha),
        grid=(n // tm,),
        in_specs=[row(d), row(ATTN_WIDTH), row(SSM_WIDTH),
                  _col_block_spec(d, ATTN_WIDTH, z_col), _col_block_spec(d, 2 * d, gate_col),
                  _const_spec(w_attn_up.shape), _const_spec(w_ssm_up.shape),
                  _const_spec(w_o.shape), _const_spec((1, d)), _const_spec((1, d))],
        out_specs=row(d),
        out_shape=jax.ShapeDtypeStruct((n, d), F32),
        compiler_params=pltpu.CompilerParams(dimension_semantics=("arbitrary",),
                                             vmem_limit_bytes=VMEM_LIMIT),
        name="merge_out_ln",
    )(x2, attn, ys, w_in, w_in, w_attn_up, w_ssm_up, w_o,
      ln_g.reshape(1, d).astype(F32), ln_b.reshape(1, d).astype(F32))


def _layer(h, w_in, lam_re, lam_im, log_dt, b_re, b_im, c_re, c_im, d_skip,
           w_glu, b_glu, w_attn_up, w_ssm_up, w_o, ln_g, ln_b, alpha):
    bsz, seq, d = h.shape
    assert seq % SPAN == 0 and all(w // r == BLOCK for w, r in GROUP_PATTERNS)
    assert tuple(r for _, r in GROUP_PATTERNS) == (1, PHASE_SPLIT, PHASE_SPLIT * PHASE_SPLIT)
    x2 = h.reshape(bsz * seq, d)
    w_in = w_in.astype(BF16)

    qkv = _qkv_proj(x2, w_in, bsz, seq)
    attn = _attention(qkv, bsz, seq)
    ssm_w = _ssm_weights(lam_re, lam_im, log_dt, b_re, b_im, c_re, c_im)
    ys = _ssm_branch(h, w_in, ssm_w, d_skip, w_glu.astype(BF16), b_glu)
    out = _merge(x2, attn, ys, w_in, w_attn_up.astype(BF16), w_ssm_up.astype(BF16),
                 w_o.astype(BF16), ln_g, ln_b, alpha)
    return out.reshape(bsz, seq, d)


def kernel(x, w_in, lam_re, lam_im, log_dt, b_re, b_im, c_re, c_im, d_skip,
           w_glu, b_glu, w_attn_up, w_ssm_up, w_o, ln_g, ln_b):
    depth = w_in.shape[0]
    alpha = (2.0 * depth) ** 0.25
    h = x
    for layer in range(depth):
        h = _layer(h, w_in[layer], lam_re[layer], lam_im[layer], log_dt[layer], b_re[layer], b_im[layer],
                   c_re[layer], c_im[layer], d_skip[layer], w_glu[layer], b_glu[layer], w_attn_up[layer],
                   w_ssm_up[layer], w_o[layer], ln_g[layer], ln_b[layer], alpha)
    return h
```
